```python
import jax, jax.numpy as jnp
from jax import lax
import numpy as np

D_MODEL = 2048
BATCH = 16
SEQ = 2048
DEPTH = 1

D_MIX = D_MODEL
RET_HEADS = 8
RET_HEAD_DIM = 128
RET_WIDTH = RET_HEADS * RET_HEAD_DIM
RET_CHUNK = 128
ROPE_BASE = 10000.0
LRU_WIDTH = D_MIX - RET_WIDTH
LRU_BLOCKS = 8
LRU_BLOCK_DIM = LRU_WIDTH // LRU_BLOCKS
CONV_WIDTH = 4
LRU_C = 8.0
PROJ_IN = 4 * RET_WIDTH + 2 * LRU_WIDTH
N_GROUPS = 4
EXPERTS_PER_GROUP = 8
N_EXPERTS = N_GROUPS * EXPERTS_PER_GROUP
TOP_K = 2
D_EXPERT = D_MODEL // 2
MOE_BLOCK = 128
EPS = 1e-6

kernel_name = "hymba_retention_rglru_hmoe"


def rms_norm(x, g):
    xf = x.astype(jnp.float32)
    y = xf * lax.rsqrt(jnp.mean(xf * xf, axis=-1, keepdims=True) + EPS)
    return (y * g.astype(jnp.float32)).astype(x.dtype)


def rope(x, pos):
    half = x.shape[-1] // 2
    inv = ROPE_BASE ** (-jnp.arange(half, dtype=jnp.float32) / half)
    ang = pos.astype(jnp.float32)[:, None] * inv[None, :]
    cos = jnp.cos(ang)[None, :, None, :]
    sin = jnp.sin(ang)[None, :, None, :]
    xf = x.astype(jnp.float32)
    x1, x2 = xf[..., :half], xf[..., half:]
    return jnp.concatenate([x1 * cos - x2 * sin, x1 * sin + x2 * cos], axis=-1).astype(x.dtype)


def retention_chunkwise(q, k, v):
    B, S, H, d = q.shape
    C = RET_CHUNK
    N = S // C
    log_g = jnp.log1p(-(2.0 ** (-5.0 - jnp.arange(H, dtype=jnp.float32))))
    idx = jnp.arange(C, dtype=jnp.float32)
    rel = idx[:, None] - idx[None, :]
    decay_mask = jnp.where(rel[None] >= 0,
                           jnp.exp(jnp.maximum(rel, 0.0)[None] * log_g[:, None, None]), 0.0)
    q_dec = jnp.exp((idx + 1.0)[None, :] * log_g[:, None])
    k_dec = jnp.exp((C - 1.0 - idx)[None, :] * log_g[:, None])
    chunk_dec = jnp.exp(C * log_g)
    qc = q.reshape(B, N, C, H, d)
    kc = k.reshape(B, N, C, H, d)
    vc = v.reshape(B, N, C, H, d)
    scores = jnp.einsum('bnihd,bnjhd->bnhij', qc, kc).astype(jnp.float32) * decay_mask
    o_inner = jnp.einsum('bnhij,bnjhe->bnihe', scores, vc.astype(jnp.float32))
    kv = jnp.einsum('bnjhd,hj,bnjhe->nbhde', kc.astype(jnp.float32), k_dec,
                    vc.astype(jnp.float32))

    def step(state, kv_n):
        return chunk_dec[None, :, None, None] * state + kv_n, state

    _, s_prev = lax.scan(step, jnp.zeros((B, H, d, d), jnp.float32), kv)
    o_cross = jnp.einsum('bnihd,hi,nbhde->bnihe', qc.astype(jnp.float32), q_dec, s_prev)
    return (o_inner + o_cross).reshape(B, S, H, d)


def causal_depthwise_conv(u, w, b):
    K = w.shape[0]
    S = u.shape[1]
    up = jnp.pad(u, ((0, 0), (K - 1, 0), (0, 0)))
    out = b
    for kk in range(K):
        out = out + up[:, kk:kk + S] * w[kk]
    return out


def rg_lru(xc, w_rg, b_rg, w_ig, b_ig, lam):
    B, S, W = xc.shape
    xb = xc.reshape(B, S, LRU_BLOCKS, LRU_BLOCK_DIM)
    r = jax.nn.sigmoid(jnp.einsum('bsnc,ncd->bsnd', xb, w_rg).reshape(B, S, W) + b_rg)
    i = jax.nn.sigmoid(jnp.einsum('bsnc,ncd->bsnd', xb, w_ig).reshape(B, S, W) + b_ig)
    log_a = (-LRU_C * r * jax.nn.softplus(-lam)).astype(jnp.float32)
    a = jnp.exp(log_a)
    bx = (jnp.sqrt(-jnp.expm1(2.0 * log_a)) * (i * xc)).astype(jnp.float32)

    def combine(left, right):
        a1, b1 = left
        a2, b2 = right
        return a1 * a2, a2 * b1 + b2

    _, h = lax.associative_scan(combine, (a, bx), axis=1)
    return h


def hierarchical_moe(h, w_group, b_group, w_router, b_router, w_gate, w_up, w_down):
    B, S, D = h.shape
    T = B * S
    xt = h.reshape(T, D)
    g_prob = jax.nn.softmax((xt @ w_group + b_group).astype(jnp.float32), axis=-1)
    g_p, g_idx = lax.top_k(g_prob, 1)
    e_logits = (xt @ w_router + b_router).astype(jnp.float32).reshape(T, N_GROUPS, EXPERTS_PER_GROUP)
    e_logits = jnp.take_along_axis(e_logits, g_idx[:, :, None], axis=1)[:, 0]
    e_top, e_loc = lax.top_k(e_logits, TOP_K)
    e_w = jax.nn.softmax(e_top, axis=-1) * g_p
    e_id = g_idx * EXPERTS_PER_GROUP + e_loc

    A = T * TOP_K
    flat_e = e_id.reshape(A)
    flat_w = e_w.reshape(A)
    order = jnp.argsort(flat_e)
    sorted_e = flat_e[order]
    counts = jnp.bincount(flat_e, length=N_EXPERTS)
    padded = (counts + MOE_BLOCK - 1) // MOE_BLOCK * MOE_BLOCK
    start = jnp.cumsum(counts) - counts
    pad_end = jnp.cumsum(padded)
    pad_start = pad_end - padded
    dest = pad_start[sorted_e] + (jnp.arange(A) - start[sorted_e])
    n_blocks = (A + N_EXPERTS * (MOE_BLOCK - 1) + MOE_BLOCK - 1) // MOE_BLOCK
    P = n_blocks * MOE_BLOCK
    tok = jnp.full((P,), T, jnp.int32).at[dest].set((order // TOP_K).astype(jnp.int32))
    wbuf = jnp.zeros((P,), jnp.float32).at[dest].set(flat_w[order])
    block_e = jnp.minimum(
        jnp.searchsorted(pad_end, jnp.arange(n_blocks) * MOE_BLOCK, side='right'), N_EXPERTS - 1)

    x_pad = jnp.concatenate([xt, jnp.zeros((1, D), xt.dtype)], axis=0)
    xs = x_pad[tok].reshape(n_blocks, MOE_BLOCK, D)

    def expert_block(args):
        xb, e = args
        hb = jax.nn.silu(xb @ w_gate[e]) * (xb @ w_up[e])
        return hb @ w_down[e]

    yb = lax.map(expert_block, (xs, block_e))
    y = yb.reshape(P, D) * wbuf[:, None].astype(yb.dtype)
    out = jnp.zeros((T + 1, D), yb.dtype).at[tok].add(y)[:T]
    return out.reshape(B, S, D)


def setup_inputs(seed: int = 0) -> dict:
    key = jax.random.key(seed)
    ks = jax.random.split(key, 24)
    f32 = jnp.float32
    L, D = DEPTH, D_MODEL

    def nrm(k, shape, scale):
        return jax.random.normal(k, shape, f32) * scale

    x = jax.random.normal(ks[0], (BATCH, SEQ, D), f32)
    norm_mix_g = 1.0 + nrm(ks[1], (L, D), 0.02)
    w_in = nrm(ks[2], (L, D, PROJ_IN), D ** -0.5)
    ret_norm_g = 1.0 + nrm(ks[3], (L, RET_WIDTH), 0.02)
    conv_w = nrm(ks[4], (L, CONV_WIDTH, LRU_WIDTH), CONV_WIDTH ** -0.5)
    conv_b = nrm(ks[5], (L, LRU_WIDTH), 0.02)
    w_rg = nrm(ks[6], (L, LRU_BLOCKS, LRU_BLOCK_DIM, LRU_BLOCK_DIM), LRU_BLOCK_DIM ** -0.5)
    b_rg = nrm(ks[7], (L, LRU_WIDTH), 0.02)
    w_ig = nrm(ks[8], (L, LRU_BLOCKS, LRU_BLOCK_DIM, LRU_BLOCK_DIM), LRU_BLOCK_DIM ** -0.5)
    b_ig = nrm(ks[9], (L, LRU_WIDTH), 0.02)
    u = jax.random.uniform(ks[10], (L, LRU_WIDTH), f32, 0.9, 0.999)
    a0 = u ** (1.0 / LRU_C)
    lru_lambda = jnp.log(a0) - jnp.log1p(-a0)
    lru_norm_g = 1.0 + nrm(ks[11], (L, LRU_WIDTH), 0.02)
    w_out = nrm(ks[12], (L, D_MIX, D), D_MIX ** -0.5)
    norm_ffn_g = 1.0 + nrm(ks[13], (L, D), 0.02)
    w_group = nrm(ks[14], (L, D, N_GROUPS), D ** -0.5)
    b_group = nrm(ks[15], (L, N_GROUPS), 0.01)
    w_router = nrm(ks[16], (L, D, N_EXPERTS), D ** -0.5)
    b_router = nrm(ks[17], (L, N_EXPERTS), 0.01)
    w_gate = nrm(ks[18], (L, N_EXPERTS, D, D_EXPERT), D ** -0.5)
    w_up = nrm(ks[19], (L, N_EXPERTS, D, D_EXPERT), D ** -0.5)
    w_down = nrm(ks[20], (L, N_EXPERTS, D_EXPERT, D), D_EXPERT ** -0.5)
    norm_final_g = 1.0 + nrm(ks[21], (D,), 0.02)
    return {"x": x, "norm_mix_g": norm_mix_g, "w_in": w_in, "ret_norm_g": ret_norm_g,
            "conv_w": conv_w, "conv_b": conv_b, "w_rg": w_rg, "b_rg": b_rg,
            "w_ig": w_ig, "b_ig": b_ig, "lru_lambda": lru_lambda, "lru_norm_g": lru_norm_g,
            "w_out": w_out, "norm_ffn_g": norm_ffn_g, "w_group": w_group, "b_group": b_group,
            "w_router": w_router, "b_router": b_router, "w_gate": w_gate, "w_up": w_up,
            "w_down": w_down, "norm_final_g": norm_final_g}


def reference(x, norm_mix_g, w_in, ret_norm_g, conv_w, conv_b, w_rg, b_rg, w_ig, b_ig,
              lru_lambda, lru_norm_g, w_out, norm_ffn_g, w_group, b_group, w_router, b_router,
              w_gate, w_up, w_down, norm_final_g):
    B, S, D = x.shape
    pos = jnp.arange(S)
    R = RET_WIDTH
    splits = [R, 2 * R, 3 * R, 4 * R, 4 * R + LRU_WIDTH]
    for l in range(DEPTH):
        h = rms_norm(x, norm_mix_g[l])
        proj = h @ w_in[l]
        q, k, v, g_ret, u, z = jnp.split(proj, splits, axis=-1)
        q = rope(q.reshape(B, S, RET_HEADS, RET_HEAD_DIM), pos) * (RET_HEAD_DIM ** -0.5)
        k = rope(k.reshape(B, S, RET_HEADS, RET_HEAD_DIM), pos)
        v = v.reshape(B, S, RET_HEADS, RET_HEAD_DIM)
        o = retention_chunkwise(q, k, v)
        mu = jnp.mean(o, axis=-1, keepdims=True)
        var = jnp.mean(jnp.square(o - mu), axis=-1, keepdims=True)
        o = (o - mu) * lax.rsqrt(var + EPS)
        o = o.reshape(B, S, R) * ret_norm_g[l].astype(jnp.float32)
        ret_out = (jax.nn.silu(g_ret.astype(jnp.float32)) * o).astype(x.dtype)
        uc = causal_depthwise_conv(u, conv_w[l], conv_b[l])
        hl = rg_lru(uc, w_rg[l], b_rg[l], w_ig[l], b_ig[l], lru_lambda[l])
        hl = hl.reshape(B, S, LRU_BLOCKS, LRU_BLOCK_DIM)
        hl = hl * lax.rsqrt(jnp.mean(hl * hl, axis=-1, keepdims=True) + EPS)
        hl = hl.reshape(B, S, LRU_WIDTH) * lru_norm_g[l].astype(jnp.float32)
        lru_out = (hl * jax.nn.gelu(z.astype(jnp.float32))).astype(x.dtype)
        mix = jnp.concatenate([ret_out, lru_out], axis=-1)
        x = x + mix @ w_out[l]
        h = rms_norm(x, norm_ffn_g[l])
        x = x + hierarchical_moe(h, w_group[l], b_group[l], w_router[l], b_router[l],
                                 w_gate[l], w_up[l], w_down[l])
    return rms_norm(x, norm_final_g)
```

```python
import functools
import math

import jax
import jax.numpy as jnp
from jax import lax
from jax.experimental import pallas as pl
from jax.experimental.pallas import tpu as pltpu

F32 = jnp.float32
BF16 = jnp.bfloat16
I32 = jnp.int32

EPS = 1e-6
RET_HEADS = 8
HEAD_DIM = 128
RET_CHUNK = 128
ROPE_BASE = 10000.0
LRU_BLOCKS = 8
LRU_BLOCK_DIM = 128
CONV_WIDTH = 4
LRU_C = 8.0
N_GROUPS = 4
EXPERTS_PER_GROUP = 8
N_EXPERTS = N_GROUPS * EXPERTS_PER_GROUP
TOP_K = 2

LANES = 128
SUBLANES = 8
VMEM_LIMIT = 56 * 1024 * 1024

INPROJ_TM = 1024
INPROJ_TN = 1024
LRU_ROWS = 128
OUTPROJ_TM = 256
MOE_ROWS = 256
COMBINE_TM = 256


def _params(sem):
    return pltpu.CompilerParams(dimension_semantics=sem, vmem_limit_bytes=VMEM_LIMIT)


def _inproj_body(x_ref, g_ref, w_ref, o_ref, h_scr):
    @pl.when(pl.program_id(1) == 0)
    def _():
        x = x_ref[...]
        ms = jnp.mean(x * x, axis=-1, keepdims=True)
        h_scr[...] = (x * lax.rsqrt(ms + EPS) * g_ref[...]).astype(BF16)

    o_ref[...] = jnp.dot(h_scr[...], w_ref[...], preferred_element_type=F32).astype(o_ref.dtype)


def _inproj(x2d, g, w_bf16, tm, tn):
    T, D = x2d.shape
    N = w_bf16.shape[1]
    return pl.pallas_call(
        _inproj_body,
        grid=(T // tm, N // tn),
        in_specs=[
            pl.BlockSpec((tm, D), lambda i, j: (i, 0)),
            pl.BlockSpec((1, D), lambda i, j: (0, 0)),
            pl.BlockSpec((D, tn), lambda i, j: (0, j)),
        ],
        out_specs=pl.BlockSpec((tm, tn), lambda i, j: (i, j)),
        out_shape=jax.ShapeDtypeStruct((T, N), BF16),
        scratch_shapes=[pltpu.VMEM((tm, D), BF16)],
        compiler_params=_params(("arbitrary", "arbitrary")),
        name="inproj",
    )(x2d, g, w_bf16)


def _retention_body(lg_ref, q_ref, k_ref, v_ref, g_ref, cos_ref, sin_ref, gn_ref, o_ref, s_scr):
    C = RET_CHUNK
    d = HEAD_DIM
    S = q_ref.shape[0]
    lg = lg_ref[pl.program_id(1)]
    row = lax.broadcasted_iota(I32, (C, d), 0).astype(F32)
    col = lax.broadcasted_iota(I32, (C, d), 1).astype(F32)
    rel = row - col
    mask = jnp.where(rel >= 0, jnp.exp(jnp.maximum(rel, 0.0) * lg), 0.0)
    q_dec = jnp.exp((row + 1.0) * lg)
    k_dec = jnp.exp((C - 1.0 - row) * lg)
    c_dec = jnp.exp(jnp.full((1, d), float(C), F32) * lg)
    scale = d ** -0.5
    s_scr[...] = jnp.zeros_like(s_scr)
    gn = gn_ref[...]

    def chunk(n, carry):
        sl = pl.ds(pl.multiple_of(n * C, C), C)
        cos = cos_ref[sl, :]
        sin = sin_ref[sl, :]
        q = q_ref[sl, :].astype(F32)
        k = k_ref[sl, :].astype(F32)
        v = v_ref[sl, :]
        q = (q * cos + pltpu.roll(q, d // 2, 1) * sin) * scale
        k = k * cos + pltpu.roll(k, d // 2, 1) * sin
        scores = lax.dot_general(q.astype(BF16), k.astype(BF16), (((1,), (1,)), ((), ())),
                                 preferred_element_type=F32) * mask
        o = jnp.dot(scores.astype(BF16), v, preferred_element_type=F32)
        state = s_scr[...]
        o = o + jnp.dot((q * q_dec).astype(BF16), state.astype(BF16), preferred_element_type=F32)
        kd_t = (k * k_dec).T.astype(BF16)
        s_scr[...] = c_dec * state + jnp.dot(kd_t, v, preferred_element_type=F32)
        mu = jnp.mean(o, axis=-1, keepdims=True)
        oc = o - mu
        var = jnp.mean(oc * oc, axis=-1, keepdims=True)
        on = oc * lax.rsqrt(var + EPS) * gn
        g = g_ref[sl, :].astype(F32)
        o_ref[sl, :] = ((g / (1.0 + jnp.exp(-g))) * on).astype(o_ref.dtype)
        return carry

    lax.fori_loop(0, S // C, chunk, 0)


def _retention(proj3, log_gamma, cos_t, sin_t, ret_norm_g):
    B, S, _ = proj3.shape
    H = RET_HEADS
    d = HEAD_DIM
    blk = lambda off: pl.BlockSpec((None, S, d), lambda b, h, off=off: (b, 0, off + h))
    return pl.pallas_call(
        _retention_body,
        grid=(B, H),
        in_specs=[
            pl.BlockSpec(memory_space=pltpu.SMEM),
            blk(0), blk(H), blk(2 * H), blk(3 * H),
            pl.BlockSpec((S, d), lambda b, h: (0, 0)),
            pl.BlockSpec((S, d), lambda b, h: (0, 0)),
            pl.BlockSpec((1, d), lambda b, h: (0, h)),
        ],
        out_specs=pl.BlockSpec((None, S, d), lambda b, h: (b, 0, h)),
        out_shape=jax.ShapeDtypeStruct((B, S, H * d), BF16),
        scratch_shapes=[pltpu.VMEM((d, d), F32)],
        compiler_params=_params(("arbitrary", "arbitrary")),
        name="retention",
    )(log_gamma, proj3, proj3, proj3, proj3, cos_t, sin_t, ret_norm_g)


def _rglru_body(u_ref, z_ref, cw_ref, cb_ref, wg_ref, brg_ref, big_ref, lam_ref, gn_ref, o_ref,
                carry_scr, halo_scr, h_scr):
    S = u_ref.shape[0]
    W = LRU_BLOCK_DIM
    R = LRU_ROWS
    K = CONV_WIDTH
    nl = -lam_ref[...]
    softplus = jnp.maximum(nl, 0.0) + jnp.log1p(jnp.exp(-jnp.abs(nl)))
    coef = -LRU_C * softplus
    cw = cw_ref[...]
    cb = cb_ref[...]
    brg = brg_ref[...]
    big = big_ref[...]
    gn = gn_ref[...]
    wg = wg_ref[...]
    carry_scr[...] = jnp.zeros_like(carry_scr)
    halo_scr[...] = jnp.zeros_like(halo_scr)
    row = lax.broadcasted_iota(I32, (R, W), 0)

    def chunk(c, carry):
        sl = pl.ds(pl.multiple_of(c * R, R), R)
        u = u_ref[sl, :].astype(F32)
        ext = jnp.concatenate([halo_scr[...], u], axis=0)
        halo_scr[...] = u[R - SUBLANES:, :]
        uc = cb + cw[K - 1:K, :] * u
        for j in range(1, K):
            uc = uc + cw[K - 1 - j:K - j, :] * pltpu.roll(ext, j, 0)[SUBLANES:, :]
        gates = jnp.dot(uc.astype(BF16), wg, preferred_element_type=F32)
        r = 1.0 / (1.0 + jnp.exp(-(gates[:, :W] + brg)))
        i = 1.0 / (1.0 + jnp.exp(-(gates[:, W:] + big)))
        log_a = coef * r
        a = jnp.exp(log_a)
        b = jnp.sqrt(1.0 - a * a) * (i * uc)
        sh = 1
        while sh < R:
            a_s = pltpu.roll(a, sh, 0)
            b_s = pltpu.roll(b, sh, 0)
            valid = row >= sh
            b = jnp.where(valid, a * b_s + b, b)
            a = jnp.where(valid, a * a_s, a)
            sh *= 2
        h = a * carry_scr[...] + b
        h_scr[...] = h
        carry_scr[...] = h_scr[R - 1:R, :]
        ms = jnp.mean(h * h, axis=-1, keepdims=True)
        hl = h * lax.rsqrt(ms + EPS) * gn
        z = z_ref[sl, :].astype(F32)
        gelu = 0.5 * z * (1.0 + jnp.tanh(math.sqrt(2.0 / math.pi) * (z + 0.044715 * (z * z * z))))
        o_ref[sl, :] = (hl * gelu).astype(o_ref.dtype)
        return carry

    lax.fori_loop(0, S // R, chunk, 0)


def _rglru(proj3, conv_w, conv_b, w_gates, b_rg, b_ig, lam, lru_norm_g):
    B, S, _ = proj3.shape
    NB = LRU_BLOCKS
    W = LRU_BLOCK_DIM
    u_off = 4 * RET_HEADS
    z_off = u_off + NB
    vec = pl.BlockSpec((1, W), lambda b, n: (0, n))
    return pl.pallas_call(
        _rglru_body,
        grid=(B, NB),
        in_specs=[
            pl.BlockSpec((None, S, W), lambda b, n: (b, 0, u_off + n)),
            pl.BlockSpec((None, S, W), lambda b, n: (b, 0, z_off + n)),
            pl.BlockSpec((CONV_WIDTH, W), lambda b, n: (0, n)),
            vec,
            pl.BlockSpec((None, W, 2 * W), lambda b, n: (n, 0, 0)),
            vec, vec, vec, vec,
        ],
        out_specs=pl.BlockSpec((None, S, W), lambda b, n: (b, 0, n)),
        out_shape=jax.ShapeDtypeStruct((B, S, NB * W), BF16),
        scratch_shapes=[pltpu.VMEM((1, W), F32), pltpu.VMEM((SUBLANES, W), F32),
                        pltpu.VMEM((LRU_ROWS, W), F32)],
        compiler_params=_params(("arbitrary", "arbitrary")),
        name="rglru",
    )(proj3, proj3, conv_w, conv_b, w_gates, b_rg, b_ig, lam, lru_norm_g)


def _outproj_body(ret_ref, lru_ref, x_ref, wo_ref, g_ref, wr_hi_ref, wr_lo_ref, br_ref,
                  x1_ref, h2_ref, lg_ref):
    R = ret_ref.shape[1]
    acc = jnp.dot(ret_ref[...], wo_ref[:R, :], preferred_element_type=F32)
    acc = acc + jnp.dot(lru_ref[...], wo_ref[R:, :], preferred_element_type=F32)
    x1 = x_ref[...] + acc
    x1_ref[...] = x1
    ms = jnp.mean(x1 * x1, axis=-1, keepdims=True)
    h2 = x1 * lax.rsqrt(ms + EPS) * g_ref[...]
    h2_ref[...] = h2
    h_hi = h2.astype(BF16)
    h_lo = (h2 - h_hi.astype(F32)).astype(BF16)
    wr_hi = wr_hi_ref[...]
    lg = jnp.dot(h_hi, wr_hi, preferred_element_type=F32)
    lg = lg + jnp.dot(h_lo, wr_hi, preferred_element_type=F32)
    lg = lg + jnp.dot(h_hi, wr_lo_ref[...], preferred_element_type=F32)
    lg_ref[...] = lg + br_ref[...]


def _outproj(ret2d, lru2d, x2d, wo_bf16, g, wr_hi, wr_lo, br, tm):
    T, D = x2d.shape
    R = ret2d.shape[1]
    L = lru2d.shape[1]
    NR = wr_hi.shape[1]
    const = lambda shape: pl.BlockSpec(shape, lambda i: (0, 0))
    return pl.pallas_call(
        _outproj_body,
        grid=(T // tm,),
        in_specs=[
            pl.BlockSpec((tm, R), lambda i: (i, 0)),
            pl.BlockSpec((tm, L), lambda i: (i, 0)),
            pl.BlockSpec((tm, D), lambda i: (i, 0)),
            const((R + L, D)), const((1, D)), const((D, NR)), const((D, NR)), const((1, NR)),
        ],
        out_specs=[
            pl.BlockSpec((tm, D), lambda i: (i, 0)),
            pl.BlockSpec((tm, D), lambda i: (i, 0)),
            pl.BlockSpec((tm, NR), lambda i: (i, 0)),
        ],
        out_shape=[
            jax.ShapeDtypeStruct((T, D), F32),
            jax.ShapeDtypeStruct((T, D), F32),
            jax.ShapeDtypeStruct((T, NR), F32),
        ],
        compiler_params=_params(("arbitrary",)),
        name="outproj",
    )(ret2d, lru2d, x2d, wo_bf16, g, wr_hi, wr_lo, br)


def _start_row_gather(src_hbm, idx_ref, dst, sem, n_rows):
    for r in range(n_rows):
        pltpu.make_async_copy(src_hbm.at[pl.ds(idx_ref[0, r], 1)], dst.at[pl.ds(r, 1)], sem).start()


def _wait_rows(dst, sem):
    pltpu.make_async_copy(dst, dst, sem).wait()


def _moe_body(be_ref, tok0_ref, tokn_ref, h2_hbm, wg_ref, wu_ref, wd_ref, y_ref, xbuf, sem):
    del be_ref
    i = pl.program_id(0)
    n = pl.num_programs(0)
    slot = lax.rem(i, 2)
    rows = xbuf.shape[1]

    @pl.when(i == 0)
    def _():
        _start_row_gather(h2_hbm, tok0_ref, xbuf.at[0], sem.at[0], rows)

    _start_row_gather(h2_hbm, tokn_ref, xbuf.at[1 - slot], sem.at[1 - slot], rows)
    _wait_rows(xbuf.at[slot], sem.at[slot])
    x = xbuf[slot].astype(BF16)
    gate = jnp.dot(x, wg_ref[...], preferred_element_type=F32)
    up = jnp.dot(x, wu_ref[...], preferred_element_type=F32)
    hmid = ((gate / (1.0 + jnp.exp(-gate))) * up).astype(BF16)
    y_ref[...] = jnp.dot(hmid, wd_ref[...], preferred_element_type=F32)

    @pl.when(i == n - 1)
    def _():
        _wait_rows(xbuf.at[1 - slot], sem.at[1 - slot])


def _moe(block_e, tok3, h2, wg_bf16, wu_bf16, wd_bf16):
    NB, _, rows = tok3.shape
    T, D = h2.shape
    E, _, DE = wg_bf16.shape
    grid_spec = pltpu.PrefetchScalarGridSpec(
        num_scalar_prefetch=1,
        grid=(NB,),
        in_specs=[
            pl.BlockSpec((None, 1, rows), lambda i, be: (0, 0, 0), memory_space=pltpu.SMEM),
            pl.BlockSpec((None, 1, rows), lambda i, be: (jnp.minimum(i + 1, NB - 1), 0, 0),
                         memory_space=pltpu.SMEM),
            pl.BlockSpec(memory_space=pl.ANY),
            pl.BlockSpec((None, D, DE), lambda i, be: (be[i], 0, 0)),
            pl.BlockSpec((None, D, DE), lambda i, be: (be[i], 0, 0)),
            pl.BlockSpec((None, DE, D), lambda i, be: (be[i], 0, 0)),
        ],
        out_specs=pl.BlockSpec((rows, D), lambda i, be: (i, 0)),
        scratch_shapes=[pltpu.VMEM((2, rows, D), F32), pltpu.SemaphoreType.DMA((2,))],
    )
    return pl.pallas_call(
        _moe_body,
        grid_spec=grid_spec,
        out_shape=jax.ShapeDtypeStruct((NB * rows, D), F32),
        compiler_params=_params(("arbitrary",)),
        name="moe",
    )(block_e, tok3, tok3, h2, wg_bf16, wu_bf16, wd_bf16)


def _combine_body(d0_ref, dn_ref, y_hbm, x1_ref, w_ref, g_ref, o_ref, ybuf, sem):
    i = pl.program_id(0)
    n = pl.num_programs(0)
    slot = lax.rem(i, 2)
    tm = x1_ref.shape[0]

    @pl.when(i == 0)
    def _():
        _start_row_gather(y_hbm, d0_ref, ybuf.at[0], sem.at[0], 2 * tm)

    _start_row_gather(y_hbm, dn_ref, ybuf.at[1 - slot], sem.at[1 - slot], 2 * tm)
    _wait_rows(ybuf.at[slot], sem.at[slot])
    w = w_ref[...]
    y0 = ybuf[slot, :tm, :]
    y1 = ybuf[slot, tm:, :]
    x = x1_ref[...] + (w[:, 0:1] * y0 + w[:, 1:2] * y1)
    ms = jnp.mean(x * x, axis=-1, keepdims=True)
    o_ref[...] = x * lax.rsqrt(ms + EPS) * g_ref[...]

    @pl.when(i == n - 1)
    def _():
        _wait_rows(ybuf.at[1 - slot], sem.at[1 - slot])


def _combine(dest3, y, x1, e_w, g, tm):
    T, D = x1.shape
    NT = T // tm
    return pl.pallas_call(
        _combine_body,
        grid=(NT,),
        in_specs=[
            pl.BlockSpec((None, 1, 2 * tm), lambda i: (0, 0, 0), memory_space=pltpu.SMEM),
            pl.BlockSpec((None, 1, 2 * tm), lambda i: (jnp.minimum(i + 1, NT - 1), 0, 0),
                         memory_space=pltpu.SMEM),
            pl.BlockSpec(memory_space=pl.ANY),
            pl.BlockSpec((tm, D), lambda i: (i, 0)),
            pl.BlockSpec((tm, TOP_K), lambda i: (i, 0)),
            pl.BlockSpec((1, D), lambda i: (0, 0)),
        ],
        out_specs=pl.BlockSpec((tm, D), lambda i: (i, 0)),
        out_shape=jax.ShapeDtypeStruct((T, D), F32),
        scratch_shapes=[pltpu.VMEM((2, 2 * tm, D), F32), pltpu.SemaphoreType.DMA((2,))],
        compiler_params=_params(("arbitrary",)),
        name="combine",
    )(dest3, dest3, y, x1, e_w, g)


def _route(logits, rows):
    T = logits.shape[0]
    G, EG = N_GROUPS, EXPERTS_PER_GROUP
    gl = logits[:, :G]
    g_idx = jnp.argmax(gl, axis=-1)
    g_p = 1.0 / jnp.sum(jnp.exp(gl - jnp.max(gl, axis=-1, keepdims=True)), axis=-1)
    el = logits[:, G:G + G * EG].reshape(T, G, EG)
    el = jnp.take_along_axis(el, g_idx[:, None, None], axis=1)[:, 0]
    i1 = jnp.argmax(el, axis=-1)
    t1 = jnp.max(el, axis=-1)
    lane = jnp.arange(EG)[None, :]
    el2 = jnp.where(lane == i1[:, None], -jnp.inf, el)
    i2 = jnp.argmax(el2, axis=-1)
    t2 = jnp.max(el2, axis=-1)
    p2 = jnp.exp(t2 - t1)
    w1 = g_p / (1.0 + p2)
    w2 = g_p * p2 / (1.0 + p2)
    e_id = (g_idx[:, None] * EG + jnp.stack([i1, i2], axis=-1)).astype(I32)
    e_w = jnp.stack([w1, w2], axis=-1).astype(F32)

    A = T * TOP_K
    flat_e = e_id.reshape(A)
    onehot = (flat_e[:, None] == jnp.arange(N_EXPERTS, dtype=I32)[None, :]).astype(I32)
    csum = jnp.cumsum(onehot, axis=0)
    rank = jnp.take_along_axis(csum, flat_e[:, None], axis=1)[:, 0] - 1
    counts = csum[-1]
    padded = (counts + rows - 1) // rows * rows
    pad_end = jnp.cumsum(padded)
    pad_start = pad_end - padded
    dest = (pad_start[flat_e] + rank).astype(I32)
    n_blocks = (A + N_EXPERTS * (rows - 1)) // rows
    P = n_blocks * rows
    tok = jnp.zeros((P,), I32).at[dest].set(jnp.arange(A, dtype=I32) // TOP_K)
    block_e = jnp.minimum(
        jnp.searchsorted(pad_end, jnp.arange(n_blocks, dtype=I32) * rows, side="right"),
        N_EXPERTS - 1).astype(I32)
    return e_w, dest.reshape(T, TOP_K), tok, block_e, n_blocks


def kernel(x, norm_mix_g, w_in, ret_norm_g, conv_w, conv_b, w_rg, b_rg, w_ig, b_ig, lru_lambda,
           lru_norm_g, w_out, norm_ffn_g, w_group, b_group, w_router, b_router, w_gate, w_up,
           w_down, norm_final_g):
    B, S, D = x.shape
    T = B * S
    depth = norm_mix_g.shape[0]
    assert depth == 1, "the combine kernel fuses the final norm, so only one layer is supported"
    H, d = RET_HEADS, HEAD_DIM

    half = d // 2
    inv = ROPE_BASE ** (-jnp.arange(half, dtype=F32) / half)
    ang = jnp.arange(S, dtype=F32)[:, None] * inv[None, :]
    cos_t = jnp.concatenate([jnp.cos(ang), jnp.cos(ang)], axis=-1)
    sin_t = jnp.concatenate([-jnp.sin(ang), jnp.sin(ang)], axis=-1)
    log_gamma = jnp.log1p(-(2.0 ** (-5.0 - jnp.arange(H, dtype=F32))))

    x2d = x.reshape(T, D)
    for l in range(depth):
        proj = _inproj(x2d, norm_mix_g[l][None, :], w_in[l].astype(BF16), INPROJ_TM, INPROJ_TN)
        proj3 = proj.reshape(B, S, proj.shape[1])
        ret = _retention(proj3, log_gamma, cos_t, sin_t, ret_norm_g[l][None, :])
        w_gates = jnp.concatenate([w_rg[l], w_ig[l]], axis=-1).astype(BF16)
        lru = _rglru(proj3, conv_w[l], conv_b[l][None, :], w_gates, b_rg[l][None, :],
                     b_ig[l][None, :], lru_lambda[l][None, :], lru_norm_g[l][None, :])

        n_route = N_GROUPS + N_EXPERTS
        wr = jnp.concatenate([w_group[l], w_router[l], jnp.zeros((D, LANES - n_route), F32)], axis=-1)
        br = jnp.concatenate([b_group[l], b_router[l], jnp.zeros((LANES - n_route,), F32)])[None, :]
        wr_hi = wr.astype(BF16)
        wr_lo = (wr - wr_hi.astype(F32)).astype(BF16)
        x1, h2, logits = _outproj(ret.reshape(T, -1), lru.reshape(T, -1), x2d, w_out[l].astype(BF16),
                                  norm_ffn_g[l][None, :], wr_hi, wr_lo, br, OUTPROJ_TM)

        e_w, dest, tok, block_e, n_blocks = _route(logits, MOE_ROWS)
        y = _moe(block_e, tok.reshape(n_blocks, 1, MOE_ROWS), h2, w_gate[l].astype(BF16),
                 w_up[l].astype(BF16), w_down[l].astype(BF16))
        dest3 = dest.reshape(T // COMBINE_TM, COMBINE_TM, TOP_K).transpose(0, 2, 1).reshape(
            T // COMBINE_TM, 1, TOP_K * COMBINE_TM)
        x2d = _combine(dest3, y, x1, e_w, norm_final_g[None, :], COMBINE_TM)
    return x2d.reshape(B, S, D)
```

```python
import functools
import math

import jax
import jax.numpy as jnp
from jax import lax
from jax.experimental import pallas as pl
from jax.experimental.pallas import tpu as pltpu

F32 = jnp.float32
BF16 = jnp.bfloat16
I32 = jnp.int32

EPS = 1e-6
RET_HEADS = 8
HEAD_DIM = 128
RET_CHUNK = 128
ROPE_BASE = 10000.0
LRU_BLOCKS = 8
LRU_BLOCK_DIM = 128
CONV_WIDTH = 4
LRU_C = 8.0
N_GROUPS = 4
EXPERTS_PER_GROUP = 8
N_EXPERTS = N_GROUPS * EXPERTS_PER_GROUP
TOP_K = 2

LANES = 128
SUBLANES = 8
VMEM_LIMIT = 56 * 1024 * 1024

INPROJ_TM = 1024
INPROJ_TN = 1024
LRU_ROWS = 128
OUTPROJ_TM = 256
MOE_ROWS = 256
MOE_UP_COLS = 256
MOE_DOWN_COLS = 512
COMBINE_TM = 256


def _params(sem):
    return pltpu.CompilerParams(dimension_semantics=sem, vmem_limit_bytes=VMEM_LIMIT)


def _inproj_body(x_ref, g_ref, w_ref, o_ref, h_scr):
    @pl.when(pl.program_id(1) == 0)
    def _():
        x = x_ref[...]
        ms = jnp.mean(x * x, axis=-1, keepdims=True)
        h_scr[...] = (x * lax.rsqrt(ms + EPS) * g_ref[...]).astype(BF16)

    o_ref[...] = jnp.dot(h_scr[...], w_ref[...], preferred_element_type=F32).astype(o_ref.dtype)


def _inproj(x2d, g, w_bf16, tm, tn):
    T, D = x2d.shape
    N = w_bf16.shape[1]
    return pl.pallas_call(
        _inproj_body,
        grid=(T // tm, N // tn),
        in_specs=[
            pl.BlockSpec((tm, D), lambda i, j: (i, 0)),
            pl.BlockSpec((1, D), lambda i, j: (0, 0)),
            pl.BlockSpec((D, tn), lambda i, j: (0, j)),
        ],
        out_specs=pl.BlockSpec((tm, tn), lambda i, j: (i, j)),
        out_shape=jax.ShapeDtypeStruct((T, N), BF16),
        scratch_shapes=[pltpu.VMEM((tm, D), BF16)],
        compiler_params=_params(("arbitrary", "arbitrary")),
        name="inproj",
    )(x2d, g, w_bf16)


def _retention_body(lg_ref, q_ref, k_ref, v_ref, g_ref, cos_ref, sin_ref, gn_ref, o_ref,
                    mask_scr, qdec_scr, kdec_scr):
    C = RET_CHUNK
    d = HEAD_DIM
    S = q_ref.shape[0]
    lg = lg_ref[pl.program_id(1)]
    row = lax.broadcasted_iota(I32, (C, d), 0).astype(F32)
    col = lax.broadcasted_iota(I32, (C, d), 1).astype(F32)
    rel = row - col
    scale = d ** -0.5
    mask_scr[...] = jnp.where(rel >= 0, jnp.exp(jnp.maximum(rel, 0.0) * lg), 0.0) * scale
    qdec_scr[...] = jnp.exp((row + 1.0) * lg) * scale
    kdec_scr[...] = jnp.exp((C - 1.0 - row) * lg)
    c_dec = jnp.exp(jnp.full((1, d), float(C), F32) * lg)
    gn = gn_ref[...]
    state = jnp.zeros((d, d), F32)

    for n in range(S // C):
        sl = pl.ds(n * C, C)
        cos = cos_ref[sl, :]
        sin = sin_ref[sl, :]
        q = q_ref[sl, :].astype(F32)
        k = k_ref[sl, :].astype(F32)
        v = v_ref[sl, :]
        q = q * cos + pltpu.roll(q, d // 2, 1) * sin
        k = k * cos + pltpu.roll(k, d // 2, 1) * sin
        scores = lax.dot_general(q.astype(BF16), k.astype(BF16), (((1,), (1,)), ((), ())),
                                 preferred_element_type=F32) * mask_scr[...]
        o = jnp.dot(scores.astype(BF16), v, preferred_element_type=F32)
        o = o + jnp.dot((q * qdec_scr[...]).astype(BF16), state.astype(BF16),
                        preferred_element_type=F32)
        kd_t = (k * kdec_scr[...]).T.astype(BF16)
        state = c_dec * state + jnp.dot(kd_t, v, preferred_element_type=F32)
        mu = jnp.mean(o, axis=-1, keepdims=True)
        oc = o - mu
        var = jnp.mean(oc * oc, axis=-1, keepdims=True)
        on = oc * lax.rsqrt(var + EPS) * gn
        g = g_ref[sl, :].astype(F32)
        o_ref[sl, :] = ((g / (1.0 + jnp.exp(-g))) * on).astype(o_ref.dtype)


def _retention(proj3, log_gamma, cos_t, sin_t, ret_norm_g):
    B, S, _ = proj3.shape
    H = RET_HEADS
    d = HEAD_DIM
    blk = lambda off: pl.BlockSpec((None, S, d), lambda b, h, off=off: (b, 0, off + h))
    return pl.pallas_call(
        _retention_body,
        grid=(B, H),
        in_specs=[
            pl.BlockSpec(memory_space=pltpu.SMEM),
            blk(0), blk(H), blk(2 * H), blk(3 * H),
            pl.BlockSpec((S, d), lambda b, h: (0, 0)),
            pl.BlockSpec((S, d), lambda b, h: (0, 0)),
            pl.BlockSpec((1, d), lambda b, h: (0, h)),
        ],
        out_specs=pl.BlockSpec((None, S, d), lambda b, h: (b, 0, h)),
        out_shape=jax.ShapeDtypeStruct((B, S, H * d), BF16),
        scratch_shapes=[pltpu.VMEM((RET_CHUNK, d), F32)] * 3,
        compiler_params=_params(("arbitrary", "arbitrary")),
        name="retention",
    )(log_gamma, proj3, proj3, proj3, proj3, cos_t, sin_t, ret_norm_g)


def _rglru_body(u_ref, z_ref, cw_ref, cb_ref, wg_ref, brg_ref, big_ref, lam_ref, gn_ref, o_ref,
                uf_scr):
    S = u_ref.shape[0]
    W = LRU_BLOCK_DIM
    R = LRU_ROWS
    K = CONV_WIDTH
    nl = -lam_ref[...]
    softplus = jnp.maximum(nl, 0.0) + jnp.log1p(jnp.exp(-jnp.abs(nl)))
    coef = -LRU_C * softplus
    cw = cw_ref[...]
    cb = cb_ref[...]
    brg = brg_ref[...]
    big = big_ref[...]
    gn = gn_ref[...]
    wg = wg_ref[...]
    row_in_tile = lax.broadcasted_iota(I32, (R, W), 0) & (SUBLANES - 1)
    uf_scr[:SUBLANES, :] = jnp.zeros((SUBLANES, W), F32)
    carry = jnp.zeros((1, W), F32)

    for c in range(S // R):
        base = SUBLANES + c * R
        u = u_ref[c * R:(c + 1) * R, :].astype(F32)
        uf_scr[base:base + R, :] = u
        uc = cb + cw[K - 1:K, :] * u
        for j in range(1, K):
            uc = uc + cw[K - 1 - j:K - j, :] * uf_scr[base - j:base - j + R, :]
        gates = jnp.dot(uc.astype(BF16), wg, preferred_element_type=F32)
        r = 1.0 / (1.0 + jnp.exp(-(gates[:, :W] + brg)))
        i = 1.0 / (1.0 + jnp.exp(-(gates[:, W:] + big)))
        a = jnp.exp(coef * r)
        t = 1.0 - a * a
        b = jnp.where(t > 0.0, t * lax.rsqrt(t), 0.0) * (i * uc)
        for sh in (1, 2, 4):
            a_s = pltpu.roll(a, sh, 0)
            b_s = pltpu.roll(b, sh, 0)
            valid = row_in_tile >= sh
            b = jnp.where(valid, a * b_s + b, b)
            a = jnp.where(valid, a * a_s, a)
        tiles = []
        for k in range(R // SUBLANES):
            rows = slice(k * SUBLANES, (k + 1) * SUBLANES)
            h_tile = a[rows, :] * carry + b[rows, :]
            carry = h_tile[SUBLANES - 1:SUBLANES, :]
            tiles.append(h_tile)
        h = jnp.concatenate(tiles, axis=0)
        ms = jnp.mean(h * h, axis=-1, keepdims=True)
        hl = h * lax.rsqrt(ms + EPS) * gn
        z = z_ref[c * R:(c + 1) * R, :].astype(F32)
        gelu = 0.5 * z * (1.0 + jnp.tanh(math.sqrt(2.0 / math.pi) * (z + 0.044715 * (z * z * z))))
        o_ref[c * R:(c + 1) * R, :] = (hl * gelu).astype(o_ref.dtype)


def _rglru(proj3, conv_w, conv_b, w_gates, b_rg, b_ig, lam, lru_norm_g):
    B, S, _ = proj3.shape
    NB = LRU_BLOCKS
    W = LRU_BLOCK_DIM
    u_off = 4 * RET_HEADS
    z_off = u_off + NB
    vec = pl.BlockSpec((1, W), lambda b, n: (0, n))
    return pl.pallas_call(
        _rglru_body,
        grid=(B, NB),
        in_specs=[
            pl.BlockSpec((None, S, W), lambda b, n: (b, 0, u_off + n)),
            pl.BlockSpec((None, S, W), lambda b, n: (b, 0, z_off + n)),
            pl.BlockSpec((CONV_WIDTH, W), lambda b, n: (0, n)),
            vec,
            pl.BlockSpec((None, W, 2 * W), lambda b, n: (n, 0, 0)),
            vec, vec, vec, vec,
        ],
        out_specs=pl.BlockSpec((None, S, W), lambda b, n: (b, 0, n)),
        out_shape=jax.ShapeDtypeStruct((B, S, NB * W), BF16),
        scratch_shapes=[pltpu.VMEM((SUBLANES + S, W), F32)],
        compiler_params=_params(("arbitrary", "arbitrary")),
        name="rglru",
    )(proj3, proj3, conv_w, conv_b, w_gates, b_rg, b_ig, lam, lru_norm_g)


def _outproj_body(ret_ref, lru_ref, x_ref, wo_ref, g_ref, wr_hi_ref, wr_lo_ref, br_ref,
                  x1_ref, h2_ref, lg_ref):
    R = ret_ref.shape[1]
    acc = jnp.dot(ret_ref[...], wo_ref[:R, :], preferred_element_type=F32)
    acc = acc + jnp.dot(lru_ref[...], wo_ref[R:, :], preferred_element_type=F32)
    x1 = x_ref[...] + acc
    x1_ref[...] = x1
    ms = jnp.mean(x1 * x1, axis=-1, keepdims=True)
    h2 = x1 * lax.rsqrt(ms + EPS) * g_ref[...]
    h2_ref[...] = h2
    h_hi = h2.astype(BF16)
    h_lo = (h2 - h_hi.astype(F32)).astype(BF16)
    wr_hi = wr_hi_ref[...]
    lg = jnp.dot(h_hi, wr_hi, preferred_element_type=F32)
    lg = lg + jnp.dot(h_lo, wr_hi, preferred_element_type=F32)
    lg = lg + jnp.dot(h_hi, wr_lo_ref[...], preferred_element_type=F32)
    lg_ref[...] = lg + br_ref[...]


def _outproj(ret2d, lru2d, x2d, wo_bf16, g, wr_hi, wr_lo, br, tm):
    T, D = x2d.shape
    R = ret2d.shape[1]
    L = lru2d.shape[1]
    NR = wr_hi.shape[1]
    const = lambda shape: pl.BlockSpec(shape, lambda i: (0, 0))
    return pl.pallas_call(
        _outproj_body,
        grid=(T // tm,),
        in_specs=[
            pl.BlockSpec((tm, R), lambda i: (i, 0)),
            pl.BlockSpec((tm, L), lambda i: (i, 0)),
            pl.BlockSpec((tm, D), lambda i: (i, 0)),
            const((R + L, D)), const((1, D)), const((D, NR)), const((D, NR)), const((1, NR)),
        ],
        out_specs=[
            pl.BlockSpec((tm, D), lambda i: (i, 0)),
            pl.BlockSpec((tm, D), lambda i: (i, 0)),
            pl.BlockSpec((tm, NR), lambda i: (i, 0)),
        ],
        out_shape=[
            jax.ShapeDtypeStruct((T, D), F32),
            jax.ShapeDtypeStruct((T, D), F32),
            jax.ShapeDtypeStruct((T, NR), F32),
        ],
        compiler_params=_params(("arbitrary",)),
        name="outproj",
    )(ret2d, lru2d, x2d, wo_bf16, g, wr_hi, wr_lo, br)


def _start_row_gather(src_hbm, idx_ref, dst, sem, lo, hi):
    for r in range(lo, hi):
        pltpu.make_async_copy(src_hbm.at[pl.ds(idx_ref[0, r], 1)], dst.at[pl.ds(r, 1)], sem).start()


def _zero_after(buf, n_rows, n_cols):
    bits = pltpu.bitcast(buf[0:SUBLANES, 0:LANES], jnp.uint32)
    zero = pltpu.bitcast((bits >> 16) >> 16, F32)
    return jnp.tile(zero, (n_rows // SUBLANES, n_cols // LANES))


def _wait_rows(dst, sem):
    pltpu.make_async_copy(dst, dst, sem).wait()


def _moe_body(be_ref, tok0_ref, tokn_ref, h2_hbm, wg_ref, wu_ref, wd_ref, y_ref, xbuf, sem):
    del be_ref
    i = pl.program_id(0)
    n = pl.num_programs(0)
    slot = lax.rem(i, 2)
    rows = xbuf.shape[1]

    @pl.when(i == 0)
    def _():
        _start_row_gather(h2_hbm, tok0_ref, xbuf.at[0], sem.at[0], 0, rows)

    _wait_rows(xbuf.at[slot], sem.at[slot])
    x = xbuf[slot].astype(BF16)

    DE = wg_ref.shape[1]
    D = wd_ref.shape[1]
    n_up, n_down = DE // MOE_UP_COLS, D // MOE_DOWN_COLS
    per = rows // (n_up + n_down)
    cur, nxt, nsem = xbuf.at[slot], xbuf.at[1 - slot], sem.at[1 - slot]
    hparts = []
    for c in range(n_up):
        cs = slice(c * MOE_UP_COLS, (c + 1) * MOE_UP_COLS)
        _start_row_gather(h2_hbm, tokn_ref, nxt, nsem, c * per, (c + 1) * per)
        zero = _zero_after(cur, rows, MOE_UP_COLS)
        gate = jnp.dot(x, wg_ref[:, cs], preferred_element_type=F32)
        up = jnp.dot(x, wu_ref[:, cs], preferred_element_type=F32) + zero
        hparts.append(((gate / (1.0 + jnp.exp(-gate))) * up).astype(BF16))
    hmid = jnp.concatenate(hparts, axis=-1)
    for c in range(n_down):
        cs = slice(c * MOE_DOWN_COLS, (c + 1) * MOE_DOWN_COLS)
        g = n_up + c
        _start_row_gather(h2_hbm, tokn_ref, nxt, nsem, g * per, rows if c == n_down - 1 else (g + 1) * per)
        zero = _zero_after(cur, rows, MOE_DOWN_COLS)
        y_ref[:, cs] = jnp.dot(hmid, wd_ref[:, cs], preferred_element_type=F32) + zero

    @pl.when(i == n - 1)
    def _():
        _wait_rows(nxt, nsem)


def _moe(block_e, tok3, h2, wg_bf16, wu_bf16, wd_bf16):
    NB, _, rows = tok3.shape
    T, D = h2.shape
    E, _, DE = wg_bf16.shape
    grid_spec = pltpu.PrefetchScalarGridSpec(
        num_scalar_prefetch=1,
        grid=(NB,),
        in_specs=[
            pl.BlockSpec((None, 1, rows), lambda i, be: (0, 0, 0), memory_space=pltpu.SMEM),
            pl.BlockSpec((None, 1, rows), lambda i, be: (jnp.minimum(i + 1, NB - 1), 0, 0),
                         memory_space=pltpu.SMEM),
            pl.BlockSpec(memory_space=pl.ANY),
            pl.BlockSpec((None, D, DE), lambda i, be: (be[i], 0, 0)),
            pl.BlockSpec((None, D, DE), lambda i, be: (be[i], 0, 0)),
            pl.BlockSpec((None, DE, D), lambda i, be: (be[i], 0, 0)),
        ],
        out_specs=pl.BlockSpec((rows, D), lambda i, be: (i, 0)),
        scratch_shapes=[pltpu.VMEM((2, rows, D), F32), pltpu.SemaphoreType.DMA((2,))],
    )
    return pl.pallas_call(
        _moe_body,
        grid_spec=grid_spec,
        out_shape=jax.ShapeDtypeStruct((NB * rows, D), F32),
        compiler_params=_params(("arbitrary",)),
        name="moe",
    )(block_e, tok3, tok3, h2, wg_bf16, wu_bf16, wd_bf16)


def _combine_body(d0_ref, dn_ref, y_hbm, x1_ref, w_ref, g_ref, o_ref, ybuf, sem):
    i = pl.program_id(0)
    n = pl.num_programs(0)
    slot = lax.rem(i, 2)
    tm = x1_ref.shape[0]

    @pl.when(i == 0)
    def _():
        _start_row_gather(y_hbm, d0_ref, ybuf.at[0], sem.at[0], 0, 2 * tm)

    _start_row_gather(y_hbm, dn_ref, ybuf.at[1 - slot], sem.at[1 - slot], 0, 2 * tm)
    _wait_rows(ybuf.at[slot], sem.at[slot])
    w = w_ref[...]
    y0 = ybuf[slot, :tm, :]
    y1 = ybuf[slot, tm:, :]
    x = x1_ref[...] + (w[:, 0:1] * y0 + w[:, 1:2] * y1)
    ms = jnp.mean(x * x, axis=-1, keepdims=True)
    o_ref[...] = x * lax.rsqrt(ms + EPS) * g_ref[...]

    @pl.when(i == n - 1)
    def _():
        _wait_rows(ybuf.at[1 - slot], sem.at[1 - slot])


def _combine(dest3, y, x1, e_w, g, tm):
    T, D = x1.shape
    NT = T // tm
    return pl.pallas_call(
        _combine_body,
        grid=(NT,),
        in_specs=[
            pl.BlockSpec((None, 1, 2 * tm), lambda i: (0, 0, 0), memory_space=pltpu.SMEM),
            pl.BlockSpec((None, 1, 2 * tm), lambda i: (jnp.minimum(i + 1, NT - 1), 0, 0),
                         memory_space=pltpu.SMEM),
            pl.BlockSpec(memory_space=pl.ANY),
            pl.BlockSpec((tm, D), lambda i: (i, 0)),
            pl.BlockSpec((tm, TOP_K), lambda i: (i, 0)),
            pl.BlockSpec((1, D), lambda i: (0, 0)),
        ],
        out_specs=pl.BlockSpec((tm, D), lambda i: (i, 0)),
        out_shape=jax.ShapeDtypeStruct((T, D), F32),
        scratch_shapes=[pltpu.VMEM((2, 2 * tm, D), F32), pltpu.SemaphoreType.DMA((2,))],
        compiler_params=_params(("arbitrary",)),
        name="combine",
    )(dest3, dest3, y, x1, e_w, g)


def _route(logits, rows):
    T = logits.shape[0]
    G, EG = N_GROUPS, EXPERTS_PER_GROUP
    gl = logits[:, :G]
    g_idx = jnp.argmax(gl, axis=-1)
    g_p = 1.0 / jnp.sum(jnp.exp(gl - jnp.max(gl, axis=-1, keepdims=True)), axis=-1)
    el = logits[:, G:G + G * EG].reshape(T, G, EG)
    el = jnp.take_along_axis(el, g_idx[:, None, None], axis=1)[:, 0]
    i1 = jnp.argmax(el, axis=-1)
    t1 = jnp.max(el, axis=-1)
    lane = jnp.arange(EG)[None, :]
    el2 = jnp.where(lane == i1[:, None], -jnp.inf, el)
    i2 = jnp.argmax(el2, axis=-1)
    t2 = jnp.max(el2, axis=-1)
    p2 = jnp.exp(t2 - t1)
    w1 = g_p / (1.0 + p2)
    w2 = g_p * p2 / (1.0 + p2)
    e_id = (g_idx[:, None] * EG + jnp.stack([i1, i2], axis=-1)).astype(I32)
    e_w = jnp.stack([w1, w2], axis=-1).astype(F32)

    A = T * TOP_K
    flat_e = e_id.reshape(A)
    onehot = (flat_e[:, None] == jnp.arange(N_EXPERTS, dtype=I32)[None, :]).astype(I32)
    csum = jnp.cumsum(onehot, axis=0)
    rank = jnp.take_along_axis(csum, flat_e[:, None], axis=1)[:, 0] - 1
    counts = csum[-1]
    padded = (counts + rows - 1) // rows * rows
    pad_end = jnp.cumsum(padded)
    pad_start = pad_end - padded
    dest = (pad_start[flat_e] + rank).astype(I32)
    n_blocks = (A + N_EXPERTS * (rows - 1)) // rows
    P = n_blocks * rows
    tok = jnp.zeros((P,), I32).at[dest].set(jnp.arange(A, dtype=I32) // TOP_K, unique_indices=True,
                                            mode="promise_in_bounds")
    block_e = jnp.minimum(
        jnp.searchsorted(pad_end, jnp.arange(n_blocks, dtype=I32) * rows, side="right"),
        N_EXPERTS - 1).astype(I32)
    return e_w, dest.reshape(T, TOP_K), tok, block_e, n_blocks


def kernel(x, norm_mix_g, w_in, ret_norm_g, conv_w, conv_b, w_rg, b_rg, w_ig, b_ig, lru_lambda,
           lru_norm_g, w_out, norm_ffn_g, w_group, b_group, w_router, b_router, w_gate, w_up,
           w_down, norm_final_g):
    B, S, D = x.shape
    T = B * S
    depth = norm_mix_g.shape[0]
    assert depth == 1, "the combine kernel fuses the final norm, so only one layer is supported"
    H, d = RET_HEADS, HEAD_DIM

    half = d // 2
    inv = ROPE_BASE ** (-jnp.arange(half, dtype=F32) / half)
    ang = jnp.arange(S, dtype=F32)[:, None] * inv[None, :]
    cos_t = jnp.concatenate([jnp.cos(ang), jnp.cos(ang)], axis=-1)
    sin_t = jnp.concatenate([-jnp.sin(ang), jnp.sin(ang)], axis=-1)
    log_gamma = jnp.log1p(-(2.0 ** (-5.0 - jnp.arange(H, dtype=F32))))

    x2d = x.reshape(T, D)
    for l in range(depth):
        proj = _inproj(x2d, norm_mix_g[l][None, :], w_in[l].astype(BF16), INPROJ_TM, INPROJ_TN)
        proj3 = proj.reshape(B, S, proj.shape[1])
        ret = _retention(proj3, log_gamma, cos_t, sin_t, ret_norm_g[l][None, :])
        w_gates = jnp.concatenate([w_rg[l], w_ig[l]], axis=-1).astype(BF16)
        lru = _rglru(proj3, conv_w[l], conv_b[l][None, :], w_gates, b_rg[l][None, :],
                     b_ig[l][None, :], lru_lambda[l][None, :], lru_norm_g[l][None, :])

        n_route = N_GROUPS + N_EXPERTS
        wr = jnp.concatenate([w_group[l], w_router[l], jnp.zeros((D, LANES - n_route), F32)], axis=-1)
        br = jnp.concatenate([b_group[l], b_router[l], jnp.zeros((LANES - n_route,), F32)])[None, :]
        wr_hi = wr.astype(BF16)
        wr_lo = (wr - wr_hi.astype(F32)).astype(BF16)
        x1, h2, logits = _outproj(ret.reshape(T, -1), lru.reshape(T, -1), x2d, w_out[l].astype(BF16),
                                  norm_ffn_g[l][None, :], wr_hi, wr_lo, br, OUTPROJ_TM)

        e_w, dest, tok, block_e, n_blocks = _route(logits, MOE_ROWS)
        y = _moe(block_e, tok.reshape(n_blocks, 1, MOE_ROWS), h2, w_gate[l].astype(BF16),
                 w_up[l].astype(BF16), w_down[l].astype(BF16))
        dest3 = dest.reshape(T // COMBINE_TM, COMBINE_TM, TOP_K).transpose(0, 2, 1).reshape(
            T // COMBINE_TM, 1, TOP_K * COMBINE_TM)
        x2d = _combine(dest3, y, x1, e_w, norm_final_g[None, :], COMBINE_TM)
    return x2d.reshape(B, S, D)
```

```python
import functools
import math

import jax
import jax.numpy as jnp
from jax import lax
from jax.experimental import pallas as pl
from jax.experimental.pallas import tpu as pltpu

F32 = jnp.float32
BF16 = jnp.bfloat16
I32 = jnp.int32

EPS = 1e-6
RET_HEADS = 8
HEAD_DIM = 128
RET_CHUNK = 128
ROPE_BASE = 10000.0
LRU_BLOCKS = 8
LRU_BLOCK_DIM = 128
CONV_WIDTH = 4
LRU_C = 8.0
N_GROUPS = 4
EXPERTS_PER_GROUP = 8
N_EXPERTS = N_GROUPS * EXPERTS_PER_GROUP
TOP_K = 2

LANES = 128
SUBLANES = 8
VMEM_LIMIT = 56 * 1024 * 1024

INPROJ_TM = 1024
INPROJ_TN = 1024
LRU_ROWS = 128
OUTPROJ_TM = 256
ROUTE_TM = 2048
ROUTE_SUB = 256
MOE_ROWS = 256
MOE_UP_COLS = 256
MOE_DOWN_COLS = 512
COMBINE_TM = 256


def _params(sem):
    return pltpu.CompilerParams(dimension_semantics=sem, vmem_limit_bytes=VMEM_LIMIT)


def _inproj_body(x_ref, g_ref, w_ref, o_ref, h_scr):
    @pl.when(pl.program_id(1) == 0)
    def _():
        x = x_ref[...]
        ms = jnp.mean(x * x, axis=-1, keepdims=True)
        h_scr[...] = (x * lax.rsqrt(ms + EPS) * g_ref[...]).astype(BF16)

    o_ref[...] = jnp.dot(h_scr[...], w_ref[...], preferred_element_type=F32).astype(o_ref.dtype)


def _inproj(x2d, g, w_bf16, tm, tn):
    T, D = x2d.shape
    N = w_bf16.shape[1]
    return pl.pallas_call(
        _inproj_body,
        grid=(T // tm, N // tn),
        in_specs=[
            pl.BlockSpec((tm, D), lambda i, j: (i, 0)),
            pl.BlockSpec((1, D), lambda i, j: (0, 0)),
            pl.BlockSpec((D, tn), lambda i, j: (0, j)),
        ],
        out_specs=pl.BlockSpec((tm, tn), lambda i, j: (i, j)),
        out_shape=jax.ShapeDtypeStruct((T, N), BF16),
        scratch_shapes=[pltpu.VMEM((tm, D), BF16)],
        compiler_params=_params(("arbitrary", "arbitrary")),
        name="inproj",
    )(x2d, g, w_bf16)


def _retention_body(lg_ref, q_ref, k_ref, v_ref, g_ref, cos_ref, sin_ref, gn_ref, o_ref,
                    mask_scr, qdec_scr, kdec_scr):
    C = RET_CHUNK
    d = HEAD_DIM
    S = q_ref.shape[0]
    lg = lg_ref[pl.program_id(1)]
    row = lax.broadcasted_iota(I32, (C, d), 0).astype(F32)
    col = lax.broadcasted_iota(I32, (C, d), 1).astype(F32)
    rel = row - col
    scale = d ** -0.5
    mask_scr[...] = jnp.where(rel >= 0, jnp.exp(jnp.maximum(rel, 0.0) * lg), 0.0) * scale
    qdec_scr[...] = jnp.exp((row + 1.0) * lg) * scale
    kdec_scr[...] = jnp.exp((C - 1.0 - row) * lg)
    c_dec = jnp.exp(jnp.full((1, d), float(C), F32) * lg)
    gn = gn_ref[...]
    state = jnp.zeros((d, d), F32)

    for n in range(S // C):
        sl = pl.ds(n * C, C)
        cos = cos_ref[sl, :]
        sin = sin_ref[sl, :]
        q = q_ref[sl, :].astype(F32)
        k = k_ref[sl, :].astype(F32)
        v = v_ref[sl, :]
        q = q * cos + pltpu.roll(q, d // 2, 1) * sin
        k = k * cos + pltpu.roll(k, d // 2, 1) * sin
        scores = lax.dot_general(q.astype(BF16), k.astype(BF16), (((1,), (1,)), ((), ())),
                                 preferred_element_type=F32) * mask_scr[...]
        o = jnp.dot(scores.astype(BF16), v, preferred_element_type=F32)
        o = o + jnp.dot((q * qdec_scr[...]).astype(BF16), state.astype(BF16),
                        preferred_element_type=F32)
        kd_t = (k * kdec_scr[...]).T.astype(BF16)
        state = c_dec * state + jnp.dot(kd_t, v, preferred_element_type=F32)
        mu = jnp.mean(o, axis=-1, keepdims=True)
        oc = o - mu
        var = jnp.mean(oc * oc, axis=-1, keepdims=True)
        on = oc * lax.rsqrt(var + EPS) * gn
        g = g_ref[sl, :].astype(F32)
        o_ref[sl, :] = ((g / (1.0 + jnp.exp(-g))) * on).astype(o_ref.dtype)


def _retention(proj3, log_gamma, cos_t, sin_t, ret_norm_g):
    B, S, _ = proj3.shape
    H = RET_HEADS
    d = HEAD_DIM
    blk = lambda off: pl.BlockSpec((None, S, d), lambda b, h, off=off: (b, 0, off + h))
    return pl.pallas_call(
        _retention_body,
        grid=(B, H),
        in_specs=[
            pl.BlockSpec(memory_space=pltpu.SMEM),
            blk(0), blk(H), blk(2 * H), blk(3 * H),
            pl.BlockSpec((S, d), lambda b, h: (0, 0)),
            pl.BlockSpec((S, d), lambda b, h: (0, 0)),
            pl.BlockSpec((1, d), lambda b, h: (0, h)),
        ],
        out_specs=pl.BlockSpec((None, S, d), lambda b, h: (b, 0, h)),
        out_shape=jax.ShapeDtypeStruct((B, S, H * d), BF16),
        scratch_shapes=[pltpu.VMEM((RET_CHUNK, d), F32)] * 3,
        compiler_params=_params(("arbitrary", "arbitrary")),
        name="retention",
    )(log_gamma, proj3, proj3, proj3, proj3, cos_t, sin_t, ret_norm_g)


def _rglru_body(u_ref, z_ref, cw_ref, cb_ref, wg_ref, brg_ref, big_ref, lam_ref, gn_ref, o_ref,
                uf_scr):
    S = u_ref.shape[0]
    W = LRU_BLOCK_DIM
    R = LRU_ROWS
    K = CONV_WIDTH
    nl = -lam_ref[...]
    softplus = jnp.maximum(nl, 0.0) + jnp.log1p(jnp.exp(-jnp.abs(nl)))
    coef = -LRU_C * softplus
    cw = cw_ref[...]
    cb = cb_ref[...]
    brg = brg_ref[...]
    big = big_ref[...]
    gn = gn_ref[...]
    wg = wg_ref[...]
    row_in_tile = lax.broadcasted_iota(I32, (R, W), 0) & (SUBLANES - 1)
    uf_scr[:SUBLANES, :] = jnp.zeros((SUBLANES, W), F32)
    carry = jnp.zeros((1, W), F32)

    for c in range(S // R):
        base = SUBLANES + c * R
        u = u_ref[c * R:(c + 1) * R, :].astype(F32)
        uf_scr[base:base + R, :] = u
        uc = cb + cw[K - 1:K, :] * u
        for j in range(1, K):
            uc = uc + cw[K - 1 - j:K - j, :] * uf_scr[base - j:base - j + R, :]
        gates = jnp.dot(uc.astype(BF16), wg, preferred_element_type=F32)
        r = 1.0 / (1.0 + jnp.exp(-(gates[:, :W] + brg)))
        i = 1.0 / (1.0 + jnp.exp(-(gates[:, W:] + big)))
        a = jnp.exp(coef * r)
        t = 1.0 - a * a
        b = jnp.where(t > 0.0, t * lax.rsqrt(t), 0.0) * (i * uc)
        for sh in (1, 2, 4):
            a_s = pltpu.roll(a, sh, 0)
            b_s = pltpu.roll(b, sh, 0)
            valid = row_in_tile >= sh
            b = jnp.where(valid, a * b_s + b, b)
            a = jnp.where(valid, a * a_s, a)
        tiles = []
        for k in range(R // SUBLANES):
            rows = slice(k * SUBLANES, (k + 1) * SUBLANES)
            h_tile = a[rows, :] * carry + b[rows, :]
            carry = h_tile[SUBLANES - 1:SUBLANES, :]
            tiles.append(h_tile)
        h = jnp.concatenate(tiles, axis=0)
        ms = jnp.mean(h * h, axis=-1, keepdims=True)
        hl = h * lax.rsqrt(ms + EPS) * gn
        z = z_ref[c * R:(c + 1) * R, :].astype(F32)
        gelu = 0.5 * z * (1.0 + jnp.tanh(math.sqrt(2.0 / math.pi) * (z + 0.044715 * (z * z * z))))
        o_ref[c * R:(c + 1) * R, :] = (hl * gelu).astype(o_ref.dtype)


def _rglru(proj3, conv_w, conv_b, w_gates, b_rg, b_ig, lam, lru_norm_g):
    B, S, _ = proj3.shape
    NB = LRU_BLOCKS
    W = LRU_BLOCK_DIM
    u_off = 4 * RET_HEADS
    z_off = u_off + NB
    vec = pl.BlockSpec((1, W), lambda b, n: (0, n))
    return pl.pallas_call(
        _rglru_body,
        grid=(B, NB),
        in_specs=[
            pl.BlockSpec((None, S, W), lambda b, n: (b, 0, u_off + n)),
            pl.BlockSpec((None, S, W), lambda b, n: (b, 0, z_off + n)),
            pl.BlockSpec((CONV_WIDTH, W), lambda b, n: (0, n)),
            vec,
            pl.BlockSpec((None, W, 2 * W), lambda b, n: (n, 0, 0)),
            vec, vec, vec, vec,
        ],
        out_specs=pl.BlockSpec((None, S, W), lambda b, n: (b, 0, n)),
        out_shape=jax.ShapeDtypeStruct((B, S, NB * W), BF16),
        scratch_shapes=[pltpu.VMEM((SUBLANES + S, W), F32)],
        compiler_params=_params(("arbitrary", "arbitrary")),
        name="rglru",
    )(proj3, proj3, conv_w, conv_b, w_gates, b_rg, b_ig, lam, lru_norm_g)


def _rows_to_tokens(x2d):
    n = x2d.shape[1] // LANES
    parts = [x2d[:, s * LANES:(s + 1) * LANES] for s in range(n)]
    return jnp.swapaxes(jnp.stack(parts, axis=0), 0, 1)


def _tokens_to_rows(x3d):
    xt = jnp.swapaxes(x3d, 0, 1)
    return jnp.concatenate([xt[s] for s in range(x3d.shape[1])], axis=-1)


def _first_lane_of_max(v, lane):
    m = jnp.max(v, axis=-1, keepdims=True)
    idx = jnp.min(jnp.where(v == m, lane, LANES), axis=-1, keepdims=True)
    return m, idx


def _outproj_body(ret_ref, lru_ref, x_ref, wo_ref, g_ref, wr_hi_ref, wr_lo_ref, br_ref,
                  x1_ref, h2_ref, lg_ref):
    R = ret_ref.shape[1]
    acc = jnp.dot(ret_ref[...], wo_ref[:R, :], preferred_element_type=F32)
    acc = acc + jnp.dot(lru_ref[...], wo_ref[R:, :], preferred_element_type=F32)
    x1 = x_ref[...] + acc
    x1_ref[...] = x1
    ms = jnp.mean(x1 * x1, axis=-1, keepdims=True)
    h2 = x1 * lax.rsqrt(ms + EPS) * g_ref[...]
    h2_ref[...] = _rows_to_tokens(h2)
    h_hi = h2.astype(BF16)
    h_lo = (h2 - h_hi.astype(F32)).astype(BF16)
    wr_hi = wr_hi_ref[...]
    lg = jnp.dot(h_hi, wr_hi, preferred_element_type=F32)
    lg = lg + jnp.dot(h_lo, wr_hi, preferred_element_type=F32)
    lg = lg + jnp.dot(h_hi, wr_lo_ref[...], preferred_element_type=F32)
    lg_ref[...] = lg + br_ref[...]


def _outproj(ret2d, lru2d, x2d, wo_bf16, g, wr_hi, wr_lo, br, tm):
    T, D = x2d.shape
    R = ret2d.shape[1]
    L = lru2d.shape[1]
    NR = wr_hi.shape[1]
    const = lambda shape: pl.BlockSpec(shape, lambda i: (0, 0))
    return pl.pallas_call(
        _outproj_body,
        grid=(T // tm,),
        in_specs=[
            pl.BlockSpec((tm, R), lambda i: (i, 0)),
            pl.BlockSpec((tm, L), lambda i: (i, 0)),
            pl.BlockSpec((tm, D), lambda i: (i, 0)),
            const((R + L, D)), const((1, D)), const((D, NR)), const((D, NR)), const((1, NR)),
        ],
        out_specs=[
            pl.BlockSpec((tm, D), lambda i: (i, 0)),
            pl.BlockSpec((tm, D // LANES, LANES), lambda i: (i, 0, 0)),
            pl.BlockSpec((tm, NR), lambda i: (i, 0)),
        ],
        out_shape=[
            jax.ShapeDtypeStruct((T, D), F32),
            jax.ShapeDtypeStruct((T, D // LANES, LANES), F32),
            jax.ShapeDtypeStruct((T, NR), F32),
        ],
        compiler_params=_params(("arbitrary",)),
        name="outproj",
    )(ret2d, lru2d, x2d, wo_bf16, g, wr_hi, wr_lo, br)


def _route_body(lg_ref, route_ref, counts_ref, run_scr):
    G, EG = N_GROUPS, EXPERTS_PER_GROUP
    sub = ROUTE_SUB
    neg = -jnp.inf

    @pl.when(pl.program_id(0) == 0)
    def _():
        run_scr[...] = jnp.zeros_like(run_scr)

    lane = lax.broadcasted_iota(I32, (sub, LANES), 1)
    r_i = lax.broadcasted_iota(I32, (sub, sub), 0)
    c_i = lax.broadcasted_iota(I32, (sub, sub), 1)
    before = (c_i < r_i).astype(BF16)
    run = run_scr[...]
    for s in range(lg_ref.shape[0] // sub):
        rows = slice(s * sub, (s + 1) * sub)
        lg = lg_ref[rows, :]
        gl = jnp.where(lane < G, lg, neg)
        g_max, g_idx = _first_lane_of_max(gl, lane)
        g_p = 1.0 / jnp.sum(jnp.exp(gl - g_max), axis=-1, keepdims=True)
        lo = G + EG * g_idx
        el = jnp.where((lane >= lo) & (lane < lo + EG), lg, neg)
        t1, i1 = _first_lane_of_max(el, lane)
        el2 = jnp.where(lane == i1, neg, el)
        t2, i2 = _first_lane_of_max(el2, lane)
        p2 = jnp.exp(t2 - t1)
        w1 = g_p / (1.0 + p2)
        w2 = g_p * p2 / (1.0 + p2)
        e1 = i1 - G
        e2 = i2 - G
        oh1 = lane == e1
        oh2 = lane == e2
        oh = (oh1 | oh2).astype(F32)
        prefix = jnp.dot(before, oh.astype(BF16), preferred_element_type=F32) + run[0:1, :]
        rank1 = jnp.sum(jnp.where(oh1, prefix, 0.0), axis=-1, keepdims=True)
        rank2 = jnp.sum(jnp.where(oh2, prefix, 0.0), axis=-1, keepdims=True)
        run = run + jnp.sum(oh, axis=0, keepdims=True)
        cols = (e1.astype(F32), e2.astype(F32), w1, w2, rank1, rank2)
        route = jnp.zeros((sub, LANES), F32)
        for j, col in enumerate(cols):
            route = jnp.where(lane == j, col, route)
        route_ref[rows, :] = route
    run_scr[...] = run
    counts_ref[...] = run


def _route(logits, tm):
    T, NR = logits.shape
    return pl.pallas_call(
        _route_body,
        grid=(T // tm,),
        in_specs=[pl.BlockSpec((tm, NR), lambda i: (i, 0))],
        out_specs=[pl.BlockSpec((tm, NR), lambda i: (i, 0)),
                   pl.BlockSpec((SUBLANES, NR), lambda i: (0, 0))],
        out_shape=[jax.ShapeDtypeStruct((T, NR), F32), jax.ShapeDtypeStruct((SUBLANES, NR), F32)],
        scratch_shapes=[pltpu.VMEM((SUBLANES, NR), F32)],
        compiler_params=_params(("arbitrary",)),
        name="route",
    )(logits)


def _start_row_gather(src_hbm, idx_ref, dst, sem, lo, hi):
    for r in range(lo, hi):
        pltpu.make_async_copy(src_hbm.at[pl.ds(idx_ref[0, r], 1)], dst.at[pl.ds(r, 1)], sem).start()


def _zero_after(tile, n_rows, n_cols):
    bits = pltpu.bitcast(tile, jnp.uint32)
    zero = pltpu.bitcast((bits >> 16) >> 16, F32)
    return jnp.tile(zero, (n_rows // SUBLANES, n_cols // LANES))


def _wait_rows(dst, sem):
    pltpu.make_async_copy(dst, dst, sem).wait()


def _moe_body(be_ref, tok0_ref, tokn_ref, h2_hbm, wg_ref, wu_ref, wd_ref, y_ref, xbuf, sem):
    del be_ref
    i = pl.program_id(0)
    n = pl.num_programs(0)
    slot = lax.rem(i, 2)
    rows = xbuf.shape[1]

    @pl.when(i == 0)
    def _():
        _start_row_gather(h2_hbm, tok0_ref, xbuf.at[0], sem.at[0], 0, rows)

    _wait_rows(xbuf.at[slot], sem.at[slot])
    x = _tokens_to_rows(xbuf[slot]).astype(BF16)

    DE = wg_ref.shape[1]
    D = wd_ref.shape[1]
    n_up, n_down = DE // MOE_UP_COLS, D // MOE_DOWN_COLS
    per = rows // (n_up + n_down)
    cur, nxt, nsem = xbuf.at[slot], xbuf.at[1 - slot], sem.at[1 - slot]
    hparts = []
    for c in range(n_up):
        cs = slice(c * MOE_UP_COLS, (c + 1) * MOE_UP_COLS)
        _start_row_gather(h2_hbm, tokn_ref, nxt, nsem, c * per, (c + 1) * per)
        zero = _zero_after(cur[0, 0:SUBLANES, :], rows, MOE_UP_COLS)
        gate = jnp.dot(x, wg_ref[:, cs], preferred_element_type=F32)
        up = jnp.dot(x, wu_ref[:, cs], preferred_element_type=F32) + zero
        hparts.append(((gate / (1.0 + jnp.exp(-gate))) * up).astype(BF16))
    hmid = jnp.concatenate(hparts, axis=-1)
    for c in range(n_down):
        cs = slice(c * MOE_DOWN_COLS, (c + 1) * MOE_DOWN_COLS)
        g = n_up + c
        _start_row_gather(h2_hbm, tokn_ref, nxt, nsem, g * per, rows if c == n_down - 1 else (g + 1) * per)
        zero = _zero_after(cur[0, 0:SUBLANES, :], rows, MOE_DOWN_COLS)
        y_ref[:, cs] = jnp.dot(hmid, wd_ref[:, cs], preferred_element_type=F32) + zero

    @pl.when(i == n - 1)
    def _():
        _wait_rows(nxt, nsem)


def _moe(block_e, tok3, h2, wg_bf16, wu_bf16, wd_bf16):
    NB, _, rows = tok3.shape
    T, n_tiles, _ = h2.shape
    E, D, DE = wg_bf16.shape
    grid_spec = pltpu.PrefetchScalarGridSpec(
        num_scalar_prefetch=1,
        grid=(NB,),
        in_specs=[
            pl.BlockSpec((None, 1, rows), lambda i, be: (0, 0, 0), memory_space=pltpu.SMEM),
            pl.BlockSpec((None, 1, rows), lambda i, be: (jnp.minimum(i + 1, NB - 1), 0, 0),
                         memory_space=pltpu.SMEM),
            pl.BlockSpec(memory_space=pl.ANY),
            pl.BlockSpec((None, D, DE), lambda i, be: (be[i], 0, 0)),
            pl.BlockSpec((None, D, DE), lambda i, be: (be[i], 0, 0)),
            pl.BlockSpec((None, DE, D), lambda i, be: (be[i], 0, 0)),
        ],
        out_specs=pl.BlockSpec((rows, D), lambda i, be: (i, 0)),
        scratch_shapes=[pltpu.VMEM((2, rows, n_tiles, LANES), F32), pltpu.SemaphoreType.DMA((2,))],
    )
    return pl.pallas_call(
        _moe_body,
        grid_spec=grid_spec,
        out_shape=jax.ShapeDtypeStruct((NB * rows, D), F32),
        compiler_params=_params(("arbitrary",)),
        name="moe",
    )(block_e, tok3, tok3, h2, wg_bf16, wu_bf16, wd_bf16)


def _combine_body(d0_ref, dn_ref, y_hbm, x1_ref, w_ref, g_ref, o_ref, ybuf, sem):
    i = pl.program_id(0)
    n = pl.num_programs(0)
    slot = lax.rem(i, 2)
    tm = x1_ref.shape[0]

    @pl.when(i == 0)
    def _():
        _start_row_gather(y_hbm, d0_ref, ybuf.at[0], sem.at[0], 0, 2 * tm)

    _start_row_gather(y_hbm, dn_ref, ybuf.at[1 - slot], sem.at[1 - slot], 0, 2 * tm)
    _wait_rows(ybuf.at[slot], sem.at[slot])
    w = w_ref[...]
    y0 = ybuf[slot, :tm, :]
    y1 = ybuf[slot, tm:, :]
    x = x1_ref[...] + (w[:, 0:1] * y0 + w[:, 1:2] * y1)
    ms = jnp.mean(x * x, axis=-1, keepdims=True)
    o_ref[...] = x * lax.rsqrt(ms + EPS) * g_ref[...]

    @pl.when(i == n - 1)
    def _():
        _wait_rows(ybuf.at[1 - slot], sem.at[1 - slot])


def _combine(dest3, y, x1, e_w, g, tm):
    T, D = x1.shape
    NT = T // tm
    return pl.pallas_call(
        _combine_body,
        grid=(NT,),
        in_specs=[
            pl.BlockSpec((None, 1, 2 * tm), lambda i: (0, 0, 0), memory_space=pltpu.SMEM),
            pl.BlockSpec((None, 1, 2 * tm), lambda i: (jnp.minimum(i + 1, NT - 1), 0, 0),
                         memory_space=pltpu.SMEM),
            pl.BlockSpec(memory_space=pl.ANY),
            pl.BlockSpec((tm, D), lambda i: (i, 0)),
            pl.BlockSpec((tm, TOP_K), lambda i: (i, 0)),
            pl.BlockSpec((1, D), lambda i: (0, 0)),
        ],
        out_specs=pl.BlockSpec((tm, D), lambda i: (i, 0)),
        out_shape=jax.ShapeDtypeStruct((T, D), F32),
        scratch_shapes=[pltpu.VMEM((2, 2 * tm, D), F32), pltpu.SemaphoreType.DMA((2,))],
        compiler_params=_params(("arbitrary",)),
        name="combine",
    )(dest3, dest3, y, x1, e_w, g)


def _layout(route, counts, rows):
    T = route.shape[0]
    A = T * TOP_K
    e_id = route[:, 0:TOP_K].astype(I32)
    e_w = route[:, TOP_K:2 * TOP_K]
    rank = route[:, 2 * TOP_K:3 * TOP_K].astype(I32)
    counts = counts[0, :N_EXPERTS].astype(I32)
    padded = (counts + rows - 1) // rows * rows
    pad_end = jnp.cumsum(padded)
    pad_start = pad_end - padded
    experts = jnp.arange(N_EXPERTS, dtype=I32)
    start_of = jnp.sum(jnp.where(e_id[:, :, None] == experts, pad_start, 0), axis=-1)
    dest = start_of + rank
    n_blocks = (A + N_EXPERTS * (rows - 1)) // rows
    P = n_blocks * rows
    tok = jnp.zeros((P,), I32).at[dest.reshape(A)].set(
        jnp.arange(A, dtype=I32) // TOP_K, unique_indices=True, mode="promise_in_bounds")
    block_e = jnp.minimum(
        jnp.searchsorted(pad_end, jnp.arange(n_blocks, dtype=I32) * rows, side="right"),
        N_EXPERTS - 1).astype(I32)
    return e_w, dest, tok, block_e, n_blocks


def kernel(x, norm_mix_g, w_in, ret_norm_g, conv_w, conv_b, w_rg, b_rg, w_ig, b_ig, lru_lambda,
           lru_norm_g, w_out, norm_ffn_g, w_group, b_group, w_router, b_router, w_gate, w_up,
           w_down, norm_final_g):
    B, S, D = x.shape
    T = B * S
    depth = norm_mix_g.shape[0]
    assert depth == 1, "the combine kernel fuses the final norm, so only one layer is supported"
    H, d = RET_HEADS, HEAD_DIM

    half = d // 2
    inv = ROPE_BASE ** (-jnp.arange(half, dtype=F32) / half)
    ang = jnp.arange(S, dtype=F32)[:, None] * inv[None, :]
    cos_t = jnp.concatenate([jnp.cos(ang), jnp.cos(ang)], axis=-1)
    sin_t = jnp.concatenate([-jnp.sin(ang), jnp.sin(ang)], axis=-1)
    log_gamma = jnp.log1p(-(2.0 ** (-5.0 - jnp.arange(H, dtype=F32))))

    x2d = x.reshape(T, D)
    for l in range(depth):
        proj = _inproj(x2d, norm_mix_g[l][None, :], w_in[l].astype(BF16), INPROJ_TM, INPROJ_TN)
        proj3 = proj.reshape(B, S, proj.shape[1])
        ret = _retention(proj3, log_gamma, cos_t, sin_t, ret_norm_g[l][None, :])
        w_gates = jnp.concatenate([w_rg[l], w_ig[l]], axis=-1).astype(BF16)
        lru = _rglru(proj3, conv_w[l], conv_b[l][None, :], w_gates, b_rg[l][None, :],
                     b_ig[l][None, :], lru_lambda[l][None, :], lru_norm_g[l][None, :])

        n_route = N_GROUPS + N_EXPERTS
        wr = jnp.concatenate([w_group[l], w_router[l], jnp.zeros((D, LANES - n_route), F32)], axis=-1)
        br = jnp.concatenate([b_group[l], b_router[l], jnp.zeros((LANES - n_route,), F32)])[None, :]
        wr_hi = wr.astype(BF16)
        wr_lo = (wr - wr_hi.astype(F32)).astype(BF16)
        x1, h2, logits = _outproj(ret.reshape(T, -1), lru.reshape(T, -1), x2d, w_out[l].astype(BF16),
                                  norm_ffn_g[l][None, :], wr_hi, wr_lo, br, OUTPROJ_TM)
        route, counts = _route(logits, ROUTE_TM)

        e_w, dest, tok, block_e, n_blocks = _layout(route, counts, MOE_ROWS)
        y = _moe(block_e, tok.reshape(n_blocks, 1, MOE_ROWS), h2, w_gate[l].astype(BF16),
                 w_up[l].astype(BF16), w_down[l].astype(BF16))
        dest3 = dest.reshape(T // COMBINE_TM, COMBINE_TM, TOP_K).transpose(0, 2, 1).reshape(
            T // COMBINE_TM, 1, TOP_K * COMBINE_TM)
        x2d = _combine(dest3, y, x1, e_w, norm_final_g[None, :], COMBINE_TM)
    return x2d.reshape(B, S, D)
```

```python
import functools
import math

import jax
import jax.numpy as jnp
from jax import lax
from jax.experimental import pallas as pl
from jax.experimental.pallas import tpu as pltpu

F32 = jnp.float32
BF16 = jnp.bfloat16
I32 = jnp.int32

EPS = 1e-6
RET_HEADS = 8
HEAD_DIM = 128
RET_CHUNK = 128
ROPE_BASE = 10000.0
LRU_BLOCKS = 8
LRU_BLOCK_DIM = 128
CONV_WIDTH = 4
LRU_C = 8.0
N_GROUPS = 4
EXPERTS_PER_GROUP = 8
N_EXPERTS = N_GROUPS * EXPERTS_PER_GROUP
TOP_K = 2

LANES = 128
SUBLANES = 8
VMEM_LIMIT = 56 * 1024 * 1024

INPROJ_TM = 1024
INPROJ_TN = 1024
LRU_ROWS = 128
OUTPROJ_TM = 256
ROUTE_TM = 2048
ROUTE_SUB = 256
MOE_ROWS = 256
MOE_SLOTS = 3
MOE_UP_COLS = 256
MOE_DOWN_COLS = 512
COMBINE_TM = 256


def _params(sem):
    return pltpu.CompilerParams(dimension_semantics=sem, vmem_limit_bytes=VMEM_LIMIT)


def _inproj_body(x_ref, g_ref, w_ref, o_ref, h_scr):
    @pl.when(pl.program_id(1) == 0)
    def _():
        x = x_ref[...]
        ms = jnp.mean(x * x, axis=-1, keepdims=True)
        h_scr[...] = (x * lax.rsqrt(ms + EPS) * g_ref[...]).astype(BF16)

    o_ref[...] = jnp.dot(h_scr[...], w_ref[...], preferred_element_type=F32).astype(o_ref.dtype)


def _inproj(x2d, g, w_bf16, tm, tn):
    T, D = x2d.shape
    N = w_bf16.shape[1]
    return pl.pallas_call(
        _inproj_body,
        grid=(T // tm, N // tn),
        in_specs=[
            pl.BlockSpec((tm, D), lambda i, j: (i, 0)),
            pl.BlockSpec((1, D), lambda i, j: (0, 0)),
            pl.BlockSpec((D, tn), lambda i, j: (0, j)),
        ],
        out_specs=pl.BlockSpec((tm, tn), lambda i, j: (i, j)),
        out_shape=jax.ShapeDtypeStruct((T, N), BF16),
        scratch_shapes=[pltpu.VMEM((tm, D), BF16)],
        compiler_params=_params(("arbitrary", "arbitrary")),
        name="inproj",
    )(x2d, g, w_bf16)


def _cast_plan(w, n_steps):
    E, R, C = w.shape
    if n_steps >= E:
        parts = n_steps // E
        assert n_steps == E * parts and R % parts == 0
        return (None, R // parts, C), (lambda s: (s // parts, s % parts, 0))
    per_step = E // n_steps
    assert E == per_step * n_steps
    return (per_step, R, C), (lambda s: (s, 0, 0))


def _retention_body(lg_ref, q_ref, k_ref, v_ref, g_ref, cos_ref, sin_ref, gn_ref, wsrc_ref, o_ref,
                    wdst_ref, mask_scr, qdec_scr, kdec_scr):
    wdst_ref[...] = wsrc_ref[...].astype(wdst_ref.dtype)
    C = RET_CHUNK
    d = HEAD_DIM
    S = q_ref.shape[0]
    lg = lg_ref[pl.program_id(1)]
    row = lax.broadcasted_iota(I32, (C, d), 0).astype(F32)
    col = lax.broadcasted_iota(I32, (C, d), 1).astype(F32)
    rel = row - col
    scale = d ** -0.5
    mask_scr[...] = jnp.where(rel >= 0, jnp.exp(jnp.maximum(rel, 0.0) * lg), 0.0) * scale
    qdec_scr[...] = jnp.exp((row + 1.0) * lg) * scale
    kdec_scr[...] = jnp.exp((C - 1.0 - row) * lg)
    c_dec = jnp.exp(jnp.full((1, d), float(C), F32) * lg)
    gn = gn_ref[...]
    state = jnp.zeros((d, d), F32)

    for n in range(S // C):
        sl = pl.ds(n * C, C)
        cos = cos_ref[sl, :]
        sin = sin_ref[sl, :]
        q = q_ref[sl, :].astype(F32)
        k = k_ref[sl, :].astype(F32)
        v = v_ref[sl, :]
        q = q * cos + pltpu.roll(q, d // 2, 1) * sin
        k = k * cos + pltpu.roll(k, d // 2, 1) * sin
        scores = lax.dot_general(q.astype(BF16), k.astype(BF16), (((1,), (1,)), ((), ())),
                                 preferred_element_type=F32) * mask_scr[...]
        o = jnp.dot(scores.astype(BF16), v, preferred_element_type=F32)
        o = o + jnp.dot((q * qdec_scr[...]).astype(BF16), state.astype(BF16),
                        preferred_element_type=F32)
        kd_t = (k * kdec_scr[...]).T.astype(BF16)
        state = c_dec * state + jnp.dot(kd_t, v, preferred_element_type=F32)
        mu = jnp.mean(o, axis=-1, keepdims=True)
        oc = o - mu
        var = jnp.mean(oc * oc, axis=-1, keepdims=True)
        on = oc * lax.rsqrt(var + EPS) * gn
        g = g_ref[sl, :].astype(F32)
        o_ref[sl, :] = ((g / (1.0 + jnp.exp(-g))) * on).astype(o_ref.dtype)


def _retention(proj3, log_gamma, cos_t, sin_t, ret_norm_g, w_cast):
    B, S, _ = proj3.shape
    H = RET_HEADS
    d = HEAD_DIM
    blk = lambda off: pl.BlockSpec((None, S, d), lambda b, h, off=off: (b, 0, off + h))
    wshape, wmap = _cast_plan(w_cast, B * H)
    wspec = pl.BlockSpec(wshape, lambda b, h: wmap(b * H + h))
    return pl.pallas_call(
        _retention_body,
        grid=(B, H),
        in_specs=[
            pl.BlockSpec(memory_space=pltpu.SMEM),
            blk(0), blk(H), blk(2 * H), blk(3 * H),
            pl.BlockSpec((S, d), lambda b, h: (0, 0)),
            pl.BlockSpec((S, d), lambda b, h: (0, 0)),
            pl.BlockSpec((1, d), lambda b, h: (0, h)),
            wspec,
        ],
        out_specs=[pl.BlockSpec((None, S, d), lambda b, h: (b, 0, h)), wspec],
        out_shape=[jax.ShapeDtypeStruct((B, S, H * d), BF16),
                   jax.ShapeDtypeStruct(w_cast.shape, BF16)],
        scratch_shapes=[pltpu.VMEM((RET_CHUNK, d), F32)] * 3,
        compiler_params=_params(("arbitrary", "arbitrary")),
        name="retention",
    )(log_gamma, proj3, proj3, proj3, proj3, cos_t, sin_t, ret_norm_g, w_cast)


def _rglru_body(u_ref, z_ref, cw_ref, cb_ref, wg_ref, brg_ref, big_ref, lam_ref, gn_ref, wsrc_ref,
                o_ref, wdst_ref, uf_scr):
    wdst_ref[...] = wsrc_ref[...].astype(wdst_ref.dtype)
    S = u_ref.shape[0]
    W = LRU_BLOCK_DIM
    R = LRU_ROWS
    K = CONV_WIDTH
    nl = -lam_ref[...]
    softplus = jnp.maximum(nl, 0.0) + jnp.log1p(jnp.exp(-jnp.abs(nl)))
    coef = -LRU_C * softplus
    cw = cw_ref[...]
    cb = cb_ref[...]
    brg = brg_ref[...]
    big = big_ref[...]
    gn = gn_ref[...]
    wg = wg_ref[...]
    row_in_tile = lax.broadcasted_iota(I32, (R, W), 0) & (SUBLANES - 1)
    uf_scr[:SUBLANES, :] = jnp.zeros((SUBLANES, W), F32)
    carry = jnp.zeros((1, W), F32)

    for c in range(S // R):
        base = SUBLANES + c * R
        u = u_ref[c * R:(c + 1) * R, :].astype(F32)
        uf_scr[base:base + R, :] = u
        uc = cb + cw[K - 1:K, :] * u
        for j in range(1, K):
            uc = uc + cw[K - 1 - j:K - j, :] * uf_scr[base - j:base - j + R, :]
        gates = jnp.dot(uc.astype(BF16), wg, preferred_element_type=F32)
        r = 1.0 / (1.0 + jnp.exp(-(gates[:, :W] + brg)))
        i = 1.0 / (1.0 + jnp.exp(-(gates[:, W:] + big)))
        a = jnp.exp(coef * r)
        t = 1.0 - a * a
        b = jnp.where(t > 0.0, t * lax.rsqrt(t), 0.0) * (i * uc)
        for sh in (1, 2, 4):
            a_s = pltpu.roll(a, sh, 0)
            b_s = pltpu.roll(b, sh, 0)
            valid = row_in_tile >= sh
            b = jnp.where(valid, a * b_s + b, b)
            a = jnp.where(valid, a * a_s, a)
        tiles = []
        for k in range(R // SUBLANES):
            rows = slice(k * SUBLANES, (k + 1) * SUBLANES)
            h_tile = a[rows, :] * carry + b[rows, :]
            carry = h_tile[SUBLANES - 1:SUBLANES, :]
            tiles.append(h_tile)
        h = jnp.concatenate(tiles, axis=0)
        ms = jnp.mean(h * h, axis=-1, keepdims=True)
        hl = h * lax.rsqrt(ms + EPS) * gn
        z = z_ref[c * R:(c + 1) * R, :].astype(F32)
        gelu = 0.5 * z * (1.0 + jnp.tanh(math.sqrt(2.0 / math.pi) * (z + 0.044715 * (z * z * z))))
        o_ref[c * R:(c + 1) * R, :] = (hl * gelu).astype(o_ref.dtype)


def _rglru(proj3, conv_w, conv_b, w_gates, b_rg, b_ig, lam, lru_norm_g, w_cast):
    B, S, _ = proj3.shape
    NB = LRU_BLOCKS
    W = LRU_BLOCK_DIM
    u_off = 4 * RET_HEADS
    z_off = u_off + NB
    vec = pl.BlockSpec((1, W), lambda b, n: (0, n))
    wshape, wmap = _cast_plan(w_cast, B * NB)
    wspec = pl.BlockSpec(wshape, lambda b, n: wmap(b * NB + n))
    return pl.pallas_call(
        _rglru_body,
        grid=(B, NB),
        in_specs=[
            pl.BlockSpec((None, S, W), lambda b, n: (b, 0, u_off + n)),
            pl.BlockSpec((None, S, W), lambda b, n: (b, 0, z_off + n)),
            pl.BlockSpec((CONV_WIDTH, W), lambda b, n: (0, n)),
            vec,
            pl.BlockSpec((None, W, 2 * W), lambda b, n: (n, 0, 0)),
            vec, vec, vec, vec,
            wspec,
        ],
        out_specs=[pl.BlockSpec((None, S, W), lambda b, n: (b, 0, n)), wspec],
        out_shape=[jax.ShapeDtypeStruct((B, S, NB * W), BF16),
                   jax.ShapeDtypeStruct(w_cast.shape, BF16)],
        scratch_shapes=[pltpu.VMEM((SUBLANES + S, W), F32)],
        compiler_params=_params(("arbitrary", "arbitrary")),
        name="rglru",
    )(proj3, proj3, conv_w, conv_b, w_gates, b_rg, b_ig, lam, lru_norm_g, w_cast)


def _rows_to_tokens(x2d):
    n = x2d.shape[1] // LANES
    parts = [x2d[:, s * LANES:(s + 1) * LANES] for s in range(n)]
    return jnp.swapaxes(jnp.stack(parts, axis=0), 0, 1)


def _tokens_to_rows(x3d):
    xt = jnp.swapaxes(x3d, 0, 1)
    return jnp.concatenate([xt[s] for s in range(x3d.shape[1])], axis=-1)


def _first_lane_of_max(v, lane):
    m = jnp.max(v, axis=-1, keepdims=True)
    idx = jnp.min(jnp.where(v == m, lane, LANES), axis=-1, keepdims=True)
    return m, idx


def _outproj_body(ret_ref, lru_ref, x_ref, wo_ref, g_ref, wr_hi_ref, wr_lo_ref, br_ref, wsrc_ref,
                  x1_ref, h2_ref, lg_ref, wdst_ref):
    wdst_ref[...] = wsrc_ref[...].astype(wdst_ref.dtype)
    R = ret_ref.shape[1]
    acc = jnp.dot(ret_ref[...], wo_ref[:R, :], preferred_element_type=F32)
    acc = acc + jnp.dot(lru_ref[...], wo_ref[R:, :], preferred_element_type=F32)
    x1 = x_ref[...] + acc
    x1_ref[...] = x1
    ms = jnp.mean(x1 * x1, axis=-1, keepdims=True)
    h2 = x1 * lax.rsqrt(ms + EPS) * g_ref[...]
    h2_ref[...] = _rows_to_tokens(h2)
    h_hi = h2.astype(BF16)
    h_lo = (h2 - h_hi.astype(F32)).astype(BF16)
    wr_hi = wr_hi_ref[...]
    lg = jnp.dot(h_hi, wr_hi, preferred_element_type=F32)
    lg = lg + jnp.dot(h_lo, wr_hi, preferred_element_type=F32)
    lg = lg + jnp.dot(h_hi, wr_lo_ref[...], preferred_element_type=F32)
    lg_ref[...] = lg + br_ref[...]


def _outproj(ret2d, lru2d, x2d, wo_bf16, g, wr_hi, wr_lo, br, w_cast, tm):
    T, D = x2d.shape
    R = ret2d.shape[1]
    L = lru2d.shape[1]
    NR = wr_hi.shape[1]
    const = lambda shape: pl.BlockSpec(shape, lambda i: (0, 0))
    wshape, wmap = _cast_plan(w_cast, T // tm)
    wspec = pl.BlockSpec(wshape, wmap)
    return pl.pallas_call(
        _outproj_body,
        grid=(T // tm,),
        in_specs=[
            pl.BlockSpec((tm, R), lambda i: (i, 0)),
            pl.BlockSpec((tm, L), lambda i: (i, 0)),
            pl.BlockSpec((tm, D), lambda i: (i, 0)),
            const((R + L, D)), const((1, D)), const((D, NR)), const((D, NR)), const((1, NR)),
            wspec,
        ],
        out_specs=[
            pl.BlockSpec((tm, D), lambda i: (i, 0)),
            pl.BlockSpec((tm, D // LANES, LANES), lambda i: (i, 0, 0)),
            pl.BlockSpec((tm, NR), lambda i: (i, 0)),
            wspec,
        ],
        out_shape=[
            jax.ShapeDtypeStruct((T, D), F32),
            jax.ShapeDtypeStruct((T, D // LANES, LANES), F32),
            jax.ShapeDtypeStruct((T, NR), F32),
            jax.ShapeDtypeStruct(w_cast.shape, BF16),
        ],
        compiler_params=_params(("arbitrary",)),
        name="outproj",
    )(ret2d, lru2d, x2d, wo_bf16, g, wr_hi, wr_lo, br, w_cast)


def _route_body(lg_ref, route_ref, counts_ref, run_scr):
    G, EG = N_GROUPS, EXPERTS_PER_GROUP
    sub = ROUTE_SUB
    neg = -jnp.inf

    @pl.when(pl.program_id(0) == 0)
    def _():
        run_scr[...] = jnp.zeros_like(run_scr)

    lane = lax.broadcasted_iota(I32, (sub, LANES), 1)
    r_i = lax.broadcasted_iota(I32, (sub, sub), 0)
    c_i = lax.broadcasted_iota(I32, (sub, sub), 1)
    before = (c_i < r_i).astype(BF16)
    run = run_scr[...]
    for s in range(lg_ref.shape[0] // sub):
        rows = slice(s * sub, (s + 1) * sub)
        lg = lg_ref[rows, :]
        gl = jnp.where(lane < G, lg, neg)
        g_max, g_idx = _first_lane_of_max(gl, lane)
        g_p = 1.0 / jnp.sum(jnp.exp(gl - g_max), axis=-1, keepdims=True)
        lo = G + EG * g_idx
        el = jnp.where((lane >= lo) & (lane < lo + EG), lg, neg)
        t1, i1 = _first_lane_of_max(el, lane)
        el2 = jnp.where(lane == i1, neg, el)
        t2, i2 = _first_lane_of_max(el2, lane)
        p2 = jnp.exp(t2 - t1)
        w1 = g_p / (1.0 + p2)
        w2 = g_p * p2 / (1.0 + p2)
        e1 = i1 - G
        e2 = i2 - G
        oh1 = lane == e1
        oh2 = lane == e2
        oh = (oh1 | oh2).astype(F32)
        prefix = jnp.dot(before, oh.astype(BF16), preferred_element_type=F32) + run[0:1, :]
        rank1 = jnp.sum(jnp.where(oh1, prefix, 0.0), axis=-1, keepdims=True)
        rank2 = jnp.sum(jnp.where(oh2, prefix, 0.0), axis=-1, keepdims=True)
        run = run + jnp.sum(oh, axis=0, keepdims=True)
        cols = (e1.astype(F32), e2.astype(F32), w1, w2, rank1, rank2)
        route = jnp.zeros((sub, LANES), F32)
        for j, col in enumerate(cols):
            route = jnp.where(lane == j, col, route)
        route_ref[rows, :] = route
    run_scr[...] = run
    counts_ref[...] = run


def _route(logits, tm):
    T, NR = logits.shape
    return pl.pallas_call(
        _route_body,
        grid=(T // tm,),
        in_specs=[pl.BlockSpec((tm, NR), lambda i: (i, 0))],
        out_specs=[pl.BlockSpec((tm, NR), lambda i: (i, 0)),
                   pl.BlockSpec((SUBLANES, NR), lambda i: (0, 0))],
        out_shape=[jax.ShapeDtypeStruct((T, NR), F32), jax.ShapeDtypeStruct((SUBLANES, NR), F32)],
        scratch_shapes=[pltpu.VMEM((SUBLANES, NR), F32)],
        compiler_params=_params(("arbitrary",)),
        name="route",
    )(logits)


def _start_row_gather(src_hbm, idx_ref, dst, sem, lo, hi):
    for r in range(lo, hi):
        pltpu.make_async_copy(src_hbm.at[pl.ds(idx_ref[0, r], 1)], dst.at[pl.ds(r, 1)], sem).start(
            priority=r % 2)


def _zero_after(tile, n_rows, n_cols):
    bits = pltpu.bitcast(tile, jnp.uint32)
    zero = pltpu.bitcast((bits >> 16) >> 16, F32)
    return jnp.tile(zero, (n_rows // SUBLANES, n_cols // LANES))


def _wait_rows(dst, sem):
    pltpu.make_async_copy(dst, dst, sem).wait()


def _moe_body(be_ref, tok0_ref, tok1_ref, tokn_ref, h2_hbm, wg_ref, wu_ref, wd_ref, y_ref, xbuf, sem):
    del be_ref
    i = pl.program_id(0)
    n = pl.num_programs(0)
    slot = lax.rem(i, MOE_SLOTS)
    slot1 = lax.rem(i + 1, MOE_SLOTS)
    slot2 = lax.rem(i + 2, MOE_SLOTS)
    rows = xbuf.shape[1]

    @pl.when(i == 0)
    def _():
        _start_row_gather(h2_hbm, tok0_ref, xbuf.at[0], sem.at[0], 0, rows)
        _start_row_gather(h2_hbm, tok1_ref, xbuf.at[1], sem.at[1], 0, rows)

    _wait_rows(xbuf.at[slot], sem.at[slot])
    x = _tokens_to_rows(xbuf[slot]).astype(BF16)

    DE = wg_ref.shape[1]
    D = wd_ref.shape[1]
    n_up, n_down = DE // MOE_UP_COLS, D // MOE_DOWN_COLS
    per = rows // (n_up + n_down)
    cur, nxt, nsem = xbuf.at[slot], xbuf.at[slot2], sem.at[slot2]
    hparts = []
    for c in range(n_up):
        cs = slice(c * MOE_UP_COLS, (c + 1) * MOE_UP_COLS)
        _start_row_gather(h2_hbm, tokn_ref, nxt, nsem, c * per, (c + 1) * per)
        zero = _zero_after(cur[0, 0:SUBLANES, :], rows, MOE_UP_COLS)
        gate = jnp.dot(x, wg_ref[:, cs], preferred_element_type=F32)
        up = jnp.dot(x, wu_ref[:, cs], preferred_element_type=F32) + zero
        hparts.append(((gate / (1.0 + jnp.exp(-gate))) * up).astype(BF16))
    hmid = jnp.concatenate(hparts, axis=-1)
    for c in range(n_down):
        cs = slice(c * MOE_DOWN_COLS, (c + 1) * MOE_DOWN_COLS)
        g = n_up + c
        _start_row_gather(h2_hbm, tokn_ref, nxt, nsem, g * per, rows if c == n_down - 1 else (g + 1) * per)
        zero = _zero_after(cur[0, 0:SUBLANES, :], rows, MOE_DOWN_COLS)
        y_ref[:, cs] = jnp.dot(hmid, wd_ref[:, cs], preferred_element_type=F32) + zero

    @pl.when(i == n - 1)
    def _():
        _wait_rows(xbuf.at[slot1], sem.at[slot1])
        _wait_rows(nxt, nsem)


def _moe(block_e, tok3, h2, wg_bf16, wu_bf16, wd_bf16):
    NB, _, rows = tok3.shape
    T, n_tiles, _ = h2.shape
    E, D, DE = wg_bf16.shape
    grid_spec = pltpu.PrefetchScalarGridSpec(
        num_scalar_prefetch=1,
        grid=(NB,),
        in_specs=[
            pl.BlockSpec((None, 1, rows), lambda i, be: (0, 0, 0), memory_space=pltpu.SMEM),
            pl.BlockSpec((None, 1, rows), lambda i, be: (min(1, NB - 1), 0, 0),
                         memory_space=pltpu.SMEM),
            pl.BlockSpec((None, 1, rows), lambda i, be: (jnp.minimum(i + 2, NB - 1), 0, 0),
                         memory_space=pltpu.SMEM),
            pl.BlockSpec(memory_space=pl.ANY),
            pl.BlockSpec((None, D, DE), lambda i, be: (be[i], 0, 0)),
            pl.BlockSpec((None, D, DE), lambda i, be: (be[i], 0, 0)),
            pl.BlockSpec((None, DE, D), lambda i, be: (be[i], 0, 0)),
        ],
        out_specs=pl.BlockSpec((rows, D), lambda i, be: (i, 0)),
        scratch_shapes=[pltpu.VMEM((MOE_SLOTS, rows, n_tiles, LANES), F32),
                        pltpu.SemaphoreType.DMA((MOE_SLOTS,))],
    )
    return pl.pallas_call(
        _moe_body,
        grid_spec=grid_spec,
        out_shape=jax.ShapeDtypeStruct((NB * rows, D), F32),
        compiler_params=_params(("arbitrary",)),
        name="moe",
    )(block_e, tok3, tok3, tok3, h2, wg_bf16, wu_bf16, wd_bf16)


def _combine_body(d0_ref, dn_ref, y_hbm, x1_ref, w_ref, g_ref, o_ref, ybuf, sem):
    i = pl.program_id(0)
    n = pl.num_programs(0)
    slot = lax.rem(i, 2)
    tm = x1_ref.shape[0]

    @pl.when(i == 0)
    def _():
        _start_row_gather(y_hbm, d0_ref, ybuf.at[0], sem.at[0], 0, 2 * tm)

    _start_row_gather(y_hbm, dn_ref, ybuf.at[1 - slot], sem.at[1 - slot], 0, 2 * tm)
    _wait_rows(ybuf.at[slot], sem.at[slot])
    w = w_ref[...]
    y0 = ybuf[slot, :tm, :]
    y1 = ybuf[slot, tm:, :]
    x = x1_ref[...] + (w[:, 0:1] * y0 + w[:, 1:2] * y1)
    ms = jnp.mean(x * x, axis=-1, keepdims=True)
    o_ref[...] = x * lax.rsqrt(ms + EPS) * g_ref[...]

    @pl.when(i == n - 1)
    def _():
        _wait_rows(ybuf.at[1 - slot], sem.at[1 - slot])


def _combine(dest3, y, x1, e_w, g, tm):
    T, D = x1.shape
    NT = T // tm
    return pl.pallas_call(
        _combine_body,
        grid=(NT,),
        in_specs=[
            pl.BlockSpec((None, 1, 2 * tm), lambda i: (0, 0, 0), memory_space=pltpu.SMEM),
            pl.BlockSpec((None, 1, 2 * tm), lambda i: (jnp.minimum(i + 1, NT - 1), 0, 0),
                         memory_space=pltpu.SMEM),
            pl.BlockSpec(memory_space=pl.ANY),
            pl.BlockSpec((tm, D), lambda i: (i, 0)),
            pl.BlockSpec((tm, TOP_K), lambda i: (i, 0)),
            pl.BlockSpec((1, D), lambda i: (0, 0)),
        ],
        out_specs=pl.BlockSpec((tm, D), lambda i: (i, 0)),
        out_shape=jax.ShapeDtypeStruct((T, D), F32),
        scratch_shapes=[pltpu.VMEM((2, 2 * tm, D), F32), pltpu.SemaphoreType.DMA((2,))],
        compiler_params=_params(("arbitrary",)),
        name="combine",
    )(dest3, dest3, y, x1, e_w, g)


def _layout(route, counts, rows):
    T = route.shape[0]
    A = T * TOP_K
    e_id = route[:, 0:TOP_K].astype(I32)
    e_w = route[:, TOP_K:2 * TOP_K]
    rank = route[:, 2 * TOP_K:3 * TOP_K].astype(I32)
    counts = counts[0, :N_EXPERTS].astype(I32)
    padded = (counts + rows - 1) // rows * rows
    pad_end = jnp.cumsum(padded)
    pad_start = pad_end - padded
    experts = jnp.arange(N_EXPERTS, dtype=I32)
    start_of = jnp.sum(jnp.where(e_id[:, :, None] == experts, pad_start, 0), axis=-1)
    dest = start_of + rank
    n_blocks = (A + N_EXPERTS * (rows - 1)) // rows
    P = n_blocks * rows
    tok = jnp.zeros((P,), I32).at[dest.reshape(A)].set(
        jnp.arange(A, dtype=I32) // TOP_K, unique_indices=True, mode="promise_in_bounds")
    block_start = jnp.arange(n_blocks, dtype=I32) * rows
    block_e = jnp.minimum(jnp.sum((pad_end[None, :] <= block_start[:, None]).astype(I32), axis=1),
                          N_EXPERTS - 1)
    return e_w, dest, tok, block_e, n_blocks


def kernel(x, norm_mix_g, w_in, ret_norm_g, conv_w, conv_b, w_rg, b_rg, w_ig, b_ig, lru_lambda,
           lru_norm_g, w_out, norm_ffn_g, w_group, b_group, w_router, b_router, w_gate, w_up,
           w_down, norm_final_g):
    B, S, D = x.shape
    T = B * S
    depth = norm_mix_g.shape[0]
    assert depth == 1, "the combine kernel fuses the final norm, so only one layer is supported"
    H, d = RET_HEADS, HEAD_DIM

    half = d // 2
    inv = ROPE_BASE ** (-jnp.arange(half, dtype=F32) / half)
    ang = jnp.arange(S, dtype=F32)[:, None] * inv[None, :]
    cos_t = jnp.concatenate([jnp.cos(ang), jnp.cos(ang)], axis=-1)
    sin_t = jnp.concatenate([-jnp.sin(ang), jnp.sin(ang)], axis=-1)
    log_gamma = jnp.log1p(-(2.0 ** (-5.0 - jnp.arange(H, dtype=F32))))

    x2d = x.reshape(T, D)
    for l in range(depth):
        proj = _inproj(x2d, norm_mix_g[l][None, :], w_in[l].astype(BF16), INPROJ_TM, INPROJ_TN)
        proj3 = proj.reshape(B, S, proj.shape[1])
        ret, wg_bf16 = _retention(proj3, log_gamma, cos_t, sin_t, ret_norm_g[l][None, :], w_gate[l])
        w_gates = jnp.concatenate([w_rg[l], w_ig[l]], axis=-1).astype(BF16)
        lru, wu_bf16 = _rglru(proj3, conv_w[l], conv_b[l][None, :], w_gates, b_rg[l][None, :],
                              b_ig[l][None, :], lru_lambda[l][None, :], lru_norm_g[l][None, :],
                              w_up[l])

        n_route = N_GROUPS + N_EXPERTS
        wr = jnp.concatenate([w_group[l], w_router[l], jnp.zeros((D, LANES - n_route), F32)], axis=-1)
        br = jnp.concatenate([b_group[l], b_router[l], jnp.zeros((LANES - n_route,), F32)])[None, :]
        wr_hi = wr.astype(BF16)
        wr_lo = (wr - wr_hi.astype(F32)).astype(BF16)
        x1, h2, logits, wd_bf16 = _outproj(ret.reshape(T, -1), lru.reshape(T, -1), x2d,
                                           w_out[l].astype(BF16), norm_ffn_g[l][None, :], wr_hi,
                                           wr_lo, br, w_down[l], OUTPROJ_TM)
        route, counts = _route(logits, ROUTE_TM)

        e_w, dest, tok, block_e, n_blocks = _layout(route, counts, MOE_ROWS)
        y = _moe(block_e, tok.reshape(n_blocks, 1, MOE_ROWS), h2, wg_bf16, wu_bf16, wd_bf16)
        dest3 = dest.reshape(T // COMBINE_TM, COMBINE_TM, TOP_K).transpose(0, 2, 1).reshape(
            T // COMBINE_TM, 1, TOP_K * COMBINE_TM)
        x2d = _combine(dest3, y, x1, e_w, norm_final_g[None, :], COMBINE_TM)
    return x2d.reshape(B, S, D)
```

```python
import functools
import math

import jax
import jax.numpy as jnp
from jax import lax
from jax.experimental import pallas as pl
from jax.experimental.pallas import tpu as pltpu

F32 = jnp.float32
BF16 = jnp.bfloat16
I32 = jnp.int32

EPS = 1e-6
RET_HEADS = 8
HEAD_DIM = 128
RET_CHUNK = 128
ROPE_BASE = 10000.0
LRU_BLOCKS = 8
LRU_BLOCK_DIM = 128
CONV_WIDTH = 4
LRU_C = 8.0
N_GROUPS = 4
EXPERTS_PER_GROUP = 8
N_EXPERTS = N_GROUPS * EXPERTS_PER_GROUP
TOP_K = 2

LANES = 128
SUBLANES = 8
VMEM_LIMIT = 56 * 1024 * 1024

INPROJ_TM = 1024
INPROJ_TN = 1024
LRU_ROWS = 128
OUTPROJ_TM = 256
ROUTE_TM = 2048
ROUTE_SUB = 256
MOE_ROWS = 256
MOE_SLOTS = 4
MOE_UP_COLS = 256
MOE_DOWN_COLS = 512
COMBINE_TM = 256


def _params(sem):
    return pltpu.CompilerParams(dimension_semantics=sem, vmem_limit_bytes=VMEM_LIMIT)


def _inproj_body(x_ref, g_ref, w_ref, o_ref, h_scr):
    @pl.when(pl.program_id(1) == 0)
    def _():
        x = x_ref[...]
        ms = jnp.mean(x * x, axis=-1, keepdims=True)
        h_scr[...] = (x * lax.rsqrt(ms + EPS) * g_ref[...]).astype(BF16)

    o_ref[...] = jnp.dot(h_scr[...], w_ref[...], preferred_element_type=F32).astype(o_ref.dtype)


def _inproj(x2d, g, w_bf16, tm, tn):
    T, D = x2d.shape
    N = w_bf16.shape[1]
    return pl.pallas_call(
        _inproj_body,
        grid=(T // tm, N // tn),
        in_specs=[
            pl.BlockSpec((tm, D), lambda i, j: (i, 0)),
            pl.BlockSpec((1, D), lambda i, j: (0, 0)),
            pl.BlockSpec((D, tn), lambda i, j: (0, j)),
        ],
        out_specs=pl.BlockSpec((tm, tn), lambda i, j: (i, j)),
        out_shape=jax.ShapeDtypeStruct((T, N), BF16),
        scratch_shapes=[pltpu.VMEM((tm, D), BF16)],
        compiler_params=_params(("arbitrary", "arbitrary")),
        name="inproj",
    )(x2d, g, w_bf16)


def _cast_plan(w, n_steps):
    E, R, C = w.shape
    if n_steps >= E:
        parts = n_steps // E
        assert n_steps == E * parts and R % parts == 0
        return (None, R // parts, C), (lambda s: (s // parts, s % parts, 0))
    per_step = E // n_steps
    assert E == per_step * n_steps
    return (per_step, R, C), (lambda s: (s, 0, 0))


def _retention_body(lg_ref, q_ref, k_ref, v_ref, g_ref, cos_ref, sin_ref, gn_ref, wsrc_ref, o_ref,
                    wdst_ref, mask_scr, qdec_scr, kdec_scr):
    wdst_ref[...] = wsrc_ref[...].astype(wdst_ref.dtype)
    C = RET_CHUNK
    d = HEAD_DIM
    S = q_ref.shape[0]
    lg = lg_ref[pl.program_id(1)]
    row = lax.broadcasted_iota(I32, (C, d), 0).astype(F32)
    col = lax.broadcasted_iota(I32, (C, d), 1).astype(F32)
    rel = row - col
    scale = d ** -0.5
    mask_scr[...] = jnp.where(rel >= 0, jnp.exp(jnp.maximum(rel, 0.0) * lg), 0.0) * scale
    qdec_scr[...] = jnp.exp((row + 1.0) * lg) * scale
    kdec_scr[...] = jnp.exp((C - 1.0 - row) * lg)
    c_dec = jnp.exp(jnp.full((1, d), float(C), F32) * lg)
    gn = gn_ref[...]
    state = jnp.zeros((d, d), F32)

    for n in range(S // C):
        sl = pl.ds(n * C, C)
        cos = cos_ref[sl, :]
        sin = sin_ref[sl, :]
        q = q_ref[sl, :].astype(F32)
        k = k_ref[sl, :].astype(F32)
        v = v_ref[sl, :]
        q = q * cos + pltpu.roll(q, d // 2, 1) * sin
        k = k * cos + pltpu.roll(k, d // 2, 1) * sin
        scores = lax.dot_general(q.astype(BF16), k.astype(BF16), (((1,), (1,)), ((), ())),
                                 preferred_element_type=F32) * mask_scr[...]
        o = jnp.dot(scores.astype(BF16), v, preferred_element_type=F32)
        o = o + jnp.dot((q * qdec_scr[...]).astype(BF16), state.astype(BF16),
                        preferred_element_type=F32)
        kd_t = (k * kdec_scr[...]).T.astype(BF16)
        state = c_dec * state + jnp.dot(kd_t, v, preferred_element_type=F32)
        mu = jnp.mean(o, axis=-1, keepdims=True)
        oc = o - mu
        var = jnp.mean(oc * oc, axis=-1, keepdims=True)
        on = oc * lax.rsqrt(var + EPS) * gn
        g = g_ref[sl, :].astype(F32)
        o_ref[sl, :] = ((g / (1.0 + jnp.exp(-g))) * on).astype(o_ref.dtype)


def _retention(proj3, log_gamma, cos_t, sin_t, ret_norm_g, w_cast):
    B, S, _ = proj3.shape
    H = RET_HEADS
    d = HEAD_DIM
    blk = lambda off: pl.BlockSpec((None, S, d), lambda b, h, off=off: (b, 0, off + h))
    wshape, wmap = _cast_plan(w_cast, B * H)
    wspec = pl.BlockSpec(wshape, lambda b, h: wmap(b * H + h))
    return pl.pallas_call(
        _retention_body,
        grid=(B, H),
        in_specs=[
            pl.BlockSpec(memory_space=pltpu.SMEM),
            blk(0), blk(H), blk(2 * H), blk(3 * H),
            pl.BlockSpec((S, d), lambda b, h: (0, 0)),
            pl.BlockSpec((S, d), lambda b, h: (0, 0)),
            pl.BlockSpec((1, d), lambda b, h: (0, h)),
            wspec,
        ],
        out_specs=[pl.BlockSpec((None, S, d), lambda b, h: (b, 0, h)), wspec],
        out_shape=[jax.ShapeDtypeStruct((B, S, H * d), BF16),
                   jax.ShapeDtypeStruct(w_cast.shape, BF16)],
        scratch_shapes=[pltpu.VMEM((RET_CHUNK, d), F32)] * 3,
        compiler_params=_params(("arbitrary", "arbitrary")),
        name="retention",
    )(log_gamma, proj3, proj3, proj3, proj3, cos_t, sin_t, ret_norm_g, w_cast)


def _rglru_body(u_ref, z_ref, cw_ref, cb_ref, wg_ref, brg_ref, big_ref, lam_ref, gn_ref, wsrc_ref,
                o_ref, wdst_ref, uf_scr):
    wdst_ref[...] = wsrc_ref[...].astype(wdst_ref.dtype)
    S = u_ref.shape[0]
    W = LRU_BLOCK_DIM
    R = LRU_ROWS
    K = CONV_WIDTH
    nl = -lam_ref[...]
    softplus = jnp.maximum(nl, 0.0) + jnp.log1p(jnp.exp(-jnp.abs(nl)))
    coef = -LRU_C * softplus
    cw = cw_ref[...]
    cb = cb_ref[...]
    brg = brg_ref[...]
    big = big_ref[...]
    gn = gn_ref[...]
    wg = wg_ref[...]
    row_in_tile = lax.broadcasted_iota(I32, (R, W), 0) & (SUBLANES - 1)
    uf_scr[:SUBLANES, :] = jnp.zeros((SUBLANES, W), F32)
    carry = jnp.zeros((1, W), F32)

    for c in range(S // R):
        base = SUBLANES + c * R
        u = u_ref[c * R:(c + 1) * R, :].astype(F32)
        uf_scr[base:base + R, :] = u
        uc = cb + cw[K - 1:K, :] * u
        for j in range(1, K):
            uc = uc + cw[K - 1 - j:K - j, :] * uf_scr[base - j:base - j + R, :]
        gates = jnp.dot(uc.astype(BF16), wg, preferred_element_type=F32)
        r = 1.0 / (1.0 + jnp.exp(-(gates[:, :W] + brg)))
        i = 1.0 / (1.0 + jnp.exp(-(gates[:, W:] + big)))
        a = jnp.exp(coef * r)
        t = 1.0 - a * a
        b = jnp.where(t > 0.0, t * lax.rsqrt(t), 0.0) * (i * uc)
        for sh in (1, 2, 4):
            a_s = pltpu.roll(a, sh, 0)
            b_s = pltpu.roll(b, sh, 0)
            valid = row_in_tile >= sh
            b = jnp.where(valid, a * b_s + b, b)
            a = jnp.where(valid, a * a_s, a)
        tiles = []
        for k in range(R // SUBLANES):
            rows = slice(k * SUBLANES, (k + 1) * SUBLANES)
            h_tile = a[rows, :] * carry + b[rows, :]
            carry = h_tile[SUBLANES - 1:SUBLANES, :]
            tiles.append(h_tile)
        h = jnp.concatenate(tiles, axis=0)
        ms = jnp.mean(h * h, axis=-1, keepdims=True)
        hl = h * lax.rsqrt(ms + EPS) * gn
        z = z_ref[c * R:(c + 1) * R, :].astype(F32)
        gelu = 0.5 * z * (1.0 + jnp.tanh(math.sqrt(2.0 / math.pi) * (z + 0.044715 * (z * z * z))))
        o_ref[c * R:(c + 1) * R, :] = (hl * gelu).astype(o_ref.dtype)


def _rglru(proj3, conv_w, conv_b, w_gates, b_rg, b_ig, lam, lru_norm_g, w_cast):
    B, S, _ = proj3.shape
    NB = LRU_BLOCKS
    W = LRU_BLOCK_DIM
    u_off = 4 * RET_HEADS
    z_off = u_off + NB
    vec = pl.BlockSpec((1, W), lambda b, n: (0, n))
    wshape, wmap = _cast_plan(w_cast, B * NB)
    wspec = pl.BlockSpec(wshape, lambda b, n: wmap(b * NB + n))
    return pl.pallas_call(
        _rglru_body,
        grid=(B, NB),
        in_specs=[
            pl.BlockSpec((None, S, W), lambda b, n: (b, 0, u_off + n)),
            pl.BlockSpec((None, S, W), lambda b, n: (b, 0, z_off + n)),
            pl.BlockSpec((CONV_WIDTH, W), lambda b, n: (0, n)),
            vec,
            pl.BlockSpec((None, W, 2 * W), lambda b, n: (n, 0, 0)),
            vec, vec, vec, vec,
            wspec,
        ],
        out_specs=[pl.BlockSpec((None, S, W), lambda b, n: (b, 0, n)), wspec],
        out_shape=[jax.ShapeDtypeStruct((B, S, NB * W), BF16),
                   jax.ShapeDtypeStruct(w_cast.shape, BF16)],
        scratch_shapes=[pltpu.VMEM((SUBLANES + S, W), F32)],
        compiler_params=_params(("arbitrary", "arbitrary")),
        name="rglru",
    )(proj3, proj3, conv_w, conv_b, w_gates, b_rg, b_ig, lam, lru_norm_g, w_cast)


def _rows_to_tokens(x2d):
    n = x2d.shape[1] // LANES
    parts = [x2d[:, s * LANES:(s + 1) * LANES] for s in range(n)]
    return jnp.swapaxes(jnp.stack(parts, axis=0), 0, 1)


def _tokens_to_rows(x3d):
    xt = jnp.swapaxes(x3d, 0, 1)
    return jnp.concatenate([xt[s] for s in range(x3d.shape[1])], axis=-1)


def _first_lane_of_max(v, lane):
    m = jnp.max(v, axis=-1, keepdims=True)
    idx = jnp.min(jnp.where(v == m, lane, LANES), axis=-1, keepdims=True)
    return m, idx


def _outproj_body(ret_ref, lru_ref, x_ref, wo_ref, g_ref, wr_ref, br_ref, wsrc_ref,
                  x1_ref, h2_ref, lg_ref, wdst_ref):
    wdst_ref[...] = wsrc_ref[...].astype(wdst_ref.dtype)
    R = ret_ref.shape[1]
    acc = jnp.dot(ret_ref[...], wo_ref[:R, :], preferred_element_type=F32)
    acc = acc + jnp.dot(lru_ref[...], wo_ref[R:, :], preferred_element_type=F32)
    x1 = x_ref[...] + acc
    x1_ref[...] = x1
    ms = jnp.mean(x1 * x1, axis=-1, keepdims=True)
    h2 = x1 * lax.rsqrt(ms + EPS) * g_ref[...]
    h2_ref[...] = _rows_to_tokens(h2)
    h_hi = h2.astype(BF16)
    h_lo = (h2 - h_hi.astype(F32)).astype(BF16)
    NR = lg_ref.shape[1]
    both = jnp.dot(h_hi, wr_ref[...], preferred_element_type=F32)
    lg = both[:, :NR] + jnp.dot(h_lo, wr_ref[:, :NR], preferred_element_type=F32)
    lg_ref[...] = lg + both[:, NR:] + br_ref[...]


def _outproj(ret2d, lru2d, x2d, wo_bf16, g, wr_hi_lo, br, w_cast, tm):
    T, D = x2d.shape
    R = ret2d.shape[1]
    L = lru2d.shape[1]
    NR = br.shape[1]
    const = lambda shape: pl.BlockSpec(shape, lambda i: (0, 0))
    wshape, wmap = _cast_plan(w_cast, T // tm)
    wspec = pl.BlockSpec(wshape, wmap)
    return pl.pallas_call(
        _outproj_body,
        grid=(T // tm,),
        in_specs=[
            pl.BlockSpec((tm, R), lambda i: (i, 0)),
            pl.BlockSpec((tm, L), lambda i: (i, 0)),
            pl.BlockSpec((tm, D), lambda i: (i, 0)),
            const((R + L, D)), const((1, D)), const((D, 2 * NR)), const((1, NR)),
            wspec,
        ],
        out_specs=[
            pl.BlockSpec((tm, D), lambda i: (i, 0)),
            pl.BlockSpec((tm, D // LANES, LANES), lambda i: (i, 0, 0)),
            pl.BlockSpec((tm, NR), lambda i: (i, 0)),
            wspec,
        ],
        out_shape=[
            jax.ShapeDtypeStruct((T, D), F32),
            jax.ShapeDtypeStruct((T, D // LANES, LANES), F32),
            jax.ShapeDtypeStruct((T, NR), F32),
            jax.ShapeDtypeStruct(w_cast.shape, BF16),
        ],
        compiler_params=_params(("arbitrary",)),
        name="outproj",
    )(ret2d, lru2d, x2d, wo_bf16, g, wr_hi_lo, br, w_cast)


def _route_body(lg_ref, route_ref, counts_ref, run_scr):
    G, EG = N_GROUPS, EXPERTS_PER_GROUP
    sub = ROUTE_SUB
    neg = -jnp.inf

    @pl.when(pl.program_id(0) == 0)
    def _():
        run_scr[...] = jnp.zeros_like(run_scr)

    lane = lax.broadcasted_iota(I32, (sub, LANES), 1)
    r_i = lax.broadcasted_iota(I32, (sub, sub), 0)
    c_i = lax.broadcasted_iota(I32, (sub, sub), 1)
    before = (c_i < r_i).astype(BF16)
    run = run_scr[...]
    for s in range(lg_ref.shape[0] // sub):
        rows = slice(s * sub, (s + 1) * sub)
        lg = lg_ref[rows, :]
        gl = jnp.where(lane < G, lg, neg)
        g_max, g_idx = _first_lane_of_max(gl, lane)
        g_p = 1.0 / jnp.sum(jnp.exp(gl - g_max), axis=-1, keepdims=True)
        lo = G + EG * g_idx
        el = jnp.where((lane >= lo) & (lane < lo + EG), lg, neg)
        t1, i1 = _first_lane_of_max(el, lane)
        el2 = jnp.where(lane == i1, neg, el)
        t2, i2 = _first_lane_of_max(el2, lane)
        p2 = jnp.exp(t2 - t1)
        w1 = g_p / (1.0 + p2)
        w2 = g_p * p2 / (1.0 + p2)
        e1 = i1 - G
        e2 = i2 - G
        oh1 = lane == e1
        oh2 = lane == e2
        oh = (oh1 | oh2).astype(F32)
        prefix = jnp.dot(before, oh.astype(BF16), preferred_element_type=F32) + run[0:1, :]
        rank1 = jnp.sum(jnp.where(oh1, prefix, 0.0), axis=-1, keepdims=True)
        rank2 = jnp.sum(jnp.where(oh2, prefix, 0.0), axis=-1, keepdims=True)
        run = run + jnp.sum(oh, axis=0, keepdims=True)
        cols = (e1.astype(F32), e2.astype(F32), w1, w2, rank1, rank2)
        route = jnp.zeros((sub, LANES), F32)
        for j, col in enumerate(cols):
            route = jnp.where(lane == j, col, route)
        route_ref[rows, :] = route
    run_scr[...] = run
    counts_ref[...] = run


def _route(logits, tm):
    T, NR = logits.shape
    return pl.pallas_call(
        _route_body,
        grid=(T // tm,),
        in_specs=[pl.BlockSpec((tm, NR), lambda i: (i, 0))],
        out_specs=[pl.BlockSpec((tm, NR), lambda i: (i, 0)),
                   pl.BlockSpec((SUBLANES, NR), lambda i: (0, 0))],
        out_shape=[jax.ShapeDtypeStruct((T, NR), F32), jax.ShapeDtypeStruct((SUBLANES, NR), F32)],
        scratch_shapes=[pltpu.VMEM((SUBLANES, NR), F32)],
        compiler_params=_params(("arbitrary",)),
        name="route",
    )(logits)


def _start_row_gather(src_hbm, idx_ref, dst, sem, lo, hi):
    for r in range(lo, hi):
        pltpu.make_async_copy(src_hbm.at[pl.ds(idx_ref[0, r], 1)], dst.at[pl.ds(r, 1)], sem).start(
            priority=r % 2)


def _zero_after(tile, n_rows, n_cols):
    bits = pltpu.bitcast(tile, jnp.uint32)
    zero = pltpu.bitcast((bits >> 16) >> 16, F32)
    return jnp.tile(zero, (n_rows // SUBLANES, n_cols // LANES))


def _wait_rows(dst, sem):
    pltpu.make_async_copy(dst, dst, sem).wait()


def _moe_body(be_ref, nu_ref, tok0_ref, tok1_ref, tok2_ref, tokn_ref, h2_hbm, wg_ref, wu_ref, wd_ref,
              y_ref, rowbuf, xb, sem):
    del be_ref
    i = pl.program_id(0)
    last = nu_ref[0] - 1
    rows = rowbuf.shape[1]

    @pl.when(i > last)
    def _():
        y_ref[...] = jnp.zeros_like(y_ref)

    @pl.when(i <= last)
    def _():
        @pl.when(i == 0)
        def _():
            for b, tok_ref in enumerate((tok0_ref, tok1_ref, tok2_ref)):
                _start_row_gather(h2_hbm, tok_ref, rowbuf.at[b], sem.at[b], 0, rows)
            _wait_rows(rowbuf.at[0], sem.at[0])
            xb[0] = _tokens_to_rows(rowbuf[0]).astype(BF16)

        s1 = lax.rem(i + 1, MOE_SLOTS)
        s2 = lax.rem(i + 2, MOE_SLOTS)
        s3 = lax.rem(i + 3, MOE_SLOTS)
        _wait_rows(rowbuf.at[s1], sem.at[s1])
        xcur = xb.at[lax.rem(i, 2)]
        xnext = _tokens_to_rows(rowbuf[s1]).astype(BF16)

        DE = wg_ref.shape[1]
        D = wd_ref.shape[1]
        n_up, n_down = DE // MOE_UP_COLS, D // MOE_DOWN_COLS
        per = rows // (n_up + n_down)
        ready, nxt, nsem = rowbuf.at[s1], rowbuf.at[s3], sem.at[s3]
        hparts = []
        for c in range(n_up):
            cs = slice(c * MOE_UP_COLS, (c + 1) * MOE_UP_COLS)
            _start_row_gather(h2_hbm, tokn_ref, nxt, nsem, c * per, (c + 1) * per)
            zero = _zero_after(ready[0, 0:SUBLANES, :], rows, MOE_UP_COLS)
            gate = jnp.dot(xcur[...], wg_ref[:, cs], preferred_element_type=F32)
            up = jnp.dot(xcur[...], wu_ref[:, cs], preferred_element_type=F32) + zero
            hparts.append(((gate / (1.0 + jnp.exp(-gate))) * up).astype(BF16))
        hmid = jnp.concatenate(hparts, axis=-1)
        xb[lax.rem(i + 1, 2)] = xnext
        for c in range(n_down):
            cs = slice(c * MOE_DOWN_COLS, (c + 1) * MOE_DOWN_COLS)
            g = n_up + c
            _start_row_gather(h2_hbm, tokn_ref, nxt, nsem, g * per,
                              rows if c == n_down - 1 else (g + 1) * per)
            zero = _zero_after(ready[0, 0:SUBLANES, :], rows, MOE_DOWN_COLS)
            y_ref[:, cs] = jnp.dot(hmid, wd_ref[:, cs], preferred_element_type=F32) + zero

        @pl.when(i == last)
        def _():
            _wait_rows(rowbuf.at[s2], sem.at[s2])
            _wait_rows(nxt, nsem)


def _moe(block_e, n_used, tok3, h2, wg_bf16, wu_bf16, wd_bf16):
    NB, _, rows = tok3.shape
    T, n_tiles, _ = h2.shape
    E, D, DE = wg_bf16.shape
    tok_spec = lambda blk: pl.BlockSpec((None, 1, rows), lambda i, be, nu: (blk(i, nu[0] - 1), 0, 0),
                                        memory_space=pltpu.SMEM)
    w_spec = lambda shape: pl.BlockSpec(shape, lambda i, be, nu: (be[i], 0, 0))
    grid_spec = pltpu.PrefetchScalarGridSpec(
        num_scalar_prefetch=2,
        grid=(NB,),
        in_specs=[
            tok_spec(lambda i, last: 0),
            tok_spec(lambda i, last: jnp.minimum(1, last)),
            tok_spec(lambda i, last: jnp.minimum(2, last)),
            tok_spec(lambda i, last: jnp.minimum(i + 3, last)),
            pl.BlockSpec(memory_space=pl.ANY),
            w_spec((None, D, DE)), w_spec((None, D, DE)), w_spec((None, DE, D)),
        ],
        out_specs=pl.BlockSpec((rows, D), lambda i, be, nu: (i, 0)),
        scratch_shapes=[pltpu.VMEM((MOE_SLOTS, rows, n_tiles, LANES), F32),
                        pltpu.VMEM((2, rows, D), BF16),
                        pltpu.SemaphoreType.DMA((MOE_SLOTS,))],
    )
    return pl.pallas_call(
        _moe_body,
        grid_spec=grid_spec,
        out_shape=jax.ShapeDtypeStruct((NB * rows, D), F32),
        compiler_params=_params(("arbitrary",)),
        name="moe",
    )(block_e, n_used, tok3, tok3, tok3, tok3, h2, wg_bf16, wu_bf16, wd_bf16)


def _combine_body(d0_ref, dn_ref, y_hbm, x1_ref, w_ref, g_ref, o_ref, ybuf, sem):
    i = pl.program_id(0)
    n = pl.num_programs(0)
    slot = lax.rem(i, 2)
    tm = x1_ref.shape[0]

    @pl.when(i == 0)
    def _():
        _start_row_gather(y_hbm, d0_ref, ybuf.at[0], sem.at[0], 0, 2 * tm)

    _start_row_gather(y_hbm, dn_ref, ybuf.at[1 - slot], sem.at[1 - slot], 0, 2 * tm)
    _wait_rows(ybuf.at[slot], sem.at[slot])
    w = w_ref[...]
    y0 = ybuf[slot, :tm, :]
    y1 = ybuf[slot, tm:, :]
    x = x1_ref[...] + (w[:, 0:1] * y0 + w[:, 1:2] * y1)
    ms = jnp.mean(x * x, axis=-1, keepdims=True)
    o_ref[...] = x * lax.rsqrt(ms + EPS) * g_ref[...]

    @pl.when(i == n - 1)
    def _():
        _wait_rows(ybuf.at[1 - slot], sem.at[1 - slot])


def _combine(dest3, y, x1, e_w, g, tm):
    T, D = x1.shape
    NT = T // tm
    return pl.pallas_call(
        _combine_body,
        grid=(NT,),
        in_specs=[
            pl.BlockSpec((None, 1, 2 * tm), lambda i: (0, 0, 0), memory_space=pltpu.SMEM),
            pl.BlockSpec((None, 1, 2 * tm), lambda i: (jnp.minimum(i + 1, NT - 1), 0, 0),
                         memory_space=pltpu.SMEM),
            pl.BlockSpec(memory_space=pl.ANY),
            pl.BlockSpec((tm, D), lambda i: (i, 0)),
            pl.BlockSpec((tm, TOP_K), lambda i: (i, 0)),
            pl.BlockSpec((1, D), lambda i: (0, 0)),
        ],
        out_specs=pl.BlockSpec((tm, D), lambda i: (i, 0)),
        out_shape=jax.ShapeDtypeStruct((T, D), F32),
        scratch_shapes=[pltpu.VMEM((2, 2 * tm, D), F32), pltpu.SemaphoreType.DMA((2,))],
        compiler_params=_params(("arbitrary",)),
        name="combine",
    )(dest3, dest3, y, x1, e_w, g)


def _layout(route, counts, rows):
    T = route.shape[0]
    A = T * TOP_K
    e_id = route[:, 0:TOP_K].astype(I32)
    e_w = route[:, TOP_K:2 * TOP_K]
    rank = route[:, 2 * TOP_K:3 * TOP_K].astype(I32)
    counts = counts[0, :N_EXPERTS].astype(I32)
    padded = (counts + rows - 1) // rows * rows
    pad_end = jnp.cumsum(padded)
    pad_start = pad_end - padded
    experts = jnp.arange(N_EXPERTS, dtype=I32)
    start_of = jnp.sum(jnp.where(e_id[:, :, None] == experts, pad_start, 0), axis=-1)
    dest = start_of + rank
    n_blocks = (A + N_EXPERTS * (rows - 1)) // rows
    P = n_blocks * rows
    tok = jnp.zeros((P,), I32).at[dest.reshape(A)].set(
        jnp.arange(A, dtype=I32) // TOP_K, unique_indices=True, mode="promise_in_bounds")
    n_used = pad_end[-1:] // rows
    block_start = jnp.minimum(jnp.arange(n_blocks, dtype=I32), n_used - 1) * rows
    block_e = jnp.sum((pad_end[None, :] <= block_start[:, None]).astype(I32), axis=1)
    return e_w, dest, tok, block_e, n_used, n_blocks


def kernel(x, norm_mix_g, w_in, ret_norm_g, conv_w, conv_b, w_rg, b_rg, w_ig, b_ig, lru_lambda,
           lru_norm_g, w_out, norm_ffn_g, w_group, b_group, w_router, b_router, w_gate, w_up,
           w_down, norm_final_g):
    B, S, D = x.shape
    T = B * S
    depth = norm_mix_g.shape[0]
    assert depth == 1, "the combine kernel fuses the final norm, so only one layer is supported"
    H, d = RET_HEADS, HEAD_DIM

    half = d // 2
    inv = ROPE_BASE ** (-jnp.arange(half, dtype=F32) / half)
    ang = jnp.arange(S, dtype=F32)[:, None] * inv[None, :]
    cos_t = jnp.concatenate([jnp.cos(ang), jnp.cos(ang)], axis=-1)
    sin_t = jnp.concatenate([-jnp.sin(ang), jnp.sin(ang)], axis=-1)
    log_gamma = jnp.log1p(-(2.0 ** (-5.0 - jnp.arange(H, dtype=F32))))

    x2d = x.reshape(T, D)
    for l in range(depth):
        proj = _inproj(x2d, norm_mix_g[l][None, :], w_in[l].astype(BF16), INPROJ_TM, INPROJ_TN)
        proj3 = proj.reshape(B, S, proj.shape[1])
        ret, wg_bf16 = _retention(proj3, log_gamma, cos_t, sin_t, ret_norm_g[l][None, :], w_gate[l])
        w_gates = jnp.concatenate([w_rg[l], w_ig[l]], axis=-1).astype(BF16)
        lru, wu_bf16 = _rglru(proj3, conv_w[l], conv_b[l][None, :], w_gates, b_rg[l][None, :],
                              b_ig[l][None, :], lru_lambda[l][None, :], lru_norm_g[l][None, :],
                              w_up[l])

        n_route = N_GROUPS + N_EXPERTS
        wr = jnp.concatenate([w_group[l], w_router[l], jnp.zeros((D, LANES - n_route), F32)], axis=-1)
        br = jnp.concatenate([b_group[l], b_router[l], jnp.zeros((LANES - n_route,), F32)])[None, :]
        wr_hi = wr.astype(BF16)
        wr_lo = (wr - wr_hi.astype(F32)).astype(BF16)
        x1, h2, logits, wd_bf16 = _outproj(ret.reshape(T, -1), lru.reshape(T, -1), x2d,
                                           w_out[l].astype(BF16), norm_ffn_g[l][None, :],
                                           jnp.concatenate([wr_hi, wr_lo], axis=1), br, w_down[l],
                                           OUTPROJ_TM)
        route, counts = _route(logits, ROUTE_TM)

        e_w, dest, tok, block_e, n_used, n_blocks = _layout(route, counts, MOE_ROWS)
        y = _moe(block_e, n_used, tok.reshape(n_blocks, 1, MOE_ROWS), h2, wg_bf16, wu_bf16, wd_bf16)
        dest3 = dest.reshape(T // COMBINE_TM, COMBINE_TM, TOP_K).transpose(0, 2, 1).reshape(
            T // COMBINE_TM, 1, TOP_K * COMBINE_TM)
        x2d = _combine(dest3, y, x1, e_w, norm_final_g[None, :], COMBINE_TM)
    return x2d.reshape(B, S, D)
```

```python
import functools
import math

import jax
import jax.numpy as jnp
from jax import lax
from jax.experimental import pallas as pl
from jax.experimental.pallas import tpu as pltpu

F32 = jnp.float32
BF16 = jnp.bfloat16
I32 = jnp.int32

EPS = 1e-6
LOG2_E = 1.4426950408889634
RET_HEADS = 8
HEAD_DIM = 128
RET_CHUNK = 128
ROPE_BASE = 10000.0
LRU_BLOCKS = 8
LRU_BLOCK_DIM = 128
CONV_WIDTH = 4
LRU_C = 8.0
N_GROUPS = 4
EXPERTS_PER_GROUP = 8
N_EXPERTS = N_GROUPS * EXPERTS_PER_GROUP
TOP_K = 2

LANES = 128
SUBLANES = 8
VMEM_LIMIT = 56 * 1024 * 1024

INPROJ_TM = 1024
INPROJ_TN = 1024
INPROJ_CAST_STEPS = 4
LRU_ROWS = 128
OUTPROJ_TM = 256
ROUTE_TM = 2048
ROUTE_SUB = 256
MOE_ROWS = 256
MOE_SLOTS = 4
MOE_UP_COLS = 256
MOE_DOWN_COLS = 512
COMBINE_TM = 256


def _params(sem):
    return pltpu.CompilerParams(dimension_semantics=sem, vmem_limit_bytes=VMEM_LIMIT)


def _inproj_body(x_ref, g_ref, w_ref, wsrc_ref, o_ref, wdst_ref, h_scr):
    wdst_ref[...] = wsrc_ref[...].astype(wdst_ref.dtype)

    @pl.when(pl.program_id(1) == 0)
    def _():
        x = x_ref[...]
        ms = jnp.mean(x * x, axis=-1, keepdims=True)
        h_scr[...] = (x * lax.rsqrt(ms + EPS) * g_ref[...]).astype(BF16)

    o_ref[...] = jnp.dot(h_scr[...], w_ref[...], preferred_element_type=F32).astype(o_ref.dtype)


def _inproj(x2d, g, w_bf16, w_cast, tm, tn):
    T, D = x2d.shape
    N = w_bf16.shape[1]
    n_i, n_j = T // tm, N // tn
    steps_per_i = min(n_j, INPROJ_CAST_STEPS)
    wshape, wmap = _cast_plan(w_cast, n_i * steps_per_i)
    wspec = pl.BlockSpec(wshape, lambda i, j: wmap(i * steps_per_i + jnp.minimum(j, steps_per_i - 1)))
    return pl.pallas_call(
        _inproj_body,
        grid=(n_i, n_j),
        in_specs=[
            pl.BlockSpec((tm, D), lambda i, j: (i, 0)),
            pl.BlockSpec((1, D), lambda i, j: (0, 0)),
            pl.BlockSpec((D, tn), lambda i, j: (0, j)),
            wspec,
        ],
        out_specs=[pl.BlockSpec((tm, tn), lambda i, j: (i, j)), wspec],
        out_shape=[jax.ShapeDtypeStruct((T, N), BF16), jax.ShapeDtypeStruct(w_cast.shape, BF16)],
        scratch_shapes=[pltpu.VMEM((tm, D), BF16)],
        compiler_params=_params(("arbitrary", "arbitrary")),
        name="inproj",
    )(x2d, g, w_bf16, w_cast)


def _cast_plan(w, n_steps):
    E, R, C = w.shape
    if n_steps >= E:
        parts = n_steps // E
        assert n_steps == E * parts and R % parts == 0
        return (None, R // parts, C), (lambda s: (s // parts, s % parts, 0))
    per_step = E // n_steps
    assert E == per_step * n_steps
    return (per_step, R, C), (lambda s: (s, 0, 0))


def _retention_body(lg_ref, q_ref, k_ref, v_ref, g_ref, cos_ref, sin_ref, gn_ref, wsrc_ref, o_ref,
                    wdst_ref, mask_scr, qdec_scr, kdec_scr, qb_scr, kb_scr, qd_scr, kv_scr, sb_scr):
    wdst_ref[...] = wsrc_ref[...].astype(wdst_ref.dtype)
    C = RET_CHUNK
    d = HEAD_DIM
    S = q_ref.shape[0]
    lg = lg_ref[pl.program_id(1)]
    row = lax.broadcasted_iota(I32, (C, d), 0).astype(F32)
    col = lax.broadcasted_iota(I32, (C, d), 1).astype(F32)
    rel = row - col
    scale = d ** -0.5
    mask_scr[...] = jnp.where(rel >= 0, jnp.exp(jnp.maximum(rel, 0.0) * lg), 0.0) * scale
    qdec_scr[...] = jnp.exp((row + 1.0) * lg) * scale
    kdec_scr[...] = jnp.exp((C - 1.0 - row) * lg)
    c_dec = jnp.exp(jnp.full((1, d), float(C), F32) * lg)
    gn = gn_ref[...]
    n_chunks = S // C

    for n in range(n_chunks):
        sl = pl.ds(n * C, C)
        cos = cos_ref[sl, :]
        sin = sin_ref[sl, :]
        q = q_ref[sl, :].astype(F32)
        k = k_ref[sl, :].astype(F32)
        q = q * cos + pltpu.roll(q, d // 2, 1) * sin
        k = k * cos + pltpu.roll(k, d // 2, 1) * sin
        qb_scr[sl, :] = q.astype(BF16)
        kb_scr[sl, :] = k.astype(BF16)
        qd_scr[sl, :] = (q * qdec_scr[...]).astype(BF16)
        kd_t = (k * kdec_scr[...]).T.astype(BF16)
        kv_scr[n] = jnp.dot(kd_t, v_ref[sl, :], preferred_element_type=F32)
    state = jnp.zeros((d, d), F32)
    for n in range(n_chunks):
        sb_scr[n] = state.astype(BF16)
        state = c_dec * state + kv_scr[n]
    for n in range(n_chunks):
        sl = pl.ds(n * C, C)
        v = v_ref[sl, :]
        scores = lax.dot_general(qb_scr[sl, :], kb_scr[sl, :], (((1,), (1,)), ((), ())),
                                 preferred_element_type=F32) * mask_scr[...]
        o = jnp.dot(scores.astype(BF16), v, preferred_element_type=F32)
        o = o + jnp.dot(qd_scr[sl, :], sb_scr[n], preferred_element_type=F32)
        mu = jnp.mean(o, axis=-1, keepdims=True)
        var = jnp.maximum(jnp.mean(o * o, axis=-1, keepdims=True) - mu * mu, 0.0)
        on = (o - mu) * lax.rsqrt(var + EPS) * gn
        g = g_ref[sl, :].astype(F32)
        o_ref[sl, :] = ((g / (1.0 + jnp.exp2(g * (-LOG2_E)))) * on).astype(o_ref.dtype)


def _retention(proj3, log_gamma, cos_t, sin_t, ret_norm_g, w_cast):
    B, S, _ = proj3.shape
    H = RET_HEADS
    d = HEAD_DIM
    blk = lambda off: pl.BlockSpec((None, S, d), lambda b, h, off=off: (b, 0, off + h))
    wshape, wmap = _cast_plan(w_cast, B * H)
    wspec = pl.BlockSpec(wshape, lambda b, h: wmap(b * H + h))
    return pl.pallas_call(
        _retention_body,
        grid=(B, H),
        in_specs=[
            pl.BlockSpec(memory_space=pltpu.SMEM),
            blk(0), blk(H), blk(2 * H), blk(3 * H),
            pl.BlockSpec((S, d), lambda b, h: (0, 0)),
            pl.BlockSpec((S, d), lambda b, h: (0, 0)),
            pl.BlockSpec((1, d), lambda b, h: (0, h)),
            wspec,
        ],
        out_specs=[pl.BlockSpec((None, S, d), lambda b, h: (b, 0, h)), wspec],
        out_shape=[jax.ShapeDtypeStruct((B, S, H * d), BF16),
                   jax.ShapeDtypeStruct(w_cast.shape, BF16)],
        scratch_shapes=[pltpu.VMEM((RET_CHUNK, d), F32)] * 3 + [pltpu.VMEM((S, d), BF16)] * 3 + [
            pltpu.VMEM((S // RET_CHUNK, d, d), F32), pltpu.VMEM((S // RET_CHUNK, d, d), BF16)],
        compiler_params=_params(("arbitrary", "arbitrary")),
        name="retention",
    )(log_gamma, proj3, proj3, proj3, proj3, cos_t, sin_t, ret_norm_g, w_cast)


def _rglru_body(u_ref, z_ref, cw_ref, cb_ref, wg_ref, brg_ref, big_ref, lam_ref, gn_ref, wsrc_ref,
                o_ref, wdst_ref, uf_scr):
    wdst_ref[...] = wsrc_ref[...].astype(wdst_ref.dtype)
    S = u_ref.shape[0]
    W = LRU_BLOCK_DIM
    R = LRU_ROWS
    K = CONV_WIDTH
    nl = -lam_ref[...]
    softplus = jnp.maximum(nl, 0.0) + jnp.log1p(jnp.exp(-jnp.abs(nl)))
    coef = -LRU_C * LOG2_E * softplus
    cw = cw_ref[...]
    cb = cb_ref[...]
    brg = -LOG2_E * brg_ref[...]
    big = -LOG2_E * big_ref[...]
    gn = gn_ref[...]
    wg = wg_ref[...]
    row_in_tile = lax.broadcasted_iota(I32, (R, W), 0) & (SUBLANES - 1)
    uf_scr[:SUBLANES, :] = jnp.zeros((SUBLANES, W), F32)
    carry = jnp.zeros((1, W), F32)

    for c in range(S // R):
        base = SUBLANES + c * R
        u = u_ref[c * R:(c + 1) * R, :].astype(F32)
        uf_scr[base:base + R, :] = u
        uc = cb + cw[K - 1:K, :] * u
        for j in range(1, K):
            uc = uc + cw[K - 1 - j:K - j, :] * uf_scr[base - j:base - j + R, :]
        gates = jnp.dot(uc.astype(BF16), wg, preferred_element_type=F32)
        r = 1.0 / (1.0 + jnp.exp2(gates[:, :W] + brg))
        i = 1.0 / (1.0 + jnp.exp2(gates[:, W:] + big))
        a = jnp.exp2(coef * r)
        t = 1.0 - a * a
        b = jnp.where(t > 0.0, t * lax.rsqrt(t), 0.0) * (i * uc)
        for sh in (1, 2, 4):
            a_s = pltpu.roll(a, sh, 0)
            b_s = pltpu.roll(b, sh, 0)
            valid = row_in_tile >= sh
            b = jnp.where(valid, a * b_s + b, b)
            a = jnp.where(valid, a * a_s, a)
        tiles = []
        for k in range(R // SUBLANES):
            rows = slice(k * SUBLANES, (k + 1) * SUBLANES)
            h_tile = a[rows, :] * carry + b[rows, :]
            carry = h_tile[SUBLANES - 1:SUBLANES, :]
            tiles.append(h_tile)
        h = jnp.concatenate(tiles, axis=0)
        ms = jnp.mean(h * h, axis=-1, keepdims=True)
        hl = h * lax.rsqrt(ms + EPS) * gn
        z = z_ref[c * R:(c + 1) * R, :].astype(F32)
        gelu = 0.5 * z * (1.0 + jnp.tanh(math.sqrt(2.0 / math.pi) * (z + 0.044715 * (z * z * z))))
        o_ref[c * R:(c + 1) * R, :] = (hl * gelu).astype(o_ref.dtype)


def _rglru(proj3, conv_w, conv_b, w_gates, b_rg, b_ig, lam, lru_norm_g, w_cast):
    B, S, _ = proj3.shape
    NB = LRU_BLOCKS
    W = LRU_BLOCK_DIM
    u_off = 4 * RET_HEADS
    z_off = u_off + NB
    vec = pl.BlockSpec((1, W), lambda b, n: (0, n))
    wshape, wmap = _cast_plan(w_cast, B * NB)
    wspec = pl.BlockSpec(wshape, lambda b, n: wmap(b * NB + n))
    return pl.pallas_call(
        _rglru_body,
        grid=(B, NB),
        in_specs=[
            pl.BlockSpec((None, S, W), lambda b, n: (b, 0, u_off + n)),
            pl.BlockSpec((None, S, W), lambda b, n: (b, 0, z_off + n)),
            pl.BlockSpec((CONV_WIDTH, W), lambda b, n: (0, n)),
            vec,
            pl.BlockSpec((None, W, 2 * W), lambda b, n: (n, 0, 0)),
            vec, vec, vec, vec,
            wspec,
        ],
        out_specs=[pl.BlockSpec((None, S, W), lambda b, n: (b, 0, n)), wspec],
        out_shape=[jax.ShapeDtypeStruct((B, S, NB * W), BF16),
                   jax.ShapeDtypeStruct(w_cast.shape, BF16)],
        scratch_shapes=[pltpu.VMEM((SUBLANES + S, W), F32)],
        compiler_params=_params(("arbitrary", "arbitrary")),
        name="rglru",
    )(proj3, proj3, conv_w, conv_b, w_gates, b_rg, b_ig, lam, lru_norm_g, w_cast)


def _rows_to_tokens(x2d):
    n = x2d.shape[1] // LANES
    parts = [x2d[:, s * LANES:(s + 1) * LANES] for s in range(n)]
    return jnp.swapaxes(jnp.stack(parts, axis=0), 0, 1)


def _tokens_to_rows(x3d):
    xt = jnp.swapaxes(x3d, 0, 1)
    return jnp.concatenate([xt[s] for s in range(x3d.shape[1])], axis=-1)


def _first_lane_of_max(v, lane):
    m = jnp.max(v, axis=-1, keepdims=True)
    idx = jnp.min(jnp.where(v == m, lane, LANES), axis=-1, keepdims=True)
    return m, idx


def _outproj_body(ret_ref, lru_ref, x_ref, wo_ref, g_ref, wr_ref, br_ref, x1_ref, h2_ref, lg_ref):
    R = ret_ref.shape[1]
    acc = jnp.dot(ret_ref[...], wo_ref[:R, :], preferred_element_type=F32)
    acc = acc + jnp.dot(lru_ref[...], wo_ref[R:, :], preferred_element_type=F32)
    x1 = x_ref[...] + acc
    x1_ref[...] = x1
    ms = jnp.mean(x1 * x1, axis=-1, keepdims=True)
    h2 = x1 * lax.rsqrt(ms + EPS) * g_ref[...]
    h2_ref[...] = _rows_to_tokens(h2)
    h_hi = h2.astype(BF16)
    h_lo = (h2 - h_hi.astype(F32)).astype(BF16)
    NR = lg_ref.shape[1]
    both = jnp.dot(h_hi, wr_ref[...], preferred_element_type=F32)
    lg = both[:, :NR] + jnp.dot(h_lo, wr_ref[:, :NR], preferred_element_type=F32)
    lg_ref[...] = lg + both[:, NR:] + br_ref[...]


def _outproj(ret2d, lru2d, x2d, wo_bf16, g, wr_hi_lo, br, tm):
    T, D = x2d.shape
    R = ret2d.shape[1]
    L = lru2d.shape[1]
    NR = br.shape[1]
    const = lambda shape: pl.BlockSpec(shape, lambda i: (0, 0))
    return pl.pallas_call(
        _outproj_body,
        grid=(T // tm,),
        in_specs=[
            pl.BlockSpec((tm, R), lambda i: (i, 0)),
            pl.BlockSpec((tm, L), lambda i: (i, 0)),
            pl.BlockSpec((tm, D), lambda i: (i, 0)),
            const((R + L, D)), const((1, D)), const((D, 2 * NR)), const((1, NR)),
        ],
        out_specs=[
            pl.BlockSpec((tm, D), lambda i: (i, 0)),
            pl.BlockSpec((tm, D // LANES, LANES), lambda i: (i, 0, 0)),
            pl.BlockSpec((tm, NR), lambda i: (i, 0)),
        ],
        out_shape=[
            jax.ShapeDtypeStruct((T, D), F32),
            jax.ShapeDtypeStruct((T, D // LANES, LANES), F32),
            jax.ShapeDtypeStruct((T, NR), F32),
        ],
        compiler_params=_params(("arbitrary",)),
        name="outproj",
    )(ret2d, lru2d, x2d, wo_bf16, g, wr_hi_lo, br)


def _route_body(lg_ref, route_ref, counts_ref, run_scr):
    G, EG = N_GROUPS, EXPERTS_PER_GROUP
    sub = ROUTE_SUB
    neg = -jnp.inf

    @pl.when(pl.program_id(0) == 0)
    def _():
        run_scr[...] = jnp.zeros_like(run_scr)

    lane = lax.broadcasted_iota(I32, (sub, LANES), 1)
    r_i = lax.broadcasted_iota(I32, (sub, sub), 0)
    c_i = lax.broadcasted_iota(I32, (sub, sub), 1)
    before = (c_i < r_i).astype(BF16)
    run = run_scr[...]
    for s in range(lg_ref.shape[0] // sub):
        rows = slice(s * sub, (s + 1) * sub)
        lg = lg_ref[rows, :]
        gl = jnp.where(lane < G, lg, neg)
        g_max, g_idx = _first_lane_of_max(gl, lane)
        g_p = 1.0 / jnp.sum(jnp.exp(gl - g_max), axis=-1, keepdims=True)
        lo = G + EG * g_idx
        el = jnp.where((lane >= lo) & (lane < lo + EG), lg, neg)
        t1, i1 = _first_lane_of_max(el, lane)
        el2 = jnp.where(lane == i1, neg, el)
        t2, i2 = _first_lane_of_max(el2, lane)
        p2 = jnp.exp(t2 - t1)
        w1 = g_p / (1.0 + p2)
        w2 = g_p * p2 / (1.0 + p2)
        e1 = i1 - G
        e2 = i2 - G
        oh1 = lane == e1
        oh2 = lane == e2
        oh = (oh1 | oh2).astype(F32)
        prefix = jnp.dot(before, oh.astype(BF16), preferred_element_type=F32) + run[0:1, :]
        rank1 = jnp.sum(jnp.where(oh1, prefix, 0.0), axis=-1, keepdims=True)
        rank2 = jnp.sum(jnp.where(oh2, prefix, 0.0), axis=-1, keepdims=True)
        run = run + jnp.sum(oh, axis=0, keepdims=True)
        cols = (e1.astype(F32), e2.astype(F32), w1, w2, rank1, rank2)
        route = jnp.zeros((sub, LANES), F32)
        for j, col in enumerate(cols):
            route = jnp.where(lane == j, col, route)
        route_ref[rows, :] = route
    run_scr[...] = run
    counts_ref[...] = run


def _route(logits, tm):
    T, NR = logits.shape
    return pl.pallas_call(
        _route_body,
        grid=(T // tm,),
        in_specs=[pl.BlockSpec((tm, NR), lambda i: (i, 0))],
        out_specs=[pl.BlockSpec((tm, NR), lambda i: (i, 0)),
                   pl.BlockSpec((SUBLANES, NR), lambda i: (0, 0))],
        out_shape=[jax.ShapeDtypeStruct((T, NR), F32), jax.ShapeDtypeStruct((SUBLANES, NR), F32)],
        scratch_shapes=[pltpu.VMEM((SUBLANES, NR), F32)],
        compiler_params=_params(("arbitrary",)),
        name="route",
    )(logits)


def _start_row_gather(src_hbm, idx_ref, dst, sem, lo, hi):
    for r in range(lo, hi):
        pltpu.make_async_copy(src_hbm.at[pl.ds(idx_ref[0, r], 1)], dst.at[pl.ds(r, 1)], sem).start(
            priority=r % 2)


def _zero_after(tile, n_rows, n_cols):
    bits = pltpu.bitcast(tile, jnp.uint32)
    zero = pltpu.bitcast((bits >> 16) >> 16, F32)
    return jnp.tile(zero, (n_rows // SUBLANES, n_cols // LANES))


def _wait_rows(dst, sem):
    pltpu.make_async_copy(dst, dst, sem).wait()


def _moe_body(be_ref, nu_ref, tok0_ref, tok1_ref, tok2_ref, tokn_ref, h2_hbm, wg_ref, wu_ref, wd_ref,
              y_ref, rowbuf, xb, sem):
    del be_ref
    i = pl.program_id(0)
    last = nu_ref[0] - 1
    rows = rowbuf.shape[1]

    @pl.when(i > last)
    def _():
        y_ref[...] = jnp.zeros_like(y_ref)

    @pl.when(i <= last)
    def _():
        @pl.when(i == 0)
        def _():
            for b, tok_ref in enumerate((tok0_ref, tok1_ref, tok2_ref)):
                _start_row_gather(h2_hbm, tok_ref, rowbuf.at[b], sem.at[b], 0, rows)
            _wait_rows(rowbuf.at[0], sem.at[0])
            xb[0] = _tokens_to_rows(rowbuf[0]).astype(BF16)

        s1 = lax.rem(i + 1, MOE_SLOTS)
        s2 = lax.rem(i + 2, MOE_SLOTS)
        s3 = lax.rem(i + 3, MOE_SLOTS)
        _wait_rows(rowbuf.at[s1], sem.at[s1])
        xcur = xb.at[lax.rem(i, 2)]
        xnext = _tokens_to_rows(rowbuf[s1]).astype(BF16)

        DE = wg_ref.shape[1]
        D = wd_ref.shape[1]
        n_up, n_down = DE // MOE_UP_COLS, D // MOE_DOWN_COLS
        per = rows // (n_up + n_down)
        ready, nxt, nsem = rowbuf.at[s1], rowbuf.at[s3], sem.at[s3]
        hparts = []
        for c in range(n_up):
            cs = slice(c * MOE_UP_COLS, (c + 1) * MOE_UP_COLS)
            _start_row_gather(h2_hbm, tokn_ref, nxt, nsem, c * per, (c + 1) * per)
            zero = _zero_after(ready[0, 0:SUBLANES, :], rows, MOE_UP_COLS)
            gate = jnp.dot(xcur[...], wg_ref[:, cs], preferred_element_type=F32)
            up = jnp.dot(xcur[...], wu_ref[:, cs], preferred_element_type=F32) + zero
            hparts.append(((gate / (1.0 + jnp.exp(-gate))) * up).astype(BF16))
        hmid = jnp.concatenate(hparts, axis=-1)
        xb[lax.rem(i + 1, 2)] = xnext
        for c in range(n_down):
            cs = slice(c * MOE_DOWN_COLS, (c + 1) * MOE_DOWN_COLS)
            g = n_up + c
            _start_row_gather(h2_hbm, tokn_ref, nxt, nsem, g * per,
                              rows if c == n_down - 1 else (g + 1) * per)
            zero = _zero_after(ready[0, 0:SUBLANES, :], rows, MOE_DOWN_COLS)
            y_ref[:, cs] = jnp.dot(hmid, wd_ref[:, cs], preferred_element_type=F32) + zero

        @pl.when(i == last)
        def _():
            _wait_rows(rowbuf.at[s2], sem.at[s2])
            _wait_rows(nxt, nsem)


def _moe(block_e, n_used, tok3, h2, wg_bf16, wu_bf16, wd_bf16):
    NB, _, rows = tok3.shape
    T, n_tiles, _ = h2.shape
    E, D, DE = wg_bf16.shape
    tok_spec = lambda blk: pl.BlockSpec((None, 1, rows), lambda i, be, nu: (blk(i, nu[0] - 1), 0, 0),
                                        memory_space=pltpu.SMEM)
    w_spec = lambda shape: pl.BlockSpec(shape, lambda i, be, nu: (be[i], 0, 0))
    grid_spec = pltpu.PrefetchScalarGridSpec(
        num_scalar_prefetch=2,
        grid=(NB,),
        in_specs=[
            tok_spec(lambda i, last: 0),
            tok_spec(lambda i, last: jnp.minimum(1, last)),
            tok_spec(lambda i, last: jnp.minimum(2, last)),
            tok_spec(lambda i, last: jnp.minimum(i + 3, last)),
            pl.BlockSpec(memory_space=pl.ANY),
            w_spec((None, D, DE)), w_spec((None, D, DE)), w_spec((None, DE, D)),
        ],
        out_specs=pl.BlockSpec((rows, D), lambda i, be, nu: (i, 0)),
        scratch_shapes=[pltpu.VMEM((MOE_SLOTS, rows, n_tiles, LANES), F32),
                        pltpu.VMEM((2, rows, D), BF16),
                        pltpu.SemaphoreType.DMA((MOE_SLOTS,))],
    )
    return pl.pallas_call(
        _moe_body,
        grid_spec=grid_spec,
        out_shape=jax.ShapeDtypeStruct((NB * rows, D), F32),
        compiler_params=_params(("arbitrary",)),
        name="moe",
    )(block_e, n_used, tok3, tok3, tok3, tok3, h2, wg_bf16, wu_bf16, wd_bf16)


def _combine_body(d0_ref, dn_ref, y_hbm, x1_ref, w_ref, g_ref, o_ref, ybuf, sem):
    i = pl.program_id(0)
    n = pl.num_programs(0)
    slot = lax.rem(i, 2)
    tm = x1_ref.shape[0]

    @pl.when(i == 0)
    def _():
        _start_row_gather(y_hbm, d0_ref, ybuf.at[0], sem.at[0], 0, 2 * tm)

    _start_row_gather(y_hbm, dn_ref, ybuf.at[1 - slot], sem.at[1 - slot], 0, 2 * tm)
    _wait_rows(ybuf.at[slot], sem.at[slot])
    w = w_ref[...]
    y0 = ybuf[slot, :tm, :]
    y1 = ybuf[slot, tm:, :]
    x = x1_ref[...] + (w[:, 0:1] * y0 + w[:, 1:2] * y1)
    ms = jnp.mean(x * x, axis=-1, keepdims=True)
    o_ref[...] = x * lax.rsqrt(ms + EPS) * g_ref[...]

    @pl.when(i == n - 1)
    def _():
        _wait_rows(ybuf.at[1 - slot], sem.at[1 - slot])


def _combine(dest3, y, x1, e_w, g, tm):
    T, D = x1.shape
    NT = T // tm
    return pl.pallas_call(
        _combine_body,
        grid=(NT,),
        in_specs=[
            pl.BlockSpec((None, 1, 2 * tm), lambda i: (0, 0, 0), memory_space=pltpu.SMEM),
            pl.BlockSpec((None, 1, 2 * tm), lambda i: (jnp.minimum(i + 1, NT - 1), 0, 0),
                         memory_space=pltpu.SMEM),
            pl.BlockSpec(memory_space=pl.ANY),
            pl.BlockSpec((tm, D), lambda i: (i, 0)),
            pl.BlockSpec((tm, TOP_K), lambda i: (i, 0)),
            pl.BlockSpec((1, D), lambda i: (0, 0)),
        ],
        out_specs=pl.BlockSpec((tm, D), lambda i: (i, 0)),
        out_shape=jax.ShapeDtypeStruct((T, D), F32),
        scratch_shapes=[pltpu.VMEM((2, 2 * tm, D), F32), pltpu.SemaphoreType.DMA((2,))],
        compiler_params=_params(("arbitrary",)),
        name="combine",
    )(dest3, dest3, y, x1, e_w, g)


def _layout(route, counts, rows):
    T = route.shape[0]
    A = T * TOP_K
    e_id = route[:, 0:TOP_K].astype(I32)
    e_w = route[:, TOP_K:2 * TOP_K]
    rank = route[:, 2 * TOP_K:3 * TOP_K].astype(I32)
    counts = counts[0, :N_EXPERTS].astype(I32)
    padded = (counts + rows - 1) // rows * rows
    pad_end = jnp.cumsum(padded)
    pad_start = pad_end - padded
    experts = jnp.arange(N_EXPERTS, dtype=I32)
    start_of = jnp.sum(jnp.where(e_id[:, :, None] == experts, pad_start, 0), axis=-1)
    dest = start_of + rank
    n_blocks = (A + N_EXPERTS * (rows - 1)) // rows
    P = n_blocks * rows
    tok = jnp.zeros((P,), I32).at[dest.reshape(A)].set(
        jnp.arange(A, dtype=I32) // TOP_K, unique_indices=True, mode="promise_in_bounds")
    n_used = pad_end[-1:] // rows
    block_start = jnp.minimum(jnp.arange(n_blocks, dtype=I32), n_used - 1) * rows
    block_e = jnp.sum((pad_end[None, :] <= block_start[:, None]).astype(I32), axis=1)
    return e_w, dest, tok, block_e, n_used, n_blocks


def kernel(x, norm_mix_g, w_in, ret_norm_g, conv_w, conv_b, w_rg, b_rg, w_ig, b_ig, lru_lambda,
           lru_norm_g, w_out, norm_ffn_g, w_group, b_group, w_router, b_router, w_gate, w_up,
           w_down, norm_final_g):
    B, S, D = x.shape
    T = B * S
    depth = norm_mix_g.shape[0]
    assert depth == 1, "the combine kernel fuses the final norm, so only one layer is supported"
    H, d = RET_HEADS, HEAD_DIM

    half = d // 2
    inv = ROPE_BASE ** (-jnp.arange(half, dtype=F32) / half)
    ang = jnp.arange(S, dtype=F32)[:, None] * inv[None, :]
    cos_t = jnp.concatenate([jnp.cos(ang), jnp.cos(ang)], axis=-1)
    sin_t = jnp.concatenate([-jnp.sin(ang), jnp.sin(ang)], axis=-1)
    log_gamma = jnp.log1p(-(2.0 ** (-5.0 - jnp.arange(H, dtype=F32))))

    x2d = x.reshape(T, D)
    for l in range(depth):
        proj, wd_bf16 = _inproj(x2d, norm_mix_g[l][None, :], w_in[l].astype(BF16), w_down[l],
                                INPROJ_TM, INPROJ_TN)
        proj3 = proj.reshape(B, S, proj.shape[1])
        ret, wg_bf16 = _retention(proj3, log_gamma, cos_t, sin_t, ret_norm_g[l][None, :], w_gate[l])
        w_gates = (-LOG2_E * jnp.concatenate([w_rg[l], w_ig[l]], axis=-1)).astype(BF16)
        lru, wu_bf16 = _rglru(proj3, conv_w[l], conv_b[l][None, :], w_gates, b_rg[l][None, :],
                              b_ig[l][None, :], lru_lambda[l][None, :], lru_norm_g[l][None, :],
                              w_up[l])

        n_route = N_GROUPS + N_EXPERTS
        wr = jnp.concatenate([w_group[l], w_router[l], jnp.zeros((D, LANES - n_route), F32)], axis=-1)
        br = jnp.concatenate([b_group[l], b_router[l], jnp.zeros((LANES - n_route,), F32)])[None, :]
        wr_hi = wr.astype(BF16)
        wr_lo = (wr - wr_hi.astype(F32)).astype(BF16)
        x1, h2, logits = _outproj(ret.reshape(T, -1), lru.reshape(T, -1), x2d, w_out[l].astype(BF16),
                                  norm_ffn_g[l][None, :], jnp.concatenate([wr_hi, wr_lo], axis=1),
                                  br, OUTPROJ_TM)
        route, counts = _route(logits, ROUTE_TM)

        e_w, dest, tok, block_e, n_used, n_blocks = _layout(route, counts, MOE_ROWS)
        y = _moe(block_e, n_used, tok.reshape(n_blocks, 1, MOE_ROWS), h2, wg_bf16, wu_bf16, wd_bf16)
        dest3 = dest.reshape(T // COMBINE_TM, COMBINE_TM, TOP_K).transpose(0, 2, 1).reshape(
            T // COMBINE_TM, 1, TOP_K * COMBINE_TM)
        x2d = _combine(dest3, y, x1, e_w, norm_final_g[None, :], COMBINE_TM)
    return x2d.reshape(B, S, D)
```

```python
import functools
import math

import jax
import jax.numpy as jnp
from jax import lax
from jax.experimental import pallas as pl
from jax.experimental.pallas import tpu as pltpu

F32 = jnp.float32
BF16 = jnp.bfloat16
I32 = jnp.int32

EPS = 1e-6
LOG2_E = 1.4426950408889634
RET_HEADS = 8
HEAD_DIM = 128
RET_CHUNK = 128
ROPE_BASE = 10000.0
LRU_BLOCKS = 8
LRU_BLOCK_DIM = 128
CONV_WIDTH = 4
LRU_C = 8.0
N_GROUPS = 4
EXPERTS_PER_GROUP = 8
N_EXPERTS = N_GROUPS * EXPERTS_PER_GROUP
TOP_K = 2

LANES = 128
SUBLANES = 8
VMEM_LIMIT = 56 * 1024 * 1024

INPROJ_TM = 1024
INPROJ_TN = 1024
INPROJ_CAST_STEPS = 4
LRU_ROWS = 128
OUTPROJ_TM = 256
ROUTE_TM = 2048
ROUTE_SUB = 256
INVERT_BLOCK = 4096
INVERT_UNROLL = 32
MOE_ROWS = 256
MOE_SLOTS = 4
MOE_UP_COLS = 256
MOE_DOWN_COLS = 512
COMBINE_TM = 256


def _params(sem):
    return pltpu.CompilerParams(dimension_semantics=sem, vmem_limit_bytes=VMEM_LIMIT)


def _inproj_body(x_ref, g_ref, w_ref, wsrc_ref, o_ref, wdst_ref, h_scr):
    wdst_ref[...] = wsrc_ref[...].astype(wdst_ref.dtype)

    @pl.when(pl.program_id(1) == 0)
    def _():
        x = x_ref[...]
        ms = jnp.mean(x * x, axis=-1, keepdims=True)
        h_scr[...] = (x * lax.rsqrt(ms + EPS) * g_ref[...]).astype(BF16)

    o_ref[...] = jnp.dot(h_scr[...], w_ref[...], preferred_element_type=F32).astype(o_ref.dtype)


def _inproj(x2d, g, w_bf16, w_cast, tm, tn):
    T, D = x2d.shape
    N = w_bf16.shape[1]
    n_i, n_j = T // tm, N // tn
    steps_per_i = min(n_j, INPROJ_CAST_STEPS)
    wshape, wmap = _cast_plan(w_cast, n_i * steps_per_i)
    wspec = pl.BlockSpec(wshape, lambda i, j: wmap(i * steps_per_i + jnp.minimum(j, steps_per_i - 1)))
    return pl.pallas_call(
        _inproj_body,
        grid=(n_i, n_j),
        in_specs=[
            pl.BlockSpec((tm, D), lambda i, j: (i, 0)),
            pl.BlockSpec((1, D), lambda i, j: (0, 0)),
            pl.BlockSpec((D, tn), lambda i, j: (0, j)),
            wspec,
        ],
        out_specs=[pl.BlockSpec((tm, tn), lambda i, j: (i, j)), wspec],
        out_shape=[jax.ShapeDtypeStruct((T, N), BF16), jax.ShapeDtypeStruct(w_cast.shape, BF16)],
        scratch_shapes=[pltpu.VMEM((tm, D), BF16)],
        compiler_params=_params(("arbitrary", "arbitrary")),
        name="inproj",
    )(x2d, g, w_bf16, w_cast)


def _cast_plan(w, n_steps):
    E, R, C = w.shape
    if n_steps >= E:
        parts = n_steps // E
        assert n_steps == E * parts and R % parts == 0
        return (None, R // parts, C), (lambda s: (s // parts, s % parts, 0))
    per_step = E // n_steps
    assert E == per_step * n_steps
    return (per_step, R, C), (lambda s: (s, 0, 0))


def _retention_body(lg_ref, q_ref, k_ref, v_ref, g_ref, cos_ref, sin_ref, gn_ref, wsrc_ref, o_ref,
                    wdst_ref, mask_scr, qdec_scr, kdec_scr, qb_scr, kb_scr, qd_scr, kv_scr, sb_scr):
    wdst_ref[...] = wsrc_ref[...].astype(wdst_ref.dtype)
    C = RET_CHUNK
    d = HEAD_DIM
    S = q_ref.shape[0]
    lg = lg_ref[pl.program_id(1)]
    row = lax.broadcasted_iota(I32, (C, d), 0).astype(F32)
    col = lax.broadcasted_iota(I32, (C, d), 1).astype(F32)
    rel = row - col
    scale = d ** -0.5
    mask_scr[...] = jnp.where(rel >= 0, jnp.exp(jnp.maximum(rel, 0.0) * lg), 0.0) * scale
    qdec_scr[...] = jnp.exp((row + 1.0) * lg) * scale
    kdec_scr[...] = jnp.exp((C - 1.0 - row) * lg)
    c_dec = jnp.exp(jnp.full((1, d), float(C), F32) * lg)
    gn = gn_ref[...]
    n_chunks = S // C

    for n in range(n_chunks):
        sl = pl.ds(n * C, C)
        cos = cos_ref[sl, :]
        sin = sin_ref[sl, :]
        q = q_ref[sl, :].astype(F32)
        k = k_ref[sl, :].astype(F32)
        q = q * cos + pltpu.roll(q, d // 2, 1) * sin
        k = k * cos + pltpu.roll(k, d // 2, 1) * sin
        qb_scr[sl, :] = q.astype(BF16)
        kb_scr[sl, :] = k.astype(BF16)
        qd_scr[sl, :] = (q * qdec_scr[...]).astype(BF16)
        kd_t = (k * kdec_scr[...]).T.astype(BF16)
        kv_scr[n] = jnp.dot(kd_t, v_ref[sl, :], preferred_element_type=F32)
    state = jnp.zeros((d, d), F32)
    for n in range(n_chunks):
        sb_scr[n] = state.astype(BF16)
        state = c_dec * state + kv_scr[n]
    for n in range(n_chunks):
        sl = pl.ds(n * C, C)
        v = v_ref[sl, :]
        scores = lax.dot_general(qb_scr[sl, :], kb_scr[sl, :], (((1,), (1,)), ((), ())),
                                 preferred_element_type=F32) * mask_scr[...]
        o = jnp.dot(scores.astype(BF16), v, preferred_element_type=F32)
        o = o + jnp.dot(qd_scr[sl, :], sb_scr[n], preferred_element_type=F32)
        mu = jnp.mean(o, axis=-1, keepdims=True)
        var = jnp.maximum(jnp.mean(o * o, axis=-1, keepdims=True) - mu * mu, 0.0)
        on = (o - mu) * lax.rsqrt(var + EPS) * gn
        g = g_ref[sl, :].astype(F32)
        o_ref[sl, :] = ((g / (1.0 + jnp.exp2(g * (-LOG2_E)))) * on).astype(o_ref.dtype)


def _retention(proj3, log_gamma, cos_t, sin_t, ret_norm_g, w_cast):
    B, S, _ = proj3.shape
    H = RET_HEADS
    d = HEAD_DIM
    blk = lambda off: pl.BlockSpec((None, S, d), lambda b, h, off=off: (b, 0, off + h))
    wshape, wmap = _cast_plan(w_cast, B * H)
    wspec = pl.BlockSpec(wshape, lambda b, h: wmap(b * H + h))
    return pl.pallas_call(
        _retention_body,
        grid=(B, H),
        in_specs=[
            pl.BlockSpec(memory_space=pltpu.SMEM),
            blk(0), blk(H), blk(2 * H), blk(3 * H),
            pl.BlockSpec((S, d), lambda b, h: (0, 0)),
            pl.BlockSpec((S, d), lambda b, h: (0, 0)),
            pl.BlockSpec((1, d), lambda b, h: (0, h)),
            wspec,
        ],
        out_specs=[pl.BlockSpec((None, S, d), lambda b, h: (b, 0, h)), wspec],
        out_shape=[jax.ShapeDtypeStruct((B, S, H * d), BF16),
                   jax.ShapeDtypeStruct(w_cast.shape, BF16)],
        scratch_shapes=[pltpu.VMEM((RET_CHUNK, d), F32)] * 3 + [pltpu.VMEM((S, d), BF16)] * 3 + [
            pltpu.VMEM((S // RET_CHUNK, d, d), F32), pltpu.VMEM((S // RET_CHUNK, d, d), BF16)],
        compiler_params=_params(("arbitrary", "arbitrary")),
        name="retention",
    )(log_gamma, proj3, proj3, proj3, proj3, cos_t, sin_t, ret_norm_g, w_cast)


def _rglru_body(u_ref, z_ref, cw_ref, cb_ref, wg_ref, brg_ref, big_ref, lam_ref, gn_ref, wsrc_ref,
                o_ref, wdst_ref, uf_scr):
    wdst_ref[...] = wsrc_ref[...].astype(wdst_ref.dtype)
    S = u_ref.shape[0]
    W = LRU_BLOCK_DIM
    R = LRU_ROWS
    K = CONV_WIDTH
    nl = -lam_ref[...]
    softplus = jnp.maximum(nl, 0.0) + jnp.log1p(jnp.exp(-jnp.abs(nl)))
    coef = -LRU_C * LOG2_E * softplus
    cw = cw_ref[...]
    cb = cb_ref[...]
    brg = -LOG2_E * brg_ref[...]
    big = -LOG2_E * big_ref[...]
    gn = gn_ref[...]
    wg = wg_ref[...]
    row_in_tile = lax.broadcasted_iota(I32, (R, W), 0) & (SUBLANES - 1)
    uf_scr[:SUBLANES, :] = jnp.zeros((SUBLANES, W), F32)
    carry = jnp.zeros((1, W), F32)

    for c in range(S // R):
        base = SUBLANES + c * R
        u = u_ref[c * R:(c + 1) * R, :].astype(F32)
        uf_scr[base:base + R, :] = u
        uc = cb + cw[K - 1:K, :] * u
        for j in range(1, K):
            uc = uc + cw[K - 1 - j:K - j, :] * uf_scr[base - j:base - j + R, :]
        gates = jnp.dot(uc.astype(BF16), wg, preferred_element_type=F32)
        r = 1.0 / (1.0 + jnp.exp2(gates[:, :W] + brg))
        i = 1.0 / (1.0 + jnp.exp2(gates[:, W:] + big))
        a = jnp.exp2(coef * r)
        t = 1.0 - a * a
        b = jnp.where(t > 0.0, t * lax.rsqrt(t), 0.0) * (i * uc)
        for sh in (1, 2, 4):
            a_s = pltpu.roll(a, sh, 0)
            b_s = pltpu.roll(b, sh, 0)
            valid = row_in_tile >= sh
            b = jnp.where(valid, a * b_s + b, b)
            a = jnp.where(valid, a * a_s, a)
        tiles = []
        for k in range(R // SUBLANES):
            rows = slice(k * SUBLANES, (k + 1) * SUBLANES)
            h_tile = a[rows, :] * carry + b[rows, :]
            carry = h_tile[SUBLANES - 1:SUBLANES, :]
            tiles.append(h_tile)
        h = jnp.concatenate(tiles, axis=0)
        ms = jnp.mean(h * h, axis=-1, keepdims=True)
        hl = h * lax.rsqrt(ms + EPS) * gn
        z = z_ref[c * R:(c + 1) * R, :].astype(F32)
        gelu = 0.5 * z * (1.0 + jnp.tanh(math.sqrt(2.0 / math.pi) * (z + 0.044715 * (z * z * z))))
        o_ref[c * R:(c + 1) * R, :] = (hl * gelu).astype(o_ref.dtype)


def _rglru(proj3, conv_w, conv_b, w_gates, b_rg, b_ig, lam, lru_norm_g, w_cast):
    B, S, _ = proj3.shape
    NB = LRU_BLOCKS
    W = LRU_BLOCK_DIM
    u_off = 4 * RET_HEADS
    z_off = u_off + NB
    vec = pl.BlockSpec((1, W), lambda b, n: (0, n))
    wshape, wmap = _cast_plan(w_cast, B * NB)
    wspec = pl.BlockSpec(wshape, lambda b, n: wmap(b * NB + n))
    return pl.pallas_call(
        _rglru_body,
        grid=(B, NB),
        in_specs=[
            pl.BlockSpec((None, S, W), lambda b, n: (b, 0, u_off + n)),
            pl.BlockSpec((None, S, W), lambda b, n: (b, 0, z_off + n)),
            pl.BlockSpec((CONV_WIDTH, W), lambda b, n: (0, n)),
            vec,
            pl.BlockSpec((None, W, 2 * W), lambda b, n: (n, 0, 0)),
            vec, vec, vec, vec,
            wspec,
        ],
        out_specs=[pl.BlockSpec((None, S, W), lambda b, n: (b, 0, n)), wspec],
        out_shape=[jax.ShapeDtypeStruct((B, S, NB * W), BF16),
                   jax.ShapeDtypeStruct(w_cast.shape, BF16)],
        scratch_shapes=[pltpu.VMEM((SUBLANES + S, W), F32)],
        compiler_params=_params(("arbitrary", "arbitrary")),
        name="rglru",
    )(proj3, proj3, conv_w, conv_b, w_gates, b_rg, b_ig, lam, lru_norm_g, w_cast)


def _rows_to_tokens(x2d):
    n = x2d.shape[1] // LANES
    parts = [x2d[:, s * LANES:(s + 1) * LANES] for s in range(n)]
    return jnp.swapaxes(jnp.stack(parts, axis=0), 0, 1)


def _tokens_to_rows(x3d):
    xt = jnp.swapaxes(x3d, 0, 1)
    return jnp.concatenate([xt[s] for s in range(x3d.shape[1])], axis=-1)


def _first_lane_of_max(v, lane):
    m = jnp.max(v, axis=-1, keepdims=True)
    idx = jnp.min(jnp.where(v == m, lane, LANES), axis=-1, keepdims=True)
    return m, idx


def _outproj_body(ret_ref, lru_ref, x_ref, wo_ref, g_ref, wr_ref, br_ref, x1_ref, h2_ref, lg_ref):
    R = ret_ref.shape[1]
    acc = jnp.dot(ret_ref[...], wo_ref[:R, :], preferred_element_type=F32)
    acc = acc + jnp.dot(lru_ref[...], wo_ref[R:, :], preferred_element_type=F32)
    x1 = x_ref[...] + acc
    x1_ref[...] = x1
    ms = jnp.mean(x1 * x1, axis=-1, keepdims=True)
    h2 = x1 * lax.rsqrt(ms + EPS) * g_ref[...]
    h2_ref[...] = _rows_to_tokens(h2)
    h_hi = h2.astype(BF16)
    h_lo = (h2 - h_hi.astype(F32)).astype(BF16)
    NR = lg_ref.shape[1]
    both = jnp.dot(h_hi, wr_ref[...], preferred_element_type=F32)
    lg = both[:, :NR] + jnp.dot(h_lo, wr_ref[:, :NR], preferred_element_type=F32)
    lg_ref[...] = lg + both[:, NR:] + br_ref[...]


def _outproj(ret2d, lru2d, x2d, wo_bf16, g, wr_hi_lo, br, tm):
    T, D = x2d.shape
    R = ret2d.shape[1]
    L = lru2d.shape[1]
    NR = br.shape[1]
    const = lambda shape: pl.BlockSpec(shape, lambda i: (0, 0))
    return pl.pallas_call(
        _outproj_body,
        grid=(T // tm,),
        in_specs=[
            pl.BlockSpec((tm, R), lambda i: (i, 0)),
            pl.BlockSpec((tm, L), lambda i: (i, 0)),
            pl.BlockSpec((tm, D), lambda i: (i, 0)),
            const((R + L, D)), const((1, D)), const((D, 2 * NR)), const((1, NR)),
        ],
        out_specs=[
            pl.BlockSpec((tm, D), lambda i: (i, 0)),
            pl.BlockSpec((tm, D // LANES, LANES), lambda i: (i, 0, 0)),
            pl.BlockSpec((tm, NR), lambda i: (i, 0)),
        ],
        out_shape=[
            jax.ShapeDtypeStruct((T, D), F32),
            jax.ShapeDtypeStruct((T, D // LANES, LANES), F32),
            jax.ShapeDtypeStruct((T, NR), F32),
        ],
        compiler_params=_params(("arbitrary",)),
        name="outproj",
    )(ret2d, lru2d, x2d, wo_bf16, g, wr_hi_lo, br)


def _route_body(lg_ref, route_ref, counts_ref, run_scr):
    G, EG = N_GROUPS, EXPERTS_PER_GROUP
    sub = ROUTE_SUB
    neg = -jnp.inf

    @pl.when(pl.program_id(0) == 0)
    def _():
        run_scr[...] = jnp.zeros_like(run_scr)

    lane = lax.broadcasted_iota(I32, (sub, LANES), 1)
    r_i = lax.broadcasted_iota(I32, (sub, sub), 0)
    c_i = lax.broadcasted_iota(I32, (sub, sub), 1)
    before = (c_i < r_i).astype(BF16)
    run = run_scr[...]
    for s in range(lg_ref.shape[0] // sub):
        rows = slice(s * sub, (s + 1) * sub)
        lg = lg_ref[rows, :]
        gl = jnp.where(lane < G, lg, neg)
        g_max, g_idx = _first_lane_of_max(gl, lane)
        g_p = 1.0 / jnp.sum(jnp.exp(gl - g_max), axis=-1, keepdims=True)
        lo = G + EG * g_idx
        el = jnp.where((lane >= lo) & (lane < lo + EG), lg, neg)
        t1, i1 = _first_lane_of_max(el, lane)
        el2 = jnp.where(lane == i1, neg, el)
        t2, i2 = _first_lane_of_max(el2, lane)
        p2 = jnp.exp(t2 - t1)
        w1 = g_p / (1.0 + p2)
        w2 = g_p * p2 / (1.0 + p2)
        e1 = i1 - G
        e2 = i2 - G
        oh1 = lane == e1
        oh2 = lane == e2
        oh = (oh1 | oh2).astype(F32)
        prefix = jnp.dot(before, oh.astype(BF16), preferred_element_type=F32) + run[0:1, :]
        rank1 = jnp.sum(jnp.where(oh1, prefix, 0.0), axis=-1, keepdims=True)
        rank2 = jnp.sum(jnp.where(oh2, prefix, 0.0), axis=-1, keepdims=True)
        run = run + jnp.sum(oh, axis=0, keepdims=True)
        cols = (e1.astype(F32), e2.astype(F32), w1, w2, rank1, rank2)
        route = jnp.zeros((sub, LANES), F32)
        for j, col in enumerate(cols):
            route = jnp.where(lane == j, col, route)
        route_ref[rows, :] = route
    run_scr[...] = run
    counts_ref[...] = run


def _route(logits, tm):
    T, NR = logits.shape
    return pl.pallas_call(
        _route_body,
        grid=(T // tm,),
        in_specs=[pl.BlockSpec((tm, NR), lambda i: (i, 0))],
        out_specs=[pl.BlockSpec((tm, NR), lambda i: (i, 0)),
                   pl.BlockSpec((SUBLANES, NR), lambda i: (0, 0))],
        out_shape=[jax.ShapeDtypeStruct((T, NR), F32), jax.ShapeDtypeStruct((SUBLANES, NR), F32)],
        scratch_shapes=[pltpu.VMEM((SUBLANES, NR), F32)],
        compiler_params=_params(("arbitrary",)),
        name="route",
    )(logits)


def _start_row_gather(src_hbm, idx_ref, dst, sem, lo, hi):
    for r in range(lo, hi):
        pltpu.make_async_copy(src_hbm.at[pl.ds(idx_ref[0, r], 1)], dst.at[pl.ds(r, 1)], sem).start(
            priority=r % 2)


def _zero_after(tile, n_rows, n_cols):
    bits = pltpu.bitcast(tile, jnp.uint32)
    zero = pltpu.bitcast((bits >> 16) >> 16, F32)
    return jnp.tile(zero, (n_rows // SUBLANES, n_cols // LANES))


def _wait_rows(dst, sem):
    pltpu.make_async_copy(dst, dst, sem).wait()


def _moe_body(be_ref, nu_ref, tok0_ref, tok1_ref, tok2_ref, tokn_ref, h2_hbm, wg_ref, wu_ref, wd_ref,
              y_ref, rowbuf, xb, sem):
    del be_ref
    i = pl.program_id(0)
    last = nu_ref[0] - 1
    rows = rowbuf.shape[1]

    @pl.when(i > last)
    def _():
        y_ref[...] = jnp.zeros_like(y_ref)

    @pl.when(i <= last)
    def _():
        @pl.when(i == 0)
        def _():
            for b, tok_ref in enumerate((tok0_ref, tok1_ref, tok2_ref)):
                _start_row_gather(h2_hbm, tok_ref, rowbuf.at[b], sem.at[b], 0, rows)
            _wait_rows(rowbuf.at[0], sem.at[0])
            xb[0] = _tokens_to_rows(rowbuf[0]).astype(BF16)

        s1 = lax.rem(i + 1, MOE_SLOTS)
        s2 = lax.rem(i + 2, MOE_SLOTS)
        s3 = lax.rem(i + 3, MOE_SLOTS)
        _wait_rows(rowbuf.at[s1], sem.at[s1])
        xcur = xb.at[lax.rem(i, 2)]
        xnext = _tokens_to_rows(rowbuf[s1]).astype(BF16)

        DE = wg_ref.shape[1]
        D = wd_ref.shape[1]
        n_up, n_down = DE // MOE_UP_COLS, D // MOE_DOWN_COLS
        per = rows // (n_up + n_down)
        ready, nxt, nsem = rowbuf.at[s1], rowbuf.at[s3], sem.at[s3]
        hparts = []
        for c in range(n_up):
            cs = slice(c * MOE_UP_COLS, (c + 1) * MOE_UP_COLS)
            _start_row_gather(h2_hbm, tokn_ref, nxt, nsem, c * per, (c + 1) * per)
            zero = _zero_after(ready[0, 0:SUBLANES, :], rows, MOE_UP_COLS)
            gate = jnp.dot(xcur[...], wg_ref[:, cs], preferred_element_type=F32)
            up = jnp.dot(xcur[...], wu_ref[:, cs], preferred_element_type=F32) + zero
            hparts.append(((gate / (1.0 + jnp.exp(-gate))) * up).astype(BF16))
        hmid = jnp.concatenate(hparts, axis=-1)
        xb[lax.rem(i + 1, 2)] = xnext
        for c in range(n_down):
            cs = slice(c * MOE_DOWN_COLS, (c + 1) * MOE_DOWN_COLS)
            g = n_up + c
            _start_row_gather(h2_hbm, tokn_ref, nxt, nsem, g * per,
                              rows if c == n_down - 1 else (g + 1) * per)
            zero = _zero_after(ready[0, 0:SUBLANES, :], rows, MOE_DOWN_COLS)
            y_ref[:, cs] = jnp.dot(hmid, wd_ref[:, cs], preferred_element_type=F32) + zero

        @pl.when(i == last)
        def _():
            _wait_rows(rowbuf.at[s2], sem.at[s2])
            _wait_rows(nxt, nsem)


def _moe(block_e, n_used, tok3, h2, wg_bf16, wu_bf16, wd_bf16):
    NB, _, rows = tok3.shape
    T, n_tiles, _ = h2.shape
    E, D, DE = wg_bf16.shape
    tok_spec = lambda blk: pl.BlockSpec((None, 1, rows), lambda i, be, nu: (blk(i, nu[0] - 1), 0, 0),
                                        memory_space=pltpu.SMEM)
    w_spec = lambda shape: pl.BlockSpec(shape, lambda i, be, nu: (be[i], 0, 0))
    grid_spec = pltpu.PrefetchScalarGridSpec(
        num_scalar_prefetch=2,
        grid=(NB,),
        in_specs=[
            tok_spec(lambda i, last: 0),
            tok_spec(lambda i, last: jnp.minimum(1, last)),
            tok_spec(lambda i, last: jnp.minimum(2, last)),
            tok_spec(lambda i, last: jnp.minimum(i + 3, last)),
            pl.BlockSpec(memory_space=pl.ANY),
            w_spec((None, D, DE)), w_spec((None, D, DE)), w_spec((None, DE, D)),
        ],
        out_specs=pl.BlockSpec((rows, D), lambda i, be, nu: (i, 0)),
        scratch_shapes=[pltpu.VMEM((MOE_SLOTS, rows, n_tiles, LANES), F32),
                        pltpu.VMEM((2, rows, D), BF16),
                        pltpu.SemaphoreType.DMA((MOE_SLOTS,))],
    )
    return pl.pallas_call(
        _moe_body,
        grid_spec=grid_spec,
        out_shape=jax.ShapeDtypeStruct((NB * rows, D), F32),
        compiler_params=_params(("arbitrary",)),
        name="moe",
    )(block_e, n_used, tok3, tok3, tok3, tok3, h2, wg_bf16, wu_bf16, wd_bf16)


def _combine_body(d0_ref, dn_ref, y_hbm, x1_ref, w_ref, g_ref, o_ref, ybuf, sem):
    i = pl.program_id(0)
    n = pl.num_programs(0)
    slot = lax.rem(i, 2)
    tm = x1_ref.shape[0]

    @pl.when(i == 0)
    def _():
        _start_row_gather(y_hbm, d0_ref, ybuf.at[0], sem.at[0], 0, 2 * tm)

    _start_row_gather(y_hbm, dn_ref, ybuf.at[1 - slot], sem.at[1 - slot], 0, 2 * tm)
    _wait_rows(ybuf.at[slot], sem.at[slot])
    w = w_ref[...]
    y0 = ybuf[slot, :tm, :]
    y1 = ybuf[slot, tm:, :]
    x = x1_ref[...] + (w[:, 0:1] * y0 + w[:, 1:2] * y1)
    ms = jnp.mean(x * x, axis=-1, keepdims=True)
    o_ref[...] = x * lax.rsqrt(ms + EPS) * g_ref[...]

    @pl.when(i == n - 1)
    def _():
        _wait_rows(ybuf.at[1 - slot], sem.at[1 - slot])


def _combine(dest3, y, x1, e_w, g, tm):
    T, D = x1.shape
    NT = T // tm
    return pl.pallas_call(
        _combine_body,
        grid=(NT,),
        in_specs=[
            pl.BlockSpec((None, 1, 2 * tm), lambda i: (0, 0, 0), memory_space=pltpu.SMEM),
            pl.BlockSpec((None, 1, 2 * tm), lambda i: (jnp.minimum(i + 1, NT - 1), 0, 0),
                         memory_space=pltpu.SMEM),
            pl.BlockSpec(memory_space=pl.ANY),
            pl.BlockSpec((tm, D), lambda i: (i, 0)),
            pl.BlockSpec((tm, TOP_K), lambda i: (i, 0)),
            pl.BlockSpec((1, D), lambda i: (0, 0)),
        ],
        out_specs=pl.BlockSpec((tm, D), lambda i: (i, 0)),
        out_shape=jax.ShapeDtypeStruct((T, D), F32),
        scratch_shapes=[pltpu.VMEM((2, 2 * tm, D), F32), pltpu.SemaphoreType.DMA((2,))],
        compiler_params=_params(("arbitrary",)),
        name="combine",
    )(dest3, dest3, y, x1, e_w, g)


def _invert_body(lo_ref, hi_ref, dest_ref, tok_ref):
    i = pl.program_id(0)
    blk = dest_ref.shape[1]

    @pl.when(i == 0)
    def _():
        def clear(j, carry):
            tok_ref[j] = 0
            return carry

        for s in range(lo_ref.shape[0]):
            lax.fori_loop(lo_ref[s], hi_ref[s], clear, 0)

    base = i * blk

    def place(j, carry):
        tok_ref[dest_ref[0, j]] = lax.shift_right_logical(base + j, TOP_K.bit_length() - 1)
        return carry

    lax.fori_loop(0, blk, place, 0, unroll=INVERT_UNROLL)


def _invert(hole_lo, hole_hi, dest_flat, n_rows):
    assert TOP_K & (TOP_K - 1) == 0
    A = dest_flat.shape[0]
    blk = min(INVERT_BLOCK, A)
    grid_spec = pltpu.PrefetchScalarGridSpec(
        num_scalar_prefetch=2,
        grid=(A // blk,),
        in_specs=[pl.BlockSpec((None, 1, blk), lambda i, lo, hi: (i, 0, 0), memory_space=pltpu.SMEM)],
        out_specs=pl.BlockSpec(memory_space=pltpu.SMEM),
    )
    return pl.pallas_call(
        _invert_body,
        grid_spec=grid_spec,
        out_shape=jax.ShapeDtypeStruct((n_rows,), I32),
        compiler_params=_params(("arbitrary",)),
        name="invert",
    )(hole_lo, hole_hi, dest_flat.reshape(A // blk, 1, blk))


def _layout(route, counts, rows):
    T = route.shape[0]
    A = T * TOP_K
    e_id = route[:, 0:TOP_K].astype(I32)
    e_w = route[:, TOP_K:2 * TOP_K]
    rank = route[:, 2 * TOP_K:3 * TOP_K].astype(I32)
    counts = counts[0, :N_EXPERTS].astype(I32)
    padded = (counts + rows - 1) // rows * rows
    pad_end = jnp.cumsum(padded)
    pad_start = pad_end - padded
    experts = jnp.arange(N_EXPERTS, dtype=I32)
    start_of = jnp.sum(jnp.where(e_id[:, :, None] == experts, pad_start, 0), axis=-1)
    dest = start_of + rank
    n_blocks = (A + N_EXPERTS * (rows - 1)) // rows
    P = n_blocks * rows
    hole_lo = jnp.concatenate([pad_start + counts, pad_end[-1:]])
    hole_hi = jnp.concatenate([pad_end, jnp.full((1,), P, I32)])
    tok = _invert(hole_lo, hole_hi, dest.reshape(A), P)
    n_used = pad_end[-1:] // rows
    block_start = jnp.minimum(jnp.arange(n_blocks, dtype=I32), n_used - 1) * rows
    block_e = jnp.sum((pad_end[None, :] <= block_start[:, None]).astype(I32), axis=1)
    return e_w, dest, tok, block_e, n_used, n_blocks


def kernel(x, norm_mix_g, w_in, ret_norm_g, conv_w, conv_b, w_rg, b_rg, w_ig, b_ig, lru_lambda,
           lru_norm_g, w_out, norm_ffn_g, w_group, b_group, w_router, b_router, w_gate, w_up,
           w_down, norm_final_g):
    B, S, D = x.shape
    T = B * S
    depth = norm_mix_g.shape[0]
    assert depth == 1, "the combine kernel fuses the final norm, so only one layer is supported"
    H, d = RET_HEADS, HEAD_DIM

    half = d // 2
    inv = ROPE_BASE ** (-jnp.arange(half, dtype=F32) / half)
    ang = jnp.arange(S, dtype=F32)[:, None] * inv[None, :]
    cos_t = jnp.concatenate([jnp.cos(ang), jnp.cos(ang)], axis=-1)
    sin_t = jnp.concatenate([-jnp.sin(ang), jnp.sin(ang)], axis=-1)
    log_gamma = jnp.log1p(-(2.0 ** (-5.0 - jnp.arange(H, dtype=F32))))

    x2d = x.reshape(T, D)
    for l in range(depth):
        proj, wd_bf16 = _inproj(x2d, norm_mix_g[l][None, :], w_in[l].astype(BF16), w_down[l],
                                INPROJ_TM, INPROJ_TN)
        proj3 = proj.reshape(B, S, proj.shape[1])
        ret, wg_bf16 = _retention(proj3, log_gamma, cos_t, sin_t, ret_norm_g[l][None, :], w_gate[l])
        w_gates = (-LOG2_E * jnp.concatenate([w_rg[l], w_ig[l]], axis=-1)).astype(BF16)
        lru, wu_bf16 = _rglru(proj3, conv_w[l], conv_b[l][None, :], w_gates, b_rg[l][None, :],
                              b_ig[l][None, :], lru_lambda[l][None, :], lru_norm_g[l][None, :],
                              w_up[l])

        n_route = N_GROUPS + N_EXPERTS
        wr = jnp.concatenate([w_group[l], w_router[l], jnp.zeros((D, LANES - n_route), F32)], axis=-1)
        br = jnp.concatenate([b_group[l], b_router[l], jnp.zeros((LANES - n_route,), F32)])[None, :]
        wr_hi = wr.astype(BF16)
        wr_lo = (wr - wr_hi.astype(F32)).astype(BF16)
        x1, h2, logits = _outproj(ret.reshape(T, -1), lru.reshape(T, -1), x2d, w_out[l].astype(BF16),
                                  norm_ffn_g[l][None, :], jnp.concatenate([wr_hi, wr_lo], axis=1),
                                  br, OUTPROJ_TM)
        route, counts = _route(logits, ROUTE_TM)

        e_w, dest, tok, block_e, n_used, n_blocks = _layout(route, counts, MOE_ROWS)
        y = _moe(block_e, n_used, tok.reshape(n_blocks, 1, MOE_ROWS), h2, wg_bf16, wu_bf16, wd_bf16)
        dest3 = dest.reshape(T // COMBINE_TM, COMBINE_TM, TOP_K).transpose(0, 2, 1).reshape(
            T // COMBINE_TM, 1, TOP_K * COMBINE_TM)
        x2d = _combine(dest3, y, x1, e_w, norm_final_g[None, :], COMBINE_TM)
    return x2d.reshape(B, S, D)
```

```python
import functools
import math

import jax
import jax.numpy as jnp
from jax import lax
from jax.experimental import pallas as pl
from jax.experimental.pallas import tpu as pltpu

F32 = jnp.float32
BF16 = jnp.bfloat16
I32 = jnp.int32

EPS = 1e-6
LOG2_E = 1.4426950408889634
RET_HEADS = 8
HEAD_DIM = 128
RET_CHUNK = 128
ROPE_BASE = 10000.0
LRU_BLOCKS = 8
LRU_BLOCK_DIM = 128
CONV_WIDTH = 4
LRU_C = 8.0
N_GROUPS = 4
EXPERTS_PER_GROUP = 8
N_EXPERTS = N_GROUPS * EXPERTS_PER_GROUP
TOP_K = 2

LANES = 128
SUBLANES = 8
VMEM_LIMIT = 56 * 1024 * 1024

INPROJ_TM = 1024
INPROJ_TN = 1024
INPROJ_CAST_STEPS = 4
LRU_ROWS = 128
OUTPROJ_TM = 256
ROUTE_TM = 2048
ROUTE_SUB = 256
ROUTE_ROWS = 40
INVERT_BLOCK = 4096
INVERT_UNROLL = 32
MOE_ROWS = 256
MOE_SLOTS = 4
MOE_UP_COLS = 256
MOE_DOWN_COLS = 512
COMBINE_TM = 256


def _params(sem):
    return pltpu.CompilerParams(dimension_semantics=sem, vmem_limit_bytes=VMEM_LIMIT)


def _inproj_body(x_ref, g_ref, w_ref, wsrc_ref, o_ref, wdst_ref, h_scr):
    wdst_ref[...] = wsrc_ref[...].astype(wdst_ref.dtype)

    @pl.when(pl.program_id(1) == 0)
    def _():
        x = x_ref[...]
        ms = jnp.mean(x * x, axis=-1, keepdims=True)
        h_scr[...] = (x * lax.rsqrt(ms + EPS) * g_ref[...]).astype(BF16)

    o_ref[...] = jnp.dot(h_scr[...], w_ref[...], preferred_element_type=F32).astype(o_ref.dtype)


def _inproj(x2d, g, w_bf16, w_cast, tm, tn):
    T, D = x2d.shape
    N = w_bf16.shape[1]
    n_i, n_j = T // tm, N // tn
    steps_per_i = min(n_j, INPROJ_CAST_STEPS)
    wshape, wmap = _cast_plan(w_cast, n_i * steps_per_i)
    wspec = pl.BlockSpec(wshape, lambda i, j: wmap(i * steps_per_i + jnp.minimum(j, steps_per_i - 1)))
    return pl.pallas_call(
        _inproj_body,
        grid=(n_i, n_j),
        in_specs=[
            pl.BlockSpec((tm, D), lambda i, j: (i, 0)),
            pl.BlockSpec((1, D), lambda i, j: (0, 0)),
            pl.BlockSpec((D, tn), lambda i, j: (0, j)),
            wspec,
        ],
        out_specs=[pl.BlockSpec((tm, tn), lambda i, j: (i, j)), wspec],
        out_shape=[jax.ShapeDtypeStruct((T, N), BF16), jax.ShapeDtypeStruct(w_cast.shape, BF16)],
        scratch_shapes=[pltpu.VMEM((tm, D), BF16)],
        compiler_params=_params(("arbitrary", "arbitrary")),
        name="inproj",
    )(x2d, g, w_bf16, w_cast)


def _cast_plan(w, n_steps):
    E, R, C = w.shape
    if n_steps >= E:
        parts = n_steps // E
        assert n_steps == E * parts and R % parts == 0
        return (None, R // parts, C), (lambda s: (s // parts, s % parts, 0))
    per_step = E // n_steps
    assert E == per_step * n_steps
    return (per_step, R, C), (lambda s: (s, 0, 0))


def _retention_body(lg_ref, q_ref, k_ref, v_ref, g_ref, cos_ref, sin_ref, gn_ref, wsrc_ref, o_ref,
                    wdst_ref, mask_scr, qdec_scr, kdec_scr, qb_scr, kb_scr, qd_scr, kv_scr, sb_scr):
    wdst_ref[...] = wsrc_ref[...].astype(wdst_ref.dtype)
    C = RET_CHUNK
    d = HEAD_DIM
    S = q_ref.shape[0]
    lg = lg_ref[pl.program_id(1)]
    row = lax.broadcasted_iota(I32, (C, d), 0).astype(F32)
    col = lax.broadcasted_iota(I32, (C, d), 1).astype(F32)
    rel = row - col
    scale = d ** -0.5
    mask_scr[...] = jnp.where(rel >= 0, jnp.exp(jnp.maximum(rel, 0.0) * lg), 0.0) * scale
    qdec_scr[...] = jnp.exp((row + 1.0) * lg) * scale
    kdec_scr[...] = jnp.exp((C - 1.0 - row) * lg)
    c_dec = jnp.exp(jnp.full((1, d), float(C), F32) * lg)
    gn = gn_ref[...]
    n_chunks = S // C

    for n in range(n_chunks):
        sl = pl.ds(n * C, C)
        cos = cos_ref[sl, :]
        sin = sin_ref[sl, :]
        q = q_ref[sl, :].astype(F32)
        k = k_ref[sl, :].astype(F32)
        q = q * cos + pltpu.roll(q, d // 2, 1) * sin
        k = k * cos + pltpu.roll(k, d // 2, 1) * sin
        qb_scr[sl, :] = q.astype(BF16)
        kb_scr[sl, :] = k.astype(BF16)
        qd_scr[sl, :] = (q * qdec_scr[...]).astype(BF16)
        kd_t = (k * kdec_scr[...]).T.astype(BF16)
        kv_scr[n] = jnp.dot(kd_t, v_ref[sl, :], preferred_element_type=F32)
    state = jnp.zeros((d, d), F32)
    for n in range(n_chunks):
        sb_scr[n] = state.astype(BF16)
        state = c_dec * state + kv_scr[n]
    for n in range(n_chunks):
        sl = pl.ds(n * C, C)
        v = v_ref[sl, :]
        scores = lax.dot_general(qb_scr[sl, :], kb_scr[sl, :], (((1,), (1,)), ((), ())),
                                 preferred_element_type=F32) * mask_scr[...]
        o = jnp.dot(scores.astype(BF16), v, preferred_element_type=F32)
        o = o + jnp.dot(qd_scr[sl, :], sb_scr[n], preferred_element_type=F32)
        mu = jnp.mean(o, axis=-1, keepdims=True)
        var = jnp.maximum(jnp.mean(o * o, axis=-1, keepdims=True) - mu * mu, 0.0)
        on = (o - mu) * lax.rsqrt(var + EPS) * gn
        g = g_ref[sl, :].astype(F32)
        o_ref[sl, :] = ((g / (1.0 + jnp.exp2(g * (-LOG2_E)))) * on).astype(o_ref.dtype)


def _retention(proj3, log_gamma, cos_t, sin_t, ret_norm_g, w_cast):
    B, S, _ = proj3.shape
    H = RET_HEADS
    d = HEAD_DIM
    blk = lambda off: pl.BlockSpec((None, S, d), lambda b, h, off=off: (b, 0, off + h))
    wshape, wmap = _cast_plan(w_cast, B * H)
    wspec = pl.BlockSpec(wshape, lambda b, h: wmap(b * H + h))
    return pl.pallas_call(
        _retention_body,
        grid=(B, H),
        in_specs=[
            pl.BlockSpec(memory_space=pltpu.SMEM),
            blk(0), blk(H), blk(2 * H), blk(3 * H),
            pl.BlockSpec((S, d), lambda b, h: (0, 0)),
            pl.BlockSpec((S, d), lambda b, h: (0, 0)),
            pl.BlockSpec((1, d), lambda b, h: (0, h)),
            wspec,
        ],
        out_specs=[pl.BlockSpec((None, S, d), lambda b, h: (b, 0, h)), wspec],
        out_shape=[jax.ShapeDtypeStruct((B, S, H * d), BF16),
                   jax.ShapeDtypeStruct(w_cast.shape, BF16)],
        scratch_shapes=[pltpu.VMEM((RET_CHUNK, d), F32)] * 3 + [pltpu.VMEM((S, d), BF16)] * 3 + [
            pltpu.VMEM((S // RET_CHUNK, d, d), F32), pltpu.VMEM((S // RET_CHUNK, d, d), BF16)],
        compiler_params=_params(("arbitrary", "arbitrary")),
        name="retention",
    )(log_gamma, proj3, proj3, proj3, proj3, cos_t, sin_t, ret_norm_g, w_cast)


def _rglru_body(u_ref, z_ref, cw_ref, cb_ref, wg_ref, brg_ref, big_ref, lam_ref, gn_ref, wsrc_ref,
                o_ref, wdst_ref, uf_scr):
    wdst_ref[...] = wsrc_ref[...].astype(wdst_ref.dtype)
    S = u_ref.shape[0]
    W = LRU_BLOCK_DIM
    R = LRU_ROWS
    K = CONV_WIDTH
    nl = -lam_ref[...]
    softplus = jnp.maximum(nl, 0.0) + jnp.log1p(jnp.exp(-jnp.abs(nl)))
    coef = -LRU_C * LOG2_E * softplus
    cw = cw_ref[...]
    cb = cb_ref[...]
    brg = -LOG2_E * brg_ref[...]
    big = -LOG2_E * big_ref[...]
    gn = gn_ref[...]
    wg = wg_ref[...]
    row_in_tile = lax.broadcasted_iota(I32, (R, W), 0) & (SUBLANES - 1)
    uf_scr[:SUBLANES, :] = jnp.zeros((SUBLANES, W), F32)
    carry = jnp.zeros((1, W), F32)

    for c in range(S // R):
        base = SUBLANES + c * R
        u = u_ref[c * R:(c + 1) * R, :].astype(F32)
        uf_scr[base:base + R, :] = u
        uc = cb + cw[K - 1:K, :] * u
        for j in range(1, K):
            uc = uc + cw[K - 1 - j:K - j, :] * uf_scr[base - j:base - j + R, :]
        gates = jnp.dot(uc.astype(BF16), wg, preferred_element_type=F32)
        r = 1.0 / (1.0 + jnp.exp2(gates[:, :W] + brg))
        i = 1.0 / (1.0 + jnp.exp2(gates[:, W:] + big))
        a = jnp.exp2(coef * r)
        t = 1.0 - a * a
        b = jnp.where(t > 0.0, t * lax.rsqrt(t), 0.0) * (i * uc)
        for sh in (1, 2, 4):
            a_s = pltpu.roll(a, sh, 0)
            b_s = pltpu.roll(b, sh, 0)
            valid = row_in_tile >= sh
            b = jnp.where(valid, a * b_s + b, b)
            a = jnp.where(valid, a * a_s, a)
        tiles = []
        for k in range(R // SUBLANES):
            rows = slice(k * SUBLANES, (k + 1) * SUBLANES)
            h_tile = a[rows, :] * carry + b[rows, :]
            carry = h_tile[SUBLANES - 1:SUBLANES, :]
            tiles.append(h_tile)
        h = jnp.concatenate(tiles, axis=0)
        ms = jnp.mean(h * h, axis=-1, keepdims=True)
        hl = h * lax.rsqrt(ms + EPS) * gn
        z = z_ref[c * R:(c + 1) * R, :].astype(F32)
        gelu = 0.5 * z * (1.0 + jnp.tanh(math.sqrt(2.0 / math.pi) * (z + 0.044715 * (z * z * z))))
        o_ref[c * R:(c + 1) * R, :] = (hl * gelu).astype(o_ref.dtype)


def _rglru(proj3, conv_w, conv_b, w_gates, b_rg, b_ig, lam, lru_norm_g, w_cast):
    B, S, _ = proj3.shape
    NB = LRU_BLOCKS
    W = LRU_BLOCK_DIM
    u_off = 4 * RET_HEADS
    z_off = u_off + NB
    vec = pl.BlockSpec((1, W), lambda b, n: (0, n))
    wshape, wmap = _cast_plan(w_cast, B * NB)
    wspec = pl.BlockSpec(wshape, lambda b, n: wmap(b * NB + n))
    return pl.pallas_call(
        _rglru_body,
        grid=(B, NB),
        in_specs=[
            pl.BlockSpec((None, S, W), lambda b, n: (b, 0, u_off + n)),
            pl.BlockSpec((None, S, W), lambda b, n: (b, 0, z_off + n)),
            pl.BlockSpec((CONV_WIDTH, W), lambda b, n: (0, n)),
            vec,
            pl.BlockSpec((None, W, 2 * W), lambda b, n: (n, 0, 0)),
            vec, vec, vec, vec,
            wspec,
        ],
        out_specs=[pl.BlockSpec((None, S, W), lambda b, n: (b, 0, n)), wspec],
        out_shape=[jax.ShapeDtypeStruct((B, S, NB * W), BF16),
                   jax.ShapeDtypeStruct(w_cast.shape, BF16)],
        scratch_shapes=[pltpu.VMEM((SUBLANES + S, W), F32)],
        compiler_params=_params(("arbitrary", "arbitrary")),
        name="rglru",
    )(proj3, proj3, conv_w, conv_b, w_gates, b_rg, b_ig, lam, lru_norm_g, w_cast)


def _rows_to_tokens(x2d):
    n = x2d.shape[1] // LANES
    parts = [x2d[:, s * LANES:(s + 1) * LANES] for s in range(n)]
    return jnp.swapaxes(jnp.stack(parts, axis=0), 0, 1)


def _tokens_to_rows(x3d):
    xt = jnp.swapaxes(x3d, 0, 1)
    return jnp.concatenate([xt[s] for s in range(x3d.shape[1])], axis=-1)


def _first_row_of_max(v, row):
    m = jnp.max(v, axis=0, keepdims=True)
    idx = jnp.min(jnp.where(v == m, row, v.shape[0]), axis=0, keepdims=True)
    return m, idx


def _outproj_body(ret_ref, lru_ref, x_ref, wo_ref, g_ref, wr_ref, br_ref, x1_ref, h2_ref, lg_ref):
    R = ret_ref.shape[1]
    acc = jnp.dot(ret_ref[...], wo_ref[:R, :], preferred_element_type=F32)
    acc = acc + jnp.dot(lru_ref[...], wo_ref[R:, :], preferred_element_type=F32)
    x1 = x_ref[...] + acc
    x1_ref[...] = x1
    ms = jnp.mean(x1 * x1, axis=-1, keepdims=True)
    h2 = x1 * lax.rsqrt(ms + EPS) * g_ref[...]
    h2_ref[...] = _rows_to_tokens(h2)
    h_hi = h2.astype(BF16)
    h_lo = (h2 - h_hi.astype(F32)).astype(BF16)
    NR = lg_ref.shape[0]
    both = jnp.dot(h_hi, wr_ref[...], preferred_element_type=F32)
    lg = both[:, :NR] + jnp.dot(h_lo, wr_ref[:, :NR], preferred_element_type=F32)
    lg_ref[...] = (lg + both[:, NR:] + br_ref[...]).T


def _outproj(ret2d, lru2d, x2d, wo_bf16, g, wr_hi_lo, br, tm):
    T, D = x2d.shape
    R = ret2d.shape[1]
    L = lru2d.shape[1]
    NR = br.shape[1]
    const = lambda shape: pl.BlockSpec(shape, lambda i: (0, 0))
    return pl.pallas_call(
        _outproj_body,
        grid=(T // tm,),
        in_specs=[
            pl.BlockSpec((tm, R), lambda i: (i, 0)),
            pl.BlockSpec((tm, L), lambda i: (i, 0)),
            pl.BlockSpec((tm, D), lambda i: (i, 0)),
            const((R + L, D)), const((1, D)), const((D, 2 * NR)), const((1, NR)),
        ],
        out_specs=[
            pl.BlockSpec((tm, D), lambda i: (i, 0)),
            pl.BlockSpec((tm, D // LANES, LANES), lambda i: (i, 0, 0)),
            pl.BlockSpec((NR, tm), lambda i: (0, i)),
        ],
        out_shape=[
            jax.ShapeDtypeStruct((T, D), F32),
            jax.ShapeDtypeStruct((T, D // LANES, LANES), F32),
            jax.ShapeDtypeStruct((NR, T), F32),
        ],
        compiler_params=_params(("arbitrary",)),
        name="outproj",
    )(ret2d, lru2d, x2d, wo_bf16, g, wr_hi_lo, br)


def _route_body(lg_ref, route_ref, counts_ref, run_scr):
    G, EG = N_GROUPS, EXPERTS_PER_GROUP
    R = ROUTE_ROWS
    sub = ROUTE_SUB
    neg = -jnp.inf

    @pl.when(pl.program_id(0) == 0)
    def _():
        run_scr[...] = jnp.zeros_like(run_scr)

    row = lax.broadcasted_iota(I32, (R, sub), 0)
    r_i = lax.broadcasted_iota(I32, (sub, sub), 0)
    c_i = lax.broadcasted_iota(I32, (sub, sub), 1)
    before = (r_i < c_i).astype(BF16)
    ones = jnp.ones((sub, LANES), BF16)
    run = run_scr[...]
    for s in range(lg_ref.shape[1] // sub):
        cols = slice(s * sub, (s + 1) * sub)
        lg = lg_ref[0:R, cols]
        gl = jnp.where(row < G, lg, neg)
        g_max, g_idx = _first_row_of_max(gl, row)
        g_p = 1.0 / jnp.sum(jnp.exp(gl - g_max), axis=0, keepdims=True)
        lo = G + EG * g_idx
        el = jnp.where((row >= lo) & (row < lo + EG), lg, neg)
        t1, i1 = _first_row_of_max(el, row)
        el2 = jnp.where(row == i1, neg, el)
        t2, i2 = _first_row_of_max(el2, row)
        p2 = jnp.exp(t2 - t1)
        w1 = g_p / (1.0 + p2)
        w2 = g_p * p2 / (1.0 + p2)
        oh1 = row == i1
        oh2 = row == i2
        oh = (oh1 | oh2).astype(BF16)
        prefix = jnp.dot(oh, before, preferred_element_type=F32) + jnp.tile(run, (1, sub // LANES))
        rank1 = jnp.sum(jnp.where(oh1, prefix, 0.0), axis=0, keepdims=True)
        rank2 = jnp.sum(jnp.where(oh2, prefix, 0.0), axis=0, keepdims=True)
        run = run + jnp.dot(oh, ones, preferred_element_type=F32)
        zero = jnp.zeros((1, sub), F32)
        route_ref[:, cols] = jnp.concatenate(
            [(i1 - G).astype(F32), (i2 - G).astype(F32), w1, w2, rank1, rank2, zero, zero], axis=0)
    run_scr[...] = run
    counts_ref[...] = run


def _route(logits_t, tm):
    NR, T = logits_t.shape
    R = ROUTE_ROWS
    return pl.pallas_call(
        _route_body,
        grid=(T // tm,),
        in_specs=[pl.BlockSpec((NR, tm), lambda i: (0, i))],
        out_specs=[pl.BlockSpec((SUBLANES, tm), lambda i: (0, i)),
                   pl.BlockSpec((R, LANES), lambda i: (0, 0))],
        out_shape=[jax.ShapeDtypeStruct((SUBLANES, T), F32), jax.ShapeDtypeStruct((R, LANES), F32)],
        scratch_shapes=[pltpu.VMEM((R, LANES), F32)],
        compiler_params=_params(("arbitrary",)),
        name="route",
    )(logits_t)


def _start_row_gather(src_hbm, idx_ref, dst, sem, lo, hi):
    for r in range(lo, hi):
        pltpu.make_async_copy(src_hbm.at[pl.ds(idx_ref[0, r], 1)], dst.at[pl.ds(r, 1)], sem).start(
            priority=r % 2)


def _zero_after(tile, n_rows, n_cols):
    bits = pltpu.bitcast(tile, jnp.uint32)
    zero = pltpu.bitcast((bits >> 16) >> 16, F32)
    return jnp.tile(zero, (n_rows // SUBLANES, n_cols // LANES))


def _wait_rows(dst, sem):
    pltpu.make_async_copy(dst, dst, sem).wait()


def _moe_body(be_ref, nu_ref, tok0_ref, tok1_ref, tok2_ref, tokn_ref, h2_hbm, wg_ref, wu_ref, wd_ref,
              y_ref, rowbuf, xb, sem):
    del be_ref
    i = pl.program_id(0)
    last = nu_ref[0] - 1
    rows = rowbuf.shape[1]

    @pl.when(i > last)
    def _():
        y_ref[...] = jnp.zeros_like(y_ref)

    @pl.when(i <= last)
    def _():
        @pl.when(i == 0)
        def _():
            for b, tok_ref in enumerate((tok0_ref, tok1_ref, tok2_ref)):
                _start_row_gather(h2_hbm, tok_ref, rowbuf.at[b], sem.at[b], 0, rows)
            _wait_rows(rowbuf.at[0], sem.at[0])
            xb[0] = _tokens_to_rows(rowbuf[0]).astype(BF16)

        s1 = lax.rem(i + 1, MOE_SLOTS)
        s2 = lax.rem(i + 2, MOE_SLOTS)
        s3 = lax.rem(i + 3, MOE_SLOTS)
        _wait_rows(rowbuf.at[s1], sem.at[s1])
        xcur = xb.at[lax.rem(i, 2)]
        xnext = _tokens_to_rows(rowbuf[s1]).astype(BF16)

        DE = wg_ref.shape[1]
        D = wd_ref.shape[1]
        n_up, n_down = DE // MOE_UP_COLS, D // MOE_DOWN_COLS
        per = rows // (n_up + n_down)
        ready, nxt, nsem = rowbuf.at[s1], rowbuf.at[s3], sem.at[s3]
        hparts = []
        for c in range(n_up):
            cs = slice(c * MOE_UP_COLS, (c + 1) * MOE_UP_COLS)
            _start_row_gather(h2_hbm, tokn_ref, nxt, nsem, c * per, (c + 1) * per)
            zero = _zero_after(ready[0, 0:SUBLANES, :], rows, MOE_UP_COLS)
            gate = jnp.dot(xcur[...], wg_ref[:, cs], preferred_element_type=F32)
            up = jnp.dot(xcur[...], wu_ref[:, cs], preferred_element_type=F32) + zero
            hparts.append(((gate / (1.0 + jnp.exp(-gate))) * up).astype(BF16))
        hmid = jnp.concatenate(hparts, axis=-1)
        xb[lax.rem(i + 1, 2)] = xnext
        for c in range(n_down):
            cs = slice(c * MOE_DOWN_COLS, (c + 1) * MOE_DOWN_COLS)
            g = n_up + c
            _start_row_gather(h2_hbm, tokn_ref, nxt, nsem, g * per,
                              rows if c == n_down - 1 else (g + 1) * per)
            zero = _zero_after(ready[0, 0:SUBLANES, :], rows, MOE_DOWN_COLS)
            y_ref[:, cs] = jnp.dot(hmid, wd_ref[:, cs], preferred_element_type=F32) + zero

        @pl.when(i == last)
        def _():
            _wait_rows(rowbuf.at[s2], sem.at[s2])
            _wait_rows(nxt, nsem)


def _moe(block_e, n_used, tok3, h2, wg_bf16, wu_bf16, wd_bf16):
    NB, _, rows = tok3.shape
    T, n_tiles, _ = h2.shape
    E, D, DE = wg_bf16.shape
    tok_spec = lambda blk: pl.BlockSpec((None, 1, rows), lambda i, be, nu: (blk(i, nu[0] - 1), 0, 0),
                                        memory_space=pltpu.SMEM)
    w_spec = lambda shape: pl.BlockSpec(shape, lambda i, be, nu: (be[i], 0, 0))
    grid_spec = pltpu.PrefetchScalarGridSpec(
        num_scalar_prefetch=2,
        grid=(NB,),
        in_specs=[
            tok_spec(lambda i, last: 0),
            tok_spec(lambda i, last: jnp.minimum(1, last)),
            tok_spec(lambda i, last: jnp.minimum(2, last)),
            tok_spec(lambda i, last: jnp.minimum(i + 3, last)),
            pl.BlockSpec(memory_space=pl.ANY),
            w_spec((None, D, DE)), w_spec((None, D, DE)), w_spec((None, DE, D)),
        ],
        out_specs=pl.BlockSpec((rows, D), lambda i, be, nu: (i, 0)),
        scratch_shapes=[pltpu.VMEM((MOE_SLOTS, rows, n_tiles, LANES), F32),
                        pltpu.VMEM((2, rows, D), BF16),
                        pltpu.SemaphoreType.DMA((MOE_SLOTS,))],
    )
    return pl.pallas_call(
        _moe_body,
        grid_spec=grid_spec,
        out_shape=jax.ShapeDtypeStruct((NB * rows, D), F32),
        compiler_params=_params(("arbitrary",)),
        name="moe",
    )(block_e, n_used, tok3, tok3, tok3, tok3, h2, wg_bf16, wu_bf16, wd_bf16)


def _combine_body(d0_ref, dn_ref, y_hbm, x1_ref, w_ref, g_ref, o_ref, ybuf, sem):
    i = pl.program_id(0)
    n = pl.num_programs(0)
    slot = lax.rem(i, 2)
    tm = x1_ref.shape[0]

    @pl.when(i == 0)
    def _():
        _start_row_gather(y_hbm, d0_ref, ybuf.at[0], sem.at[0], 0, 2 * tm)

    _start_row_gather(y_hbm, dn_ref, ybuf.at[1 - slot], sem.at[1 - slot], 0, 2 * tm)
    _wait_rows(ybuf.at[slot], sem.at[slot])
    w = w_ref[...]
    y0 = ybuf[slot, :tm, :]
    y1 = ybuf[slot, tm:, :]
    x = x1_ref[...] + (w[:, 0:1] * y0 + w[:, 1:2] * y1)
    ms = jnp.mean(x * x, axis=-1, keepdims=True)
    o_ref[...] = x * lax.rsqrt(ms + EPS) * g_ref[...]

    @pl.when(i == n - 1)
    def _():
        _wait_rows(ybuf.at[1 - slot], sem.at[1 - slot])


def _combine(dest3, y, x1, e_w, g, tm):
    T, D = x1.shape
    NT = T // tm
    return pl.pallas_call(
        _combine_body,
        grid=(NT,),
        in_specs=[
            pl.BlockSpec((None, 1, 2 * tm), lambda i: (0, 0, 0), memory_space=pltpu.SMEM),
            pl.BlockSpec((None, 1, 2 * tm), lambda i: (jnp.minimum(i + 1, NT - 1), 0, 0),
                         memory_space=pltpu.SMEM),
            pl.BlockSpec(memory_space=pl.ANY),
            pl.BlockSpec((tm, D), lambda i: (i, 0)),
            pl.BlockSpec((tm, TOP_K), lambda i: (i, 0)),
            pl.BlockSpec((1, D), lambda i: (0, 0)),
        ],
        out_specs=pl.BlockSpec((tm, D), lambda i: (i, 0)),
        out_shape=jax.ShapeDtypeStruct((T, D), F32),
        scratch_shapes=[pltpu.VMEM((2, 2 * tm, D), F32), pltpu.SemaphoreType.DMA((2,))],
        compiler_params=_params(("arbitrary",)),
        name="combine",
    )(dest3, dest3, y, x1, e_w, g)


def _invert_body(lo_ref, hi_ref, dest_ref, tok_ref):
    i = pl.program_id(0)
    blk = dest_ref.shape[1]

    @pl.when(i == 0)
    def _():
        def clear(j, carry):
            tok_ref[j] = 0
            return carry

        for s in range(lo_ref.shape[0]):
            lax.fori_loop(lo_ref[s], hi_ref[s], clear, 0)

    shift = TOP_K.bit_length() - 1
    n = INVERT_UNROLL

    def place(c, carry):
        j0 = pl.multiple_of(c * n, n)
        tok0 = lax.shift_right_logical(i * blk + j0, shift)
        for u in range(n):
            tok_ref[dest_ref[0, j0 + u]] = tok0 + (u >> shift)
        return carry

    lax.fori_loop(0, blk // n, place, 0)


def _invert(hole_lo, hole_hi, dest_flat, n_rows):
    assert TOP_K & (TOP_K - 1) == 0
    A = dest_flat.shape[0]
    blk = min(INVERT_BLOCK, A)
    grid_spec = pltpu.PrefetchScalarGridSpec(
        num_scalar_prefetch=2,
        grid=(A // blk,),
        in_specs=[pl.BlockSpec((None, 1, blk), lambda i, lo, hi: (i, 0, 0), memory_space=pltpu.SMEM)],
        out_specs=pl.BlockSpec(memory_space=pltpu.SMEM),
    )
    return pl.pallas_call(
        _invert_body,
        grid_spec=grid_spec,
        out_shape=jax.ShapeDtypeStruct((n_rows,), I32),
        compiler_params=_params(("arbitrary",)),
        name="invert",
    )(hole_lo, hole_hi, dest_flat.reshape(A // blk, 1, blk))


def _layout(route, counts, rows):
    T = route.shape[1]
    A = T * TOP_K
    e_id = route[0:TOP_K].T.astype(I32)
    e_w = route[TOP_K:2 * TOP_K].T
    rank = route[2 * TOP_K:3 * TOP_K].T.astype(I32)
    counts = counts[N_GROUPS:N_GROUPS + N_EXPERTS, 0].astype(I32)
    padded = (counts + rows - 1) // rows * rows
    pad_end = jnp.cumsum(padded)
    pad_start = pad_end - padded
    experts = jnp.arange(N_EXPERTS, dtype=I32)
    start_of = jnp.sum(jnp.where(e_id[:, :, None] == experts, pad_start, 0), axis=-1)
    dest = start_of + rank
    n_blocks = (A + N_EXPERTS * (rows - 1)) // rows
    P = n_blocks * rows
    hole_lo = jnp.concatenate([pad_start + counts, pad_end[-1:]])
    hole_hi = jnp.concatenate([pad_end, jnp.full((1,), P, I32)])
    tok = _invert(hole_lo, hole_hi, dest.reshape(A), P)
    n_used = pad_end[-1:] // rows
    block_start = jnp.minimum(jnp.arange(n_blocks, dtype=I32), n_used - 1) * rows
    block_e = jnp.sum((pad_end[None, :] <= block_start[:, None]).astype(I32), axis=1)
    return e_w, dest, tok, block_e, n_used, n_blocks


def kernel(x, norm_mix_g, w_in, ret_norm_g, conv_w, conv_b, w_rg, b_rg, w_ig, b_ig, lru_lambda,
           lru_norm_g, w_out, norm_ffn_g, w_group, b_group, w_router, b_router, w_gate, w_up,
           w_down, norm_final_g):
    B, S, D = x.shape
    T = B * S
    depth = norm_mix_g.shape[0]
    assert depth == 1, "the combine kernel fuses the final norm, so only one layer is supported"
    H, d = RET_HEADS, HEAD_DIM

    half = d // 2
    inv = ROPE_BASE ** (-jnp.arange(half, dtype=F32) / half)
    ang = jnp.arange(S, dtype=F32)[:, None] * inv[None, :]
    cos_t = jnp.concatenate([jnp.cos(ang), jnp.cos(ang)], axis=-1)
    sin_t = jnp.concatenate([-jnp.sin(ang), jnp.sin(ang)], axis=-1)
    log_gamma = jnp.log1p(-(2.0 ** (-5.0 - jnp.arange(H, dtype=F32))))

    x2d = x.reshape(T, D)
    for l in range(depth):
        proj, wd_bf16 = _inproj(x2d, norm_mix_g[l][None, :], w_in[l].astype(BF16), w_down[l],
                                INPROJ_TM, INPROJ_TN)
        proj3 = proj.reshape(B, S, proj.shape[1])
        ret, wg_bf16 = _retention(proj3, log_gamma, cos_t, sin_t, ret_norm_g[l][None, :], w_gate[l])
        w_gates = (-LOG2_E * jnp.concatenate([w_rg[l], w_ig[l]], axis=-1)).astype(BF16)
        lru, wu_bf16 = _rglru(proj3, conv_w[l], conv_b[l][None, :], w_gates, b_rg[l][None, :],
                              b_ig[l][None, :], lru_lambda[l][None, :], lru_norm_g[l][None, :],
                              w_up[l])

        n_route = N_GROUPS + N_EXPERTS
        wr = jnp.concatenate([w_group[l], w_router[l], jnp.zeros((D, LANES - n_route), F32)], axis=-1)
        br = jnp.concatenate([b_group[l], b_router[l], jnp.zeros((LANES - n_route,), F32)])[None, :]
        wr_hi = wr.astype(BF16)
        wr_lo = (wr - wr_hi.astype(F32)).astype(BF16)
        x1, h2, logits_t = _outproj(ret.reshape(T, -1), lru.reshape(T, -1), x2d, w_out[l].astype(BF16),
                                    norm_ffn_g[l][None, :], jnp.concatenate([wr_hi, wr_lo], axis=1),
                                    br, OUTPROJ_TM)
        route, counts = _route(logits_t, ROUTE_TM)

        e_w, dest, tok, block_e, n_used, n_blocks = _layout(route, counts, MOE_ROWS)
        y = _moe(block_e, n_used, tok.reshape(n_blocks, 1, MOE_ROWS), h2, wg_bf16, wu_bf16, wd_bf16)
        dest3 = dest.reshape(T // COMBINE_TM, COMBINE_TM, TOP_K).transpose(0, 2, 1).reshape(
            T // COMBINE_TM, 1, TOP_K * COMBINE_TM)
        x2d = _combine(dest3, y, x1, e_w, norm_final_g[None, :], COMBINE_TM)
    return x2d.reshape(B, S, D)
```

```python
import functools
import math

import jax
import jax.numpy as jnp
from jax import lax
from jax.experimental import pallas as pl
from jax.experimental.pallas import tpu as pltpu

F32 = jnp.float32
BF16 = jnp.bfloat16
I32 = jnp.int32

EPS = 1e-6
LOG2_E = 1.4426950408889634
RET_HEADS = 8
HEAD_DIM = 128
RET_CHUNK = 128
ROPE_BASE = 10000.0
LRU_BLOCKS = 8
LRU_BLOCK_DIM = 128
CONV_WIDTH = 4
LRU_C = 8.0
N_GROUPS = 4
EXPERTS_PER_GROUP = 8
N_EXPERTS = N_GROUPS * EXPERTS_PER_GROUP
TOP_K = 2

LANES = 128
SUBLANES = 8
VMEM_LIMIT = 56 * 1024 * 1024

INPROJ_TM = 1024
INPROJ_TN = 1536
LRU_ROWS = 128
OUTPROJ_TM = 256
ROUTE_TM = 2048
ROUTE_SUB = 256
ROUTE_ROWS = 40
INVERT_BLOCK = 2048
INVERT_UNROLL = 32
MOE_ROWS = 256
MOE_SLOTS = 4
MOE_UP_COLS = 256
MOE_DOWN_COLS = 512
COMBINE_TM = 256
COMBINE_SLOTS = 3
COMBINE_CHUNKS = 8


def _params(sem):
    return pltpu.CompilerParams(dimension_semantics=sem, vmem_limit_bytes=VMEM_LIMIT)


def _inproj_body(x_ref, g_ref, w_ref, o_ref, h_scr):
    @pl.when(pl.program_id(1) == 0)
    def _():
        x = x_ref[...]
        ms = jnp.mean(x * x, axis=-1, keepdims=True)
        h_scr[...] = (x * lax.rsqrt(ms + EPS) * g_ref[...]).astype(BF16)

    o_ref[...] = jnp.dot(h_scr[...], w_ref[...], preferred_element_type=F32).astype(o_ref.dtype)


def _inproj(x2d, g, w_bf16, tm, tn):
    T, D = x2d.shape
    N = w_bf16.shape[1]
    return pl.pallas_call(
        _inproj_body,
        grid=(T // tm, N // tn),
        in_specs=[
            pl.BlockSpec((tm, D), lambda i, j: (i, 0)),
            pl.BlockSpec((1, D), lambda i, j: (0, 0)),
            pl.BlockSpec((D, tn), lambda i, j: (0, j)),
        ],
        out_specs=pl.BlockSpec((tm, tn), lambda i, j: (i, j)),
        out_shape=jax.ShapeDtypeStruct((T, N), BF16),
        scratch_shapes=[pltpu.VMEM((tm, D), BF16)],
        compiler_params=_params(("arbitrary", "arbitrary")),
        name="inproj",
    )(x2d, g, w_bf16)


def _cast_plan(w, n_steps):
    E, R, C = w.shape
    if n_steps >= E:
        parts = n_steps // E
        assert n_steps == E * parts and R % parts == 0
        return (None, R // parts, C), (lambda s: (s // parts, s % parts, 0))
    per_step = E // n_steps
    assert E == per_step * n_steps
    return (per_step, R, C), (lambda s: (s, 0, 0))


def _retention_body(lg_ref, q_ref, k_ref, v_ref, g_ref, cos_ref, sin_ref, gn_ref, wsrc_ref, o_ref,
                    wdst_ref, mask_scr, qdec_scr, kdec_scr, qb_scr, kb_scr, qd_scr, kv_scr, sb_scr):
    wdst_ref[...] = wsrc_ref[...].astype(wdst_ref.dtype)
    C = RET_CHUNK
    d = HEAD_DIM
    S = q_ref.shape[0]
    lg = lg_ref[pl.program_id(1)]
    row = lax.broadcasted_iota(I32, (C, d), 0).astype(F32)
    col = lax.broadcasted_iota(I32, (C, d), 1).astype(F32)
    rel = row - col
    scale = d ** -0.5
    mask_scr[...] = jnp.where(rel >= 0, jnp.exp(jnp.maximum(rel, 0.0) * lg), 0.0) * scale
    qdec_scr[...] = jnp.exp((row + 1.0) * lg) * scale
    kdec_scr[...] = jnp.exp((C - 1.0 - row) * lg)
    c_dec = jnp.exp(jnp.full((1, d), float(C), F32) * lg)
    gn = gn_ref[...]
    n_chunks = S // C

    for n in range(n_chunks):
        sl = pl.ds(n * C, C)
        cos = cos_ref[sl, :]
        sin = sin_ref[sl, :]
        q = q_ref[sl, :].astype(F32)
        k = k_ref[sl, :].astype(F32)
        q = q * cos + pltpu.roll(q, d // 2, 1) * sin
        k = k * cos + pltpu.roll(k, d // 2, 1) * sin
        qb_scr[sl, :] = q.astype(BF16)
        kb_scr[sl, :] = k.astype(BF16)
        qd_scr[sl, :] = (q * qdec_scr[...]).astype(BF16)
        kd_t = (k * kdec_scr[...]).T.astype(BF16)
        kv_scr[n] = jnp.dot(kd_t, v_ref[sl, :], preferred_element_type=F32)
    state = jnp.zeros((d, d), F32)
    for n in range(n_chunks):
        sb_scr[n] = state.astype(BF16)
        state = c_dec * state + kv_scr[n]
    for n in range(n_chunks):
        sl = pl.ds(n * C, C)
        v = v_ref[sl, :]
        scores = lax.dot_general(qb_scr[sl, :], kb_scr[sl, :], (((1,), (1,)), ((), ())),
                                 preferred_element_type=F32) * mask_scr[...]
        o = jnp.dot(scores.astype(BF16), v, preferred_element_type=F32)
        o = o + jnp.dot(qd_scr[sl, :], sb_scr[n], preferred_element_type=F32)
        mu = jnp.mean(o, axis=-1, keepdims=True)
        var = jnp.maximum(jnp.mean(o * o, axis=-1, keepdims=True) - mu * mu, 0.0)
        on = (o - mu) * lax.rsqrt(var + EPS) * gn
        g = g_ref[sl, :].astype(F32)
        o_ref[sl, :] = ((g / (1.0 + jnp.exp2(g * (-LOG2_E)))) * on).astype(o_ref.dtype)


def _retention(proj3, log_gamma, cos_t, sin_t, ret_norm_g, w_cast):
    B, S, _ = proj3.shape
    H = RET_HEADS
    d = HEAD_DIM
    blk = lambda off: pl.BlockSpec((None, S, d), lambda b, h, off=off: (b, 0, off + h))
    wshape, wmap = _cast_plan(w_cast, B * H)
    wspec = pl.BlockSpec(wshape, lambda b, h: wmap(b * H + h))
    return pl.pallas_call(
        _retention_body,
        grid=(B, H),
        in_specs=[
            pl.BlockSpec(memory_space=pltpu.SMEM),
            blk(0), blk(H), blk(2 * H), blk(3 * H),
            pl.BlockSpec((S, d), lambda b, h: (0, 0)),
            pl.BlockSpec((S, d), lambda b, h: (0, 0)),
            pl.BlockSpec((1, d), lambda b, h: (0, h)),
            wspec,
        ],
        out_specs=[pl.BlockSpec((None, S, d), lambda b, h: (b, 0, h)), wspec],
        out_shape=[jax.ShapeDtypeStruct((B, S, H * d), BF16),
                   jax.ShapeDtypeStruct(w_cast.shape, BF16)],
        scratch_shapes=[pltpu.VMEM((RET_CHUNK, d), F32)] * 3 + [pltpu.VMEM((S, d), BF16)] * 3 + [
            pltpu.VMEM((S // RET_CHUNK, d, d), F32), pltpu.VMEM((S // RET_CHUNK, d, d), BF16)],
        compiler_params=_params(("arbitrary", "arbitrary")),
        name="retention",
    )(log_gamma, proj3, proj3, proj3, proj3, cos_t, sin_t, ret_norm_g, w_cast)


def _rglru_body(u_ref, z_ref, cw_ref, cb_ref, wg_ref, brg_ref, big_ref, lam_ref, gn_ref, wsrc_ref,
                o_ref, wdst_ref, uf_scr):
    wdst_ref[...] = wsrc_ref[...].astype(wdst_ref.dtype)
    S = u_ref.shape[0]
    W = LRU_BLOCK_DIM
    R = LRU_ROWS
    K = CONV_WIDTH
    nl = -lam_ref[...]
    softplus = jnp.maximum(nl, 0.0) + jnp.log1p(jnp.exp(-jnp.abs(nl)))
    coef = -LRU_C * LOG2_E * softplus
    cw = cw_ref[...]
    cb = cb_ref[...]
    brg = -LOG2_E * brg_ref[...]
    big = -LOG2_E * big_ref[...]
    gn = gn_ref[...]
    wg = wg_ref[...]
    row_in_tile = lax.broadcasted_iota(I32, (R, W), 0) & (SUBLANES - 1)
    uf_scr[:SUBLANES, :] = jnp.zeros((SUBLANES, W), F32)
    carry = jnp.zeros((1, W), F32)

    for c in range(S // R):
        base = SUBLANES + c * R
        u = u_ref[c * R:(c + 1) * R, :].astype(F32)
        uf_scr[base:base + R, :] = u
        uc = cb + cw[K - 1:K, :] * u
        for j in range(1, K):
            uc = uc + cw[K - 1 - j:K - j, :] * uf_scr[base - j:base - j + R, :]
        gates = jnp.dot(uc.astype(BF16), wg, preferred_element_type=F32)
        r = 1.0 / (1.0 + jnp.exp2(gates[:, :W] + brg))
        i = 1.0 / (1.0 + jnp.exp2(gates[:, W:] + big))
        a = jnp.exp2(coef * r)
        t = 1.0 - a * a
        b = jnp.where(t > 0.0, t * lax.rsqrt(t), 0.0) * (i * uc)
        for sh in (1, 2, 4):
            a_s = pltpu.roll(a, sh, 0)
            b_s = pltpu.roll(b, sh, 0)
            valid = row_in_tile >= sh
            b = jnp.where(valid, a * b_s + b, b)
            a = jnp.where(valid, a * a_s, a)
        tiles = []
        for k in range(R // SUBLANES):
            rows = slice(k * SUBLANES, (k + 1) * SUBLANES)
            h_tile = a[rows, :] * carry + b[rows, :]
            carry = h_tile[SUBLANES - 1:SUBLANES, :]
            tiles.append(h_tile)
        h = jnp.concatenate(tiles, axis=0)
        ms = jnp.mean(h * h, axis=-1, keepdims=True)
        hl = h * lax.rsqrt(ms + EPS) * gn
        z = z_ref[c * R:(c + 1) * R, :].astype(F32)
        gelu = 0.5 * z * (1.0 + jnp.tanh(math.sqrt(2.0 / math.pi) * (z + 0.044715 * (z * z * z))))
        o_ref[c * R:(c + 1) * R, :] = (hl * gelu).astype(o_ref.dtype)


def _rglru(proj3, conv_w, conv_b, w_gates, b_rg, b_ig, lam, lru_norm_g, w_cast):
    B, S, _ = proj3.shape
    NB = LRU_BLOCKS
    W = LRU_BLOCK_DIM
    u_off = 4 * RET_HEADS
    z_off = u_off + NB
    vec = pl.BlockSpec((1, W), lambda b, n: (0, n))
    wshape, wmap = _cast_plan(w_cast, B * NB)
    wspec = pl.BlockSpec(wshape, lambda b, n: wmap(b * NB + n))
    return pl.pallas_call(
        _rglru_body,
        grid=(B, NB),
        in_specs=[
            pl.BlockSpec((None, S, W), lambda b, n: (b, 0, u_off + n)),
            pl.BlockSpec((None, S, W), lambda b, n: (b, 0, z_off + n)),
            pl.BlockSpec((CONV_WIDTH, W), lambda b, n: (0, n)),
            vec,
            pl.BlockSpec((None, W, 2 * W), lambda b, n: (n, 0, 0)),
            vec, vec, vec, vec,
            wspec,
        ],
        out_specs=[pl.BlockSpec((None, S, W), lambda b, n: (b, 0, n)), wspec],
        out_shape=[jax.ShapeDtypeStruct((B, S, NB * W), BF16),
                   jax.ShapeDtypeStruct(w_cast.shape, BF16)],
        scratch_shapes=[pltpu.VMEM((SUBLANES + S, W), F32)],
        compiler_params=_params(("arbitrary", "arbitrary")),
        name="rglru",
    )(proj3, proj3, conv_w, conv_b, w_gates, b_rg, b_ig, lam, lru_norm_g, w_cast)


def _rows_to_tokens(x2d):
    n = x2d.shape[1] // LANES
    parts = [x2d[:, s * LANES:(s + 1) * LANES] for s in range(n)]
    return jnp.swapaxes(jnp.stack(parts, axis=0), 0, 1)


def _tokens_to_rows(x3d):
    xt = jnp.swapaxes(x3d, 0, 1)
    return jnp.concatenate([xt[s] for s in range(x3d.shape[1])], axis=-1)


def _first_row_of_max(v, row):
    m = jnp.max(v, axis=0, keepdims=True)
    idx = jnp.min(jnp.where(v == m, row, v.shape[0]), axis=0, keepdims=True)
    return m, idx


def _outproj_body(ret_ref, lru_ref, x_ref, wo_ref, g_ref, wr_ref, br_ref, x1_ref, h2_ref, lg_ref):
    R = ret_ref.shape[1]
    acc = jnp.dot(ret_ref[...], wo_ref[:R, :], preferred_element_type=F32)
    acc = acc + jnp.dot(lru_ref[...], wo_ref[R:, :], preferred_element_type=F32)
    x1 = x_ref[...] + acc
    x1_ref[...] = x1
    ms = jnp.mean(x1 * x1, axis=-1, keepdims=True)
    h2 = x1 * lax.rsqrt(ms + EPS) * g_ref[...]
    h2_ref[...] = _rows_to_tokens(h2).astype(h2_ref.dtype)
    h_hi = h2.astype(BF16)
    h_lo = (h2 - h_hi.astype(F32)).astype(BF16)
    NR = lg_ref.shape[0]
    both = jnp.dot(h_hi, wr_ref[...], preferred_element_type=F32)
    lg = both[:, :NR] + jnp.dot(h_lo, wr_ref[:, :NR], preferred_element_type=F32)
    lg_ref[...] = (lg + both[:, NR:] + br_ref[...]).T


def _outproj(ret2d, lru2d, x2d, wo_bf16, g, wr_hi_lo, br, tm):
    T, D = x2d.shape
    R = ret2d.shape[1]
    L = lru2d.shape[1]
    NR = br.shape[1]
    const = lambda shape: pl.BlockSpec(shape, lambda i: (0, 0))
    return pl.pallas_call(
        _outproj_body,
        grid=(T // tm,),
        in_specs=[
            pl.BlockSpec((tm, R), lambda i: (i, 0)),
            pl.BlockSpec((tm, L), lambda i: (i, 0)),
            pl.BlockSpec((tm, D), lambda i: (i, 0)),
            const((R + L, D)), const((1, D)), const((D, 2 * NR)), const((1, NR)),
        ],
        out_specs=[
            pl.BlockSpec((tm, D), lambda i: (i, 0)),
            pl.BlockSpec((tm, D // LANES, LANES), lambda i: (i, 0, 0)),
            pl.BlockSpec((NR, tm), lambda i: (0, i)),
        ],
        out_shape=[
            jax.ShapeDtypeStruct((T, D), F32),
            jax.ShapeDtypeStruct((T, D // LANES, LANES), BF16),
            jax.ShapeDtypeStruct((NR, T), F32),
        ],
        compiler_params=_params(("arbitrary",)),
        name="outproj",
    )(ret2d, lru2d, x2d, wo_bf16, g, wr_hi_lo, br)


def _route_body(lg_ref, route_ref, counts_ref, run_scr):
    G, EG = N_GROUPS, EXPERTS_PER_GROUP
    R = ROUTE_ROWS
    sub = ROUTE_SUB
    neg = -jnp.inf

    @pl.when(pl.program_id(0) == 0)
    def _():
        run_scr[...] = jnp.zeros_like(run_scr)

    row = lax.broadcasted_iota(I32, (R, sub), 0)
    r_i = lax.broadcasted_iota(I32, (sub, sub), 0)
    c_i = lax.broadcasted_iota(I32, (sub, sub), 1)
    before = (r_i < c_i).astype(BF16)
    ones = jnp.ones((sub, LANES), BF16)
    run = run_scr[...]
    for s in range(lg_ref.shape[1] // sub):
        cols = slice(s * sub, (s + 1) * sub)
        lg = lg_ref[0:R, cols]
        gl = jnp.where(row < G, lg, neg)
        g_max, g_idx = _first_row_of_max(gl, row)
        g_p = 1.0 / jnp.sum(jnp.exp(gl - g_max), axis=0, keepdims=True)
        lo = G + EG * g_idx
        el = jnp.where((row >= lo) & (row < lo + EG), lg, neg)
        t1, i1 = _first_row_of_max(el, row)
        el2 = jnp.where(row == i1, neg, el)
        t2, i2 = _first_row_of_max(el2, row)
        p2 = jnp.exp(t2 - t1)
        w1 = g_p / (1.0 + p2)
        w2 = g_p * p2 / (1.0 + p2)
        oh1 = row == i1
        oh2 = row == i2
        oh = (oh1 | oh2).astype(BF16)
        prefix = jnp.dot(oh, before, preferred_element_type=F32) + jnp.tile(run, (1, sub // LANES))
        rank1 = jnp.sum(jnp.where(oh1, prefix, 0.0), axis=0, keepdims=True)
        rank2 = jnp.sum(jnp.where(oh2, prefix, 0.0), axis=0, keepdims=True)
        run = run + jnp.dot(oh, ones, preferred_element_type=F32)
        zero = jnp.zeros((1, sub), F32)
        route_ref[:, cols] = jnp.concatenate(
            [(i1 - G).astype(F32), (i2 - G).astype(F32), w1, w2, rank1, rank2, zero, zero], axis=0)
    run_scr[...] = run
    counts_ref[...] = run


def _route(logits_t, tm):
    NR, T = logits_t.shape
    R = ROUTE_ROWS
    return pl.pallas_call(
        _route_body,
        grid=(T // tm,),
        in_specs=[pl.BlockSpec((NR, tm), lambda i: (0, i))],
        out_specs=[pl.BlockSpec((SUBLANES, tm), lambda i: (0, i)),
                   pl.BlockSpec((R, LANES), lambda i: (0, 0))],
        out_shape=[jax.ShapeDtypeStruct((SUBLANES, T), F32), jax.ShapeDtypeStruct((R, LANES), F32)],
        scratch_shapes=[pltpu.VMEM((R, LANES), F32)],
        compiler_params=_params(("arbitrary",)),
        name="route",
    )(logits_t)


def _start_row_gather(src_hbm, idx_ref, dst, sem, lo, hi):
    for r in range(lo, hi):
        pltpu.make_async_copy(src_hbm.at[pl.ds(idx_ref[0, r], 1)], dst.at[pl.ds(r, 1)], sem).start(
            priority=r % 2)


def _zero_after(tile, n_rows, n_cols):
    bits = pltpu.bitcast(tile, jnp.uint32)
    zero = pltpu.bitcast((bits >> 16) >> 16, F32)
    return jnp.tile(zero, (n_rows // SUBLANES, n_cols // LANES))


def _pack_bf16_pair(lo, hi):
    lo_bits = pltpu.bitcast(lo.astype(BF16).astype(F32), jnp.uint32) >> 16
    hi_bits = pltpu.bitcast(hi.astype(BF16).astype(F32), jnp.uint32) & jnp.uint32(0xFFFF0000)
    return lo_bits | hi_bits


def _unpack_bf16_pair(words):
    lo = pltpu.bitcast(words << 16, F32)
    hi = pltpu.bitcast(words & jnp.uint32(0xFFFF0000), F32)
    return lo, hi


def _wait_rows(dst, sem):
    pltpu.make_async_copy(dst, dst, sem).wait()


def _moe_body(be_ref, nu_ref, tok0_ref, tok1_ref, tok2_ref, tokn_ref, h2_hbm, wg_ref, wu_ref, wd_ref,
              y_ref, rowbuf, xb, sem):
    del be_ref
    i = pl.program_id(0)
    last = nu_ref[0] - 1
    rows = rowbuf.shape[1]

    @pl.when(i > last)
    def _():
        y_ref[...] = jnp.zeros_like(y_ref)

    @pl.when(i <= last)
    def _():
        @pl.when(i == 0)
        def _():
            for b, tok_ref in enumerate((tok0_ref, tok1_ref, tok2_ref)):
                _start_row_gather(h2_hbm, tok_ref, rowbuf.at[b], sem.at[b], 0, rows)
            _wait_rows(rowbuf.at[0], sem.at[0])
            xb[0] = _tokens_to_rows(rowbuf[0].astype(F32)).astype(BF16)

        s1 = lax.rem(i + 1, MOE_SLOTS)
        s2 = lax.rem(i + 2, MOE_SLOTS)
        s3 = lax.rem(i + 3, MOE_SLOTS)
        _wait_rows(rowbuf.at[s1], sem.at[s1])
        xcur = xb.at[lax.rem(i, 2)]
        xnext = _tokens_to_rows(rowbuf[s1].astype(F32)).astype(BF16)

        DE = wg_ref.shape[1]
        D = wd_ref.shape[1]
        n_up, n_down = DE // MOE_UP_COLS, D // MOE_DOWN_COLS
        per = rows // (n_up + n_down)
        ready, nxt, nsem = rowbuf.at[s1], rowbuf.at[s3], sem.at[s3]
        hparts = []
        for c in range(n_up):
            cs = slice(c * MOE_UP_COLS, (c + 1) * MOE_UP_COLS)
            _start_row_gather(h2_hbm, tokn_ref, nxt, nsem, c * per, (c + 1) * per)
            zero = _zero_after(ready[0], rows, MOE_UP_COLS)
            gate = jnp.dot(xcur[...], wg_ref[:, cs], preferred_element_type=F32)
            up = jnp.dot(xcur[...], wu_ref[:, cs], preferred_element_type=F32) + zero
            hparts.append(((gate / (1.0 + jnp.exp(-gate))) * up).astype(BF16))
        hmid = jnp.concatenate(hparts, axis=-1)
        xb[lax.rem(i + 1, 2)] = xnext
        yparts = []
        for c in range(n_down):
            cs = slice(c * MOE_DOWN_COLS, (c + 1) * MOE_DOWN_COLS)
            g = n_up + c
            _start_row_gather(h2_hbm, tokn_ref, nxt, nsem, g * per,
                              rows if c == n_down - 1 else (g + 1) * per)
            zero = _zero_after(ready[0], rows, MOE_DOWN_COLS)
            yparts.append(jnp.dot(hmid, wd_ref[:, cs], preferred_element_type=F32) + zero)
        for c in range(n_down // 2):
            cs = slice(c * MOE_DOWN_COLS, (c + 1) * MOE_DOWN_COLS)
            y_ref[:, cs] = _pack_bf16_pair(yparts[c], yparts[c + n_down // 2])

        @pl.when(i == last)
        def _():
            _wait_rows(rowbuf.at[s2], sem.at[s2])
            _wait_rows(nxt, nsem)


def _moe(block_e, n_used, tok3, h2, wg_bf16, wu_bf16, wd_bf16):
    NB, _, rows = tok3.shape
    T, n_tiles, _ = h2.shape
    E, D, DE = wg_bf16.shape
    tok_spec = lambda blk: pl.BlockSpec((None, 1, rows), lambda i, be, nu: (blk(i, nu[0] - 1), 0, 0),
                                        memory_space=pltpu.SMEM)
    w_spec = lambda shape: pl.BlockSpec(shape, lambda i, be, nu: (be[i], 0, 0))
    grid_spec = pltpu.PrefetchScalarGridSpec(
        num_scalar_prefetch=2,
        grid=(NB,),
        in_specs=[
            tok_spec(lambda i, last: 0),
            tok_spec(lambda i, last: jnp.minimum(1, last)),
            tok_spec(lambda i, last: jnp.minimum(2, last)),
            tok_spec(lambda i, last: jnp.minimum(i + 3, last)),
            pl.BlockSpec(memory_space=pl.ANY),
            w_spec((None, D, DE)), w_spec((None, D, DE)), w_spec((None, DE, D)),
        ],
        out_specs=pl.BlockSpec((rows, D // 2), lambda i, be, nu: (i, 0)),
        scratch_shapes=[pltpu.VMEM((MOE_SLOTS, rows, n_tiles, LANES), h2.dtype),
                        pltpu.VMEM((2, rows, D), BF16),
                        pltpu.SemaphoreType.DMA((MOE_SLOTS,))],
    )
    return pl.pallas_call(
        _moe_body,
        grid_spec=grid_spec,
        out_shape=jax.ShapeDtypeStruct((NB * rows, D // 2), jnp.uint32),
        compiler_params=_params(("arbitrary",)),
        name="moe",
    )(block_e, n_used, tok3, tok3, tok3, tok3, h2, wg_bf16, wu_bf16, wd_bf16)


def _combine_body(d0_ref, d1_ref, dn_ref, y_hbm, x1_ref, w_ref, g_ref, o_ref, ybuf, sem):
    i = pl.program_id(0)
    n = pl.num_programs(0)
    slot = lax.rem(i, COMBINE_SLOTS)
    slot1 = lax.rem(i + 1, COMBINE_SLOTS)
    slot2 = lax.rem(i + 2, COMBINE_SLOTS)
    tm = x1_ref.shape[0]

    @pl.when(i == 0)
    def _():
        _start_row_gather(y_hbm, d0_ref, ybuf.at[0], sem.at[0], 0, 2 * tm)
        _start_row_gather(y_hbm, d1_ref, ybuf.at[1], sem.at[1], 0, 2 * tm)

    _wait_rows(ybuf.at[slot], sem.at[slot])
    g = g_ref[...]
    yb, nxt, nsem = ybuf.at[slot], ybuf.at[slot2], sem.at[slot2]
    rc = tm // COMBINE_CHUNKS
    per = 2 * tm // COMBINE_CHUNKS
    for c in range(COMBINE_CHUNKS):
        rows = slice(c * rc, (c + 1) * rc)
        w = w_ref[rows, :]
        y0 = jnp.concatenate(_unpack_bf16_pair(yb[c * rc:(c + 1) * rc, :]), axis=-1)
        y1 = jnp.concatenate(_unpack_bf16_pair(yb[tm + c * rc:tm + (c + 1) * rc, :]), axis=-1)
        x = x1_ref[rows, :] + (w[:, 0:1] * y0 + w[:, 1:2] * y1)
        ms = jnp.mean(x * x, axis=-1, keepdims=True)
        o_ref[rows, :] = x * lax.rsqrt(ms + EPS) * g
        _start_row_gather(y_hbm, dn_ref, nxt, nsem, c * per, (c + 1) * per)

    @pl.when(i == n - 1)
    def _():
        _wait_rows(ybuf.at[slot1], sem.at[slot1])
        _wait_rows(nxt, nsem)


def _combine(dest3, y, x1, e_w, g, tm):
    T, D = x1.shape
    NT = T // tm
    return pl.pallas_call(
        _combine_body,
        grid=(NT,),
        in_specs=[
            pl.BlockSpec((None, 1, 2 * tm), lambda i: (0, 0, 0), memory_space=pltpu.SMEM),
            pl.BlockSpec((None, 1, 2 * tm), lambda i: (min(1, NT - 1), 0, 0), memory_space=pltpu.SMEM),
            pl.BlockSpec((None, 1, 2 * tm), lambda i: (jnp.minimum(i + 2, NT - 1), 0, 0),
                         memory_space=pltpu.SMEM),
            pl.BlockSpec(memory_space=pl.ANY),
            pl.BlockSpec((tm, D), lambda i: (i, 0)),
            pl.BlockSpec((tm, TOP_K), lambda i: (i, 0)),
            pl.BlockSpec((1, D), lambda i: (0, 0)),
        ],
        out_specs=pl.BlockSpec((tm, D), lambda i: (i, 0)),
        out_shape=jax.ShapeDtypeStruct((T, D), F32),
        scratch_shapes=[pltpu.VMEM((COMBINE_SLOTS, 2 * tm, y.shape[1]), y.dtype),
                        pltpu.SemaphoreType.DMA((COMBINE_SLOTS,))],
        compiler_params=_params(("arbitrary",)),
        name="combine",
    )(dest3, dest3, dest3, y, x1, e_w, g)


def _invert_body(lo_ref, hi_ref, dest_ref, wsrc_ref, tok_ref, wdst_ref):
    wdst_ref[...] = wsrc_ref[...].astype(wdst_ref.dtype)
    i = pl.program_id(0)
    blk = dest_ref.shape[1]

    @pl.when(i == 0)
    def _():
        def clear(j, carry):
            tok_ref[j] = 0
            return carry

        for s in range(lo_ref.shape[0]):
            lax.fori_loop(lo_ref[s], hi_ref[s], clear, 0)

    shift = TOP_K.bit_length() - 1
    n = INVERT_UNROLL

    def place(c, carry):
        j0 = pl.multiple_of(c * n, n)
        tok0 = lax.shift_right_logical(i * blk + j0, shift)
        for u in range(n):
            tok_ref[dest_ref[0, j0 + u]] = tok0 + (u >> shift)
        return carry

    lax.fori_loop(0, blk // n, place, 0)


def _invert(hole_lo, hole_hi, dest_flat, n_rows, w_cast):
    assert TOP_K & (TOP_K - 1) == 0
    A = dest_flat.shape[0]
    blk = min(INVERT_BLOCK, A)
    wshape, wmap = _cast_plan(w_cast, A // blk)
    wspec = pl.BlockSpec(wshape, lambda i, lo, hi: wmap(i))
    grid_spec = pltpu.PrefetchScalarGridSpec(
        num_scalar_prefetch=2,
        grid=(A // blk,),
        in_specs=[pl.BlockSpec((None, 1, blk), lambda i, lo, hi: (i, 0, 0), memory_space=pltpu.SMEM),
                  wspec],
        out_specs=[pl.BlockSpec(memory_space=pltpu.SMEM), wspec],
    )
    return pl.pallas_call(
        _invert_body,
        grid_spec=grid_spec,
        out_shape=[jax.ShapeDtypeStruct((n_rows,), I32), jax.ShapeDtypeStruct(w_cast.shape, BF16)],
        compiler_params=_params(("arbitrary",)),
        name="invert",
    )(hole_lo, hole_hi, dest_flat.reshape(A // blk, 1, blk), w_cast)


def _layout(route, counts, rows, w_cast):
    T = route.shape[1]
    A = T * TOP_K
    e_id = route[0:TOP_K].T.astype(I32)
    e_w = route[TOP_K:2 * TOP_K].T
    rank = route[2 * TOP_K:3 * TOP_K].T.astype(I32)
    counts = counts[N_GROUPS:N_GROUPS + N_EXPERTS, 0].astype(I32)
    padded = (counts + rows - 1) // rows * rows
    pad_end = jnp.cumsum(padded)
    pad_start = pad_end - padded
    experts = jnp.arange(N_EXPERTS, dtype=I32)
    start_of = jnp.sum(jnp.where(e_id[:, :, None] == experts, pad_start, 0), axis=-1)
    dest = start_of + rank
    n_blocks = (A + N_EXPERTS * (rows - 1)) // rows
    P = n_blocks * rows
    hole_lo = jnp.concatenate([pad_start + counts, pad_end[-1:]])
    hole_hi = jnp.concatenate([pad_end, jnp.full((1,), P, I32)])
    tok, w_cast_bf16 = _invert(hole_lo, hole_hi, dest.reshape(A), P, w_cast)
    n_used = pad_end[-1:] // rows
    block_start = jnp.minimum(jnp.arange(n_blocks, dtype=I32), n_used - 1) * rows
    block_e = jnp.sum((pad_end[None, :] <= block_start[:, None]).astype(I32), axis=1)
    return e_w, dest, tok, block_e, n_used, n_blocks, w_cast_bf16


def kernel(x, norm_mix_g, w_in, ret_norm_g, conv_w, conv_b, w_rg, b_rg, w_ig, b_ig, lru_lambda,
           lru_norm_g, w_out, norm_ffn_g, w_group, b_group, w_router, b_router, w_gate, w_up,
           w_down, norm_final_g):
    B, S, D = x.shape
    T = B * S
    depth = norm_mix_g.shape[0]
    assert depth == 1, "the combine kernel fuses the final norm, so only one layer is supported"
    H, d = RET_HEADS, HEAD_DIM

    half = d // 2
    inv = ROPE_BASE ** (-jnp.arange(half, dtype=F32) / half)
    ang = jnp.arange(S, dtype=F32)[:, None] * inv[None, :]
    cos_t = jnp.concatenate([jnp.cos(ang), jnp.cos(ang)], axis=-1)
    sin_t = jnp.concatenate([-jnp.sin(ang), jnp.sin(ang)], axis=-1)
    log_gamma = jnp.log1p(-(2.0 ** (-5.0 - jnp.arange(H, dtype=F32))))

    x2d = x.reshape(T, D)
    for l in range(depth):
        proj = _inproj(x2d, norm_mix_g[l][None, :], w_in[l].astype(BF16), INPROJ_TM, INPROJ_TN)
        proj3 = proj.reshape(B, S, proj.shape[1])
        ret, wg_bf16 = _retention(proj3, log_gamma, cos_t, sin_t, ret_norm_g[l][None, :], w_gate[l])
        w_gates = (-LOG2_E * jnp.concatenate([w_rg[l], w_ig[l]], axis=-1)).astype(BF16)
        lru, wu_bf16 = _rglru(proj3, conv_w[l], conv_b[l][None, :], w_gates, b_rg[l][None, :],
                              b_ig[l][None, :], lru_lambda[l][None, :], lru_norm_g[l][None, :],
                              w_up[l])

        n_route = N_GROUPS + N_EXPERTS
        wr = jnp.concatenate([w_group[l], w_router[l], jnp.zeros((D, LANES - n_route), F32)], axis=-1)
        br = jnp.concatenate([b_group[l], b_router[l], jnp.zeros((LANES - n_route,), F32)])[None, :]
        wr_hi = wr.astype(BF16)
        wr_lo = (wr - wr_hi.astype(F32)).astype(BF16)
        x1, h2, logits_t = _outproj(ret.reshape(T, -1), lru.reshape(T, -1), x2d, w_out[l].astype(BF16),
                                    norm_ffn_g[l][None, :], jnp.concatenate([wr_hi, wr_lo], axis=1),
                                    br, OUTPROJ_TM)
        route, counts = _route(logits_t, ROUTE_TM)

        e_w, dest, tok, block_e, n_used, n_blocks, wd_bf16 = _layout(route, counts, MOE_ROWS,
                                                                     w_down[l])
        y = _moe(block_e, n_used, tok.reshape(n_blocks, 1, MOE_ROWS), h2, wg_bf16, wu_bf16, wd_bf16)
        dest3 = dest.reshape(T // COMBINE_TM, COMBINE_TM, TOP_K).transpose(0, 2, 1).reshape(
            T // COMBINE_TM, 1, TOP_K * COMBINE_TM)
        x2d = _combine(dest3, y, x1, e_w, norm_final_g[None, :], COMBINE_TM)
    return x2d.reshape(B, S, D)
```

```python
import functools
import math

import jax
import jax.numpy as jnp
from jax import lax
from jax.experimental import pallas as pl
from jax.experimental.pallas import tpu as pltpu

F32 = jnp.float32
BF16 = jnp.bfloat16
I32 = jnp.int32

EPS = 1e-6
LOG2_E = 1.4426950408889634
RET_HEADS = 8
HEAD_DIM = 128
RET_CHUNK = 128
ROPE_BASE = 10000.0
LRU_BLOCKS = 8
LRU_BLOCK_DIM = 128
CONV_WIDTH = 4
LRU_C = 8.0
N_GROUPS = 4
EXPERTS_PER_GROUP = 8
N_EXPERTS = N_GROUPS * EXPERTS_PER_GROUP
TOP_K = 2

LANES = 128
SUBLANES = 8
VMEM_LIMIT = 56 * 1024 * 1024

INPROJ_TM = 1024
INPROJ_TN = 2048
LRU_ROWS = 128
OUTPROJ_TM = 512
ROUTE_TM = 2048
ROUTE_SUB = 256
ROUTE_ROWS = 40
INVERT_BLOCK = 2048
INVERT_UNROLL = 32
MOE_ROWS = 256
MOE_SLOTS = 4
MOE_UP_COLS = 256
MOE_DOWN_COLS = 512
COMBINE_TM = 256
COMBINE_SLOTS = 3
COMBINE_CHUNKS = 8


def _params(sem):
    return pltpu.CompilerParams(dimension_semantics=sem, vmem_limit_bytes=VMEM_LIMIT)


def _inproj_body(x_ref, g_ref, w_ref, o_ref, h_scr):
    @pl.when(pl.program_id(1) == 0)
    def _():
        x = x_ref[...]
        ms = jnp.mean(x * x, axis=-1, keepdims=True)
        h_scr[...] = (x * lax.rsqrt(ms + EPS) * g_ref[...]).astype(BF16)

    o_ref[...] = jnp.dot(h_scr[...], w_ref[...], preferred_element_type=F32).astype(o_ref.dtype)


def _inproj(x2d, g, w_bf16, tm, tn):
    T, D = x2d.shape
    N = w_bf16.shape[1]
    return pl.pallas_call(
        _inproj_body,
        grid=(T // tm, N // tn),
        in_specs=[
            pl.BlockSpec((tm, D), lambda i, j: (i, 0)),
            pl.BlockSpec((1, D), lambda i, j: (0, 0)),
            pl.BlockSpec((D, tn), lambda i, j: (0, j)),
        ],
        out_specs=pl.BlockSpec((tm, tn), lambda i, j: (i, j)),
        out_shape=jax.ShapeDtypeStruct((T, N), BF16),
        scratch_shapes=[pltpu.VMEM((tm, D), BF16)],
        compiler_params=_params(("arbitrary", "arbitrary")),
        name="inproj",
    )(x2d, g, w_bf16)


def _cast_plan(w, n_steps):
    E, R, C = w.shape
    if n_steps >= E:
        parts = n_steps // E
        assert n_steps == E * parts and R % parts == 0
        return (None, R // parts, C), (lambda s: (s // parts, s % parts, 0))
    per_step = E // n_steps
    assert E == per_step * n_steps
    return (per_step, R, C), (lambda s: (s, 0, 0))


def _retention_body(lg_ref, q_ref, k_ref, v_ref, g_ref, cos_ref, sin_ref, gn_ref, wsrc_ref, o_ref,
                    wdst_ref, mask_scr, qdec_scr, kdec_scr, qb_scr, kb_scr, qd_scr, kv_scr, sb_scr):
    wdst_ref[...] = wsrc_ref[...].astype(wdst_ref.dtype)
    C = RET_CHUNK
    d = HEAD_DIM
    S = q_ref.shape[0]
    lg = lg_ref[pl.program_id(1)]
    row = lax.broadcasted_iota(I32, (C, d), 0).astype(F32)
    col = lax.broadcasted_iota(I32, (C, d), 1).astype(F32)
    rel = row - col
    scale = d ** -0.5
    mask_scr[...] = jnp.where(rel >= 0, jnp.exp(jnp.maximum(rel, 0.0) * lg), 0.0) * scale
    qdec_scr[...] = jnp.exp((row + 1.0) * lg) * scale
    kdec_scr[...] = jnp.exp((C - 1.0 - row) * lg)
    c_dec = jnp.exp(jnp.full((1, d), float(C), F32) * lg)
    gn = gn_ref[...]
    n_chunks = S // C

    for n in range(n_chunks):
        sl = pl.ds(n * C, C)
        cos = cos_ref[sl, :]
        sin = sin_ref[sl, :]
        q = q_ref[sl, :].astype(F32)
        k = k_ref[sl, :].astype(F32)
        q = q * cos + pltpu.roll(q, d // 2, 1) * sin
        k = k * cos + pltpu.roll(k, d // 2, 1) * sin
        qb_scr[sl, :] = q.astype(BF16)
        kb_scr[sl, :] = k.astype(BF16)
        qd_scr[sl, :] = (q * qdec_scr[...]).astype(BF16)
        kd_t = (k * kdec_scr[...]).T.astype(BF16)
        kv_scr[n] = jnp.dot(kd_t, v_ref[sl, :], preferred_element_type=F32)
    state = jnp.zeros((d, d), F32)
    for n in range(n_chunks):
        sb_scr[n] = state.astype(BF16)
        state = c_dec * state + kv_scr[n]
    for n in range(n_chunks):
        sl = pl.ds(n * C, C)
        v = v_ref[sl, :]
        scores = lax.dot_general(qb_scr[sl, :], kb_scr[sl, :], (((1,), (1,)), ((), ())),
                                 preferred_element_type=F32) * mask_scr[...]
        o = jnp.dot(scores.astype(BF16), v, preferred_element_type=F32)
        o = o + jnp.dot(qd_scr[sl, :], sb_scr[n], preferred_element_type=F32)
        mu = jnp.mean(o, axis=-1, keepdims=True)
        var = jnp.maximum(jnp.mean(o * o, axis=-1, keepdims=True) - mu * mu, 0.0)
        on = (o - mu) * lax.rsqrt(var + EPS) * gn
        g = g_ref[sl, :].astype(F32)
        o_ref[sl, :] = ((g / (1.0 + jnp.exp2(g * (-LOG2_E)))) * on).astype(o_ref.dtype)


def _retention(proj3, log_gamma, cos_t, sin_t, ret_norm_g, w_cast):
    B, S, _ = proj3.shape
    H = RET_HEADS
    d = HEAD_DIM
    blk = lambda off: pl.BlockSpec((None, S, d), lambda b, h, off=off: (b, 0, off + h))
    wshape, wmap = _cast_plan(w_cast, B * H)
    wspec = pl.BlockSpec(wshape, lambda b, h: wmap(b * H + h))
    return pl.pallas_call(
        _retention_body,
        grid=(B, H),
        in_specs=[
            pl.BlockSpec(memory_space=pltpu.SMEM),
            blk(0), blk(H), blk(2 * H), blk(3 * H),
            pl.BlockSpec((S, d), lambda b, h: (0, 0)),
            pl.BlockSpec((S, d), lambda b, h: (0, 0)),
            pl.BlockSpec((1, d), lambda b, h: (0, h)),
            wspec,
        ],
        out_specs=[pl.BlockSpec((None, S, d), lambda b, h: (b, 0, h)), wspec],
        out_shape=[jax.ShapeDtypeStruct((B, S, H * d), BF16),
                   jax.ShapeDtypeStruct(w_cast.shape, BF16)],
        scratch_shapes=[pltpu.VMEM((RET_CHUNK, d), F32)] * 3 + [pltpu.VMEM((S, d), BF16)] * 3 + [
            pltpu.VMEM((S // RET_CHUNK, d, d), F32), pltpu.VMEM((S // RET_CHUNK, d, d), BF16)],
        compiler_params=_params(("arbitrary", "arbitrary")),
        name="retention",
    )(log_gamma, proj3, proj3, proj3, proj3, cos_t, sin_t, ret_norm_g, w_cast)


def _rglru_body(u_ref, z_ref, cw_ref, cb_ref, wg_ref, brg_ref, big_ref, lam_ref, gn_ref, wsrc_ref,
                o_ref, wdst_ref, uf_scr):
    wdst_ref[...] = wsrc_ref[...].astype(wdst_ref.dtype)
    S = u_ref.shape[0]
    W = LRU_BLOCK_DIM
    R = LRU_ROWS
    K = CONV_WIDTH
    nl = -lam_ref[...]
    softplus = jnp.maximum(nl, 0.0) + jnp.log1p(jnp.exp(-jnp.abs(nl)))
    coef = -LRU_C * LOG2_E * softplus
    cw = cw_ref[...]
    cb = cb_ref[...]
    brg = -LOG2_E * brg_ref[...]
    big = -LOG2_E * big_ref[...]
    gn = gn_ref[...]
    wg = wg_ref[...]
    row_in_tile = lax.broadcasted_iota(I32, (R, W), 0) & (SUBLANES - 1)
    uf_scr[:SUBLANES, :] = jnp.zeros((SUBLANES, W), F32)
    carry = jnp.zeros((1, W), F32)

    for c in range(S // R):
        base = SUBLANES + c * R
        u = u_ref[c * R:(c + 1) * R, :].astype(F32)
        uf_scr[base:base + R, :] = u
        uc = cb + cw[K - 1:K, :] * u
        for j in range(1, K):
            uc = uc + cw[K - 1 - j:K - j, :] * uf_scr[base - j:base - j + R, :]
        gates = jnp.dot(uc.astype(BF16), wg, preferred_element_type=F32)
        r = 1.0 / (1.0 + jnp.exp2(gates[:, :W] + brg))
        i = 1.0 / (1.0 + jnp.exp2(gates[:, W:] + big))
        a = jnp.exp2(coef * r)
        t = 1.0 - a * a
        b = jnp.where(t > 0.0, t * lax.rsqrt(t), 0.0) * (i * uc)
        for sh in (1, 2, 4):
            a_s = pltpu.roll(a, sh, 0)
            b_s = pltpu.roll(b, sh, 0)
            valid = row_in_tile >= sh
            b = jnp.where(valid, a * b_s + b, b)
            a = jnp.where(valid, a * a_s, a)
        tiles = []
        for k in range(R // SUBLANES):
            rows = slice(k * SUBLANES, (k + 1) * SUBLANES)
            h_tile = a[rows, :] * carry + b[rows, :]
            carry = h_tile[SUBLANES - 1:SUBLANES, :]
            tiles.append(h_tile)
        h = jnp.concatenate(tiles, axis=0)
        ms = jnp.mean(h * h, axis=-1, keepdims=True)
        hl = h * lax.rsqrt(ms + EPS) * gn
        z = z_ref[c * R:(c + 1) * R, :].astype(F32)
        gelu = 0.5 * z * (1.0 + jnp.tanh(math.sqrt(2.0 / math.pi) * (z + 0.044715 * (z * z * z))))
        o_ref[c * R:(c + 1) * R, :] = (hl * gelu).astype(o_ref.dtype)


def _rglru(proj3, conv_w, conv_b, w_gates, b_rg, b_ig, lam, lru_norm_g, w_cast):
    B, S, _ = proj3.shape
    NB = LRU_BLOCKS
    W = LRU_BLOCK_DIM
    u_off = 4 * RET_HEADS
    z_off = u_off + NB
    vec = pl.BlockSpec((1, W), lambda b, n: (0, n))
    wshape, wmap = _cast_plan(w_cast, B * NB)
    wspec = pl.BlockSpec(wshape, lambda b, n: wmap(b * NB + n))
    return pl.pallas_call(
        _rglru_body,
        grid=(B, NB),
        in_specs=[
            pl.BlockSpec((None, S, W), lambda b, n: (b, 0, u_off + n)),
            pl.BlockSpec((None, S, W), lambda b, n: (b, 0, z_off + n)),
            pl.BlockSpec((CONV_WIDTH, W), lambda b, n: (0, n)),
            vec,
            pl.BlockSpec((None, W, 2 * W), lambda b, n: (n, 0, 0)),
            vec, vec, vec, vec,
            wspec,
        ],
        out_specs=[pl.BlockSpec((None, S, W), lambda b, n: (b, 0, n)), wspec],
        out_shape=[jax.ShapeDtypeStruct((B, S, NB * W), BF16),
                   jax.ShapeDtypeStruct(w_cast.shape, BF16)],
        scratch_shapes=[pltpu.VMEM((SUBLANES + S, W), F32)],
        compiler_params=_params(("arbitrary", "arbitrary")),
        name="rglru",
    )(proj3, proj3, conv_w, conv_b, w_gates, b_rg, b_ig, lam, lru_norm_g, w_cast)


def _rows_to_tokens(x2d):
    n = x2d.shape[1] // LANES
    parts = [x2d[:, s * LANES:(s + 1) * LANES] for s in range(n)]
    return jnp.swapaxes(jnp.stack(parts, axis=0), 0, 1)


def _tokens_to_rows(x3d):
    xt = jnp.swapaxes(x3d, 0, 1)
    return jnp.concatenate([xt[s] for s in range(x3d.shape[1])], axis=-1)


def _first_row_of_max(v, row):
    m = jnp.max(v, axis=0, keepdims=True)
    idx = jnp.min(jnp.where(v == m, row, v.shape[0]), axis=0, keepdims=True)
    return m, idx


def _outproj_body(ret_ref, lru_ref, x_ref, wo_ref, g_ref, wr_ref, br_ref, x1_ref, h2_ref, lg_ref):
    R = ret_ref.shape[1]
    acc = jnp.dot(ret_ref[...], wo_ref[:R, :], preferred_element_type=F32)
    acc = acc + jnp.dot(lru_ref[...], wo_ref[R:, :], preferred_element_type=F32)
    x1 = x_ref[...] + acc
    x1_ref[...] = x1
    ms = jnp.mean(x1 * x1, axis=-1, keepdims=True)
    h2 = x1 * lax.rsqrt(ms + EPS) * g_ref[...]
    h2_ref[...] = _rows_to_tokens(h2).astype(h2_ref.dtype)
    h_hi = h2.astype(BF16)
    h_lo = (h2 - h_hi.astype(F32)).astype(BF16)
    NR = lg_ref.shape[0]
    both = jnp.dot(h_hi, wr_ref[...], preferred_element_type=F32)
    lg = both[:, :NR] + jnp.dot(h_lo, wr_ref[:, :NR], preferred_element_type=F32)
    lg_ref[...] = (lg + both[:, NR:] + br_ref[...]).T


def _outproj(ret2d, lru2d, x2d, wo_bf16, g, wr_hi_lo, br, tm):
    T, D = x2d.shape
    R = ret2d.shape[1]
    L = lru2d.shape[1]
    NR = br.shape[1]
    const = lambda shape: pl.BlockSpec(shape, lambda i: (0, 0))
    return pl.pallas_call(
        _outproj_body,
        grid=(T // tm,),
        in_specs=[
            pl.BlockSpec((tm, R), lambda i: (i, 0)),
            pl.BlockSpec((tm, L), lambda i: (i, 0)),
            pl.BlockSpec((tm, D), lambda i: (i, 0)),
            const((R + L, D)), const((1, D)), const((D, 2 * NR)), const((1, NR)),
        ],
        out_specs=[
            pl.BlockSpec((tm, D), lambda i: (i, 0)),
            pl.BlockSpec((tm, D // LANES, LANES), lambda i: (i, 0, 0)),
            pl.BlockSpec((NR, tm), lambda i: (0, i)),
        ],
        out_shape=[
            jax.ShapeDtypeStruct((T, D), F32),
            jax.ShapeDtypeStruct((T, D // LANES, LANES), BF16),
            jax.ShapeDtypeStruct((NR, T), F32),
        ],
        compiler_params=_params(("arbitrary",)),
        name="outproj",
    )(ret2d, lru2d, x2d, wo_bf16, g, wr_hi_lo, br)


def _route_body(lg_ref, route_ref, counts_ref, run_scr):
    G, EG = N_GROUPS, EXPERTS_PER_GROUP
    R = ROUTE_ROWS
    sub = ROUTE_SUB
    neg = -jnp.inf

    @pl.when(pl.program_id(0) == 0)
    def _():
        run_scr[...] = jnp.zeros_like(run_scr)

    row = lax.broadcasted_iota(I32, (R, sub), 0)
    r_i = lax.broadcasted_iota(I32, (sub, sub), 0)
    c_i = lax.broadcasted_iota(I32, (sub, sub), 1)
    before = (r_i < c_i).astype(BF16)
    ones = jnp.ones((sub, LANES), BF16)
    run = run_scr[...]
    for s in range(lg_ref.shape[1] // sub):
        cols = slice(s * sub, (s + 1) * sub)
        lg = lg_ref[0:R, cols]
        gl = jnp.where(row < G, lg, neg)
        g_max, g_idx = _first_row_of_max(gl, row)
        g_p = 1.0 / jnp.sum(jnp.exp(gl - g_max), axis=0, keepdims=True)
        lo = G + EG * g_idx
        el = jnp.where((row >= lo) & (row < lo + EG), lg, neg)
        t1, i1 = _first_row_of_max(el, row)
        el2 = jnp.where(row == i1, neg, el)
        t2, i2 = _first_row_of_max(el2, row)
        p2 = jnp.exp(t2 - t1)
        w1 = g_p / (1.0 + p2)
        w2 = g_p * p2 / (1.0 + p2)
        oh1 = row == i1
        oh2 = row == i2
        oh = (oh1 | oh2).astype(BF16)
        prefix = jnp.dot(oh, before, preferred_element_type=F32) + jnp.tile(run, (1, sub // LANES))
        rank1 = jnp.sum(jnp.where(oh1, prefix, 0.0), axis=0, keepdims=True)
        rank2 = jnp.sum(jnp.where(oh2, prefix, 0.0), axis=0, keepdims=True)
        run = run + jnp.dot(oh, ones, preferred_element_type=F32)
        zero = jnp.zeros((1, sub), F32)
        route_ref[:, cols] = jnp.concatenate(
            [(i1 - G).astype(F32), (i2 - G).astype(F32), w1, w2, rank1, rank2, zero, zero], axis=0)
    run_scr[...] = run
    counts_ref[...] = run


def _route(logits_t, tm):
    NR, T = logits_t.shape
    R = ROUTE_ROWS
    return pl.pallas_call(
        _route_body,
        grid=(T // tm,),
        in_specs=[pl.BlockSpec((NR, tm), lambda i: (0, i))],
        out_specs=[pl.BlockSpec((SUBLANES, tm), lambda i: (0, i)),
                   pl.BlockSpec((R, LANES), lambda i: (0, 0))],
        out_shape=[jax.ShapeDtypeStruct((SUBLANES, T), F32), jax.ShapeDtypeStruct((R, LANES), F32)],
        scratch_shapes=[pltpu.VMEM((R, LANES), F32)],
        compiler_params=_params(("arbitrary",)),
        name="route",
    )(logits_t)


def _start_row_gather(src_hbm, idx_ref, dst, sem, lo, hi):
    for r in range(lo, hi):
        pltpu.make_async_copy(src_hbm.at[pl.ds(idx_ref[0, r], 1)], dst.at[pl.ds(r, 1)], sem).start(
            priority=r % 2)


def _zero_after(tile, n_rows, n_cols):
    bits = pltpu.bitcast(tile, jnp.uint32)
    zero = pltpu.bitcast((bits >> 16) >> 16, F32)
    return jnp.tile(zero, (n_rows // SUBLANES, n_cols // LANES))


def _pack_bf16_pair(lo, hi):
    lo_bits = pltpu.bitcast(lo.astype(BF16).astype(F32), jnp.uint32) >> 16
    hi_bits = pltpu.bitcast(hi.astype(BF16).astype(F32), jnp.uint32) & jnp.uint32(0xFFFF0000)
    return lo_bits | hi_bits


def _unpack_bf16_pair(words):
    lo = pltpu.bitcast(words << 16, F32)
    hi = pltpu.bitcast(words & jnp.uint32(0xFFFF0000), F32)
    return lo, hi


def _wait_rows(dst, sem):
    pltpu.make_async_copy(dst, dst, sem).wait()


def _moe_body(be_ref, nu_ref, tok0_ref, tok1_ref, tok2_ref, tokn_ref, h2_hbm, wg_ref, wu_ref, wd_ref,
              y_ref, rowbuf, xb, sem):
    del be_ref
    i = pl.program_id(0)
    last = nu_ref[0] - 1
    rows = rowbuf.shape[1]

    @pl.when(i > last)
    def _():
        y_ref[...] = jnp.zeros_like(y_ref)

    @pl.when(i <= last)
    def _():
        @pl.when(i == 0)
        def _():
            for b, tok_ref in enumerate((tok0_ref, tok1_ref, tok2_ref)):
                _start_row_gather(h2_hbm, tok_ref, rowbuf.at[b], sem.at[b], 0, rows)
            _wait_rows(rowbuf.at[0], sem.at[0])
            xb[0] = _tokens_to_rows(rowbuf[0].astype(F32)).astype(BF16)

        s1 = lax.rem(i + 1, MOE_SLOTS)
        s2 = lax.rem(i + 2, MOE_SLOTS)
        s3 = lax.rem(i + 3, MOE_SLOTS)
        _wait_rows(rowbuf.at[s1], sem.at[s1])
        xcur = xb.at[lax.rem(i, 2)]
        xnext = _tokens_to_rows(rowbuf[s1].astype(F32)).astype(BF16)

        DE = wg_ref.shape[1]
        D = wd_ref.shape[1]
        n_up, n_down = DE // MOE_UP_COLS, D // MOE_DOWN_COLS
        per = rows // (n_up + n_down)
        ready, nxt, nsem = rowbuf.at[s1], rowbuf.at[s3], sem.at[s3]
        hparts = []
        for c in range(n_up):
            cs = slice(c * MOE_UP_COLS, (c + 1) * MOE_UP_COLS)
            _start_row_gather(h2_hbm, tokn_ref, nxt, nsem, c * per, (c + 1) * per)
            zero = _zero_after(ready[0], rows, MOE_UP_COLS)
            gate = jnp.dot(xcur[...], wg_ref[:, cs], preferred_element_type=F32)
            up = jnp.dot(xcur[...], wu_ref[:, cs], preferred_element_type=F32) + zero
            hparts.append(((gate / (1.0 + jnp.exp(-gate))) * up).astype(BF16))
        hmid = jnp.concatenate(hparts, axis=-1)
        xb[lax.rem(i + 1, 2)] = xnext
        yparts = []
        for c in range(n_down):
            cs = slice(c * MOE_DOWN_COLS, (c + 1) * MOE_DOWN_COLS)
            g = n_up + c
            _start_row_gather(h2_hbm, tokn_ref, nxt, nsem, g * per,
                              rows if c == n_down - 1 else (g + 1) * per)
            zero = _zero_after(ready[0], rows, MOE_DOWN_COLS)
            yparts.append(jnp.dot(hmid, wd_ref[:, cs], preferred_element_type=F32) + zero)
        for c in range(n_down // 2):
            cs = slice(c * MOE_DOWN_COLS, (c + 1) * MOE_DOWN_COLS)
            y_ref[:, cs] = _pack_bf16_pair(yparts[c], yparts[c + n_down // 2])

        @pl.when(i == last)
        def _():
            _wait_rows(rowbuf.at[s2], sem.at[s2])
            _wait_rows(nxt, nsem)


def _moe(block_e, n_used, tok3, h2, wg_bf16, wu_bf16, wd_bf16):
    NB, _, rows = tok3.shape
    T, n_tiles, _ = h2.shape
    E, D, DE = wg_bf16.shape
    tok_spec = lambda blk: pl.BlockSpec((None, 1, rows), lambda i, be, nu: (blk(i, nu[0] - 1), 0, 0),
                                        memory_space=pltpu.SMEM)
    w_spec = lambda shape: pl.BlockSpec(shape, lambda i, be, nu: (be[i], 0, 0))
    grid_spec = pltpu.PrefetchScalarGridSpec(
        num_scalar_prefetch=2,
        grid=(NB,),
        in_specs=[
            tok_spec(lambda i, last: 0),
            tok_spec(lambda i, last: jnp.minimum(1, last)),
            tok_spec(lambda i, last: jnp.minimum(2, last)),
            tok_spec(lambda i, last: jnp.minimum(i + 3, last)),
            pl.BlockSpec(memory_space=pl.ANY),
            w_spec((None, D, DE)), w_spec((None, D, DE)), w_spec((None, DE, D)),
        ],
        out_specs=pl.BlockSpec((rows, D // 2), lambda i, be, nu: (i, 0)),
        scratch_shapes=[pltpu.VMEM((MOE_SLOTS, rows, n_tiles, LANES), h2.dtype),
                        pltpu.VMEM((2, rows, D), BF16),
                        pltpu.SemaphoreType.DMA((MOE_SLOTS,))],
    )
    return pl.pallas_call(
        _moe_body,
        grid_spec=grid_spec,
        out_shape=jax.ShapeDtypeStruct((NB * rows, D // 2), jnp.uint32),
        compiler_params=_params(("arbitrary",)),
        name="moe",
    )(block_e, n_used, tok3, tok3, tok3, tok3, h2, wg_bf16, wu_bf16, wd_bf16)


def _combine_body(d0_ref, d1_ref, dn_ref, y_hbm, x1_ref, w_ref, g_ref, o_ref, ybuf, sem):
    i = pl.program_id(0)
    n = pl.num_programs(0)
    slot = lax.rem(i, COMBINE_SLOTS)
    slot1 = lax.rem(i + 1, COMBINE_SLOTS)
    slot2 = lax.rem(i + 2, COMBINE_SLOTS)
    tm = x1_ref.shape[0]

    @pl.when(i == 0)
    def _():
        _start_row_gather(y_hbm, d0_ref, ybuf.at[0], sem.at[0], 0, 2 * tm)
        _start_row_gather(y_hbm, d1_ref, ybuf.at[1], sem.at[1], 0, 2 * tm)

    _wait_rows(ybuf.at[slot], sem.at[slot])
    g = g_ref[...]
    yb, nxt, nsem = ybuf.at[slot], ybuf.at[slot2], sem.at[slot2]
    rc = tm // COMBINE_CHUNKS
    per = 2 * tm // COMBINE_CHUNKS
    for c in range(COMBINE_CHUNKS):
        rows = slice(c * rc, (c + 1) * rc)
        w = w_ref[rows, :]
        y0 = jnp.concatenate(_unpack_bf16_pair(yb[c * rc:(c + 1) * rc, :]), axis=-1)
        y1 = jnp.concatenate(_unpack_bf16_pair(yb[tm + c * rc:tm + (c + 1) * rc, :]), axis=-1)
        x = x1_ref[rows, :] + (w[:, 0:1] * y0 + w[:, 1:2] * y1)
        ms = jnp.mean(x * x, axis=-1, keepdims=True)
        o_ref[rows, :] = x * lax.rsqrt(ms + EPS) * g
        _start_row_gather(y_hbm, dn_ref, nxt, nsem, c * per, (c + 1) * per)

    @pl.when(i == n - 1)
    def _():
        _wait_rows(ybuf.at[slot1], sem.at[slot1])
        _wait_rows(nxt, nsem)


def _combine(dest3, y, x1, e_w, g, tm):
    T, D = x1.shape
    NT = T // tm
    return pl.pallas_call(
        _combine_body,
        grid=(NT,),
        in_specs=[
            pl.BlockSpec((None, 1, 2 * tm), lambda i: (0, 0, 0), memory_space=pltpu.SMEM),
            pl.BlockSpec((None, 1, 2 * tm), lambda i: (min(1, NT - 1), 0, 0), memory_space=pltpu.SMEM),
            pl.BlockSpec((None, 1, 2 * tm), lambda i: (jnp.minimum(i + 2, NT - 1), 0, 0),
                         memory_space=pltpu.SMEM),
            pl.BlockSpec(memory_space=pl.ANY),
            pl.BlockSpec((tm, D), lambda i: (i, 0)),
            pl.BlockSpec((tm, TOP_K), lambda i: (i, 0)),
            pl.BlockSpec((1, D), lambda i: (0, 0)),
        ],
        out_specs=pl.BlockSpec((tm, D), lambda i: (i, 0)),
        out_shape=jax.ShapeDtypeStruct((T, D), F32),
        scratch_shapes=[pltpu.VMEM((COMBINE_SLOTS, 2 * tm, y.shape[1]), y.dtype),
                        pltpu.SemaphoreType.DMA((COMBINE_SLOTS,))],
        compiler_params=_params(("arbitrary",)),
        name="combine",
    )(dest3, dest3, dest3, y, x1, e_w, g)


def _invert_body(lo_ref, hi_ref, dest_ref, wsrc_ref, tok_ref, wdst_ref):
    wdst_ref[...] = wsrc_ref[...].astype(wdst_ref.dtype)
    i = pl.program_id(0)
    blk = dest_ref.shape[1]

    @pl.when(i == 0)
    def _():
        def clear(j, carry):
            tok_ref[j] = 0
            return carry

        for s in range(lo_ref.shape[0]):
            lax.fori_loop(lo_ref[s], hi_ref[s], clear, 0)

    shift = TOP_K.bit_length() - 1
    n = INVERT_UNROLL

    def place(c, carry):
        j0 = pl.multiple_of(c * n, n)
        tok0 = lax.shift_right_logical(i * blk + j0, shift)
        for u in range(n):
            tok_ref[dest_ref[0, j0 + u]] = tok0 + (u >> shift)
        return carry

    lax.fori_loop(0, blk // n, place, 0)


def _invert(hole_lo, hole_hi, dest_flat, n_rows, w_cast):
    assert TOP_K & (TOP_K - 1) == 0
    A = dest_flat.shape[0]
    blk = min(INVERT_BLOCK, A)
    wshape, wmap = _cast_plan(w_cast, A // blk)
    wspec = pl.BlockSpec(wshape, lambda i, lo, hi: wmap(i))
    grid_spec = pltpu.PrefetchScalarGridSpec(
        num_scalar_prefetch=2,
        grid=(A // blk,),
        in_specs=[pl.BlockSpec((None, 1, blk), lambda i, lo, hi: (i, 0, 0), memory_space=pltpu.SMEM),
                  wspec],
        out_specs=[pl.BlockSpec(memory_space=pltpu.SMEM), wspec],
    )
    return pl.pallas_call(
        _invert_body,
        grid_spec=grid_spec,
        out_shape=[jax.ShapeDtypeStruct((n_rows,), I32), jax.ShapeDtypeStruct(w_cast.shape, BF16)],
        compiler_params=_params(("arbitrary",)),
        name="invert",
    )(hole_lo, hole_hi, dest_flat.reshape(A // blk, 1, blk), w_cast)


def _layout(route, counts, rows, w_cast):
    T = route.shape[1]
    A = T * TOP_K
    e_id = route[0:TOP_K].T.astype(I32)
    e_w = route[TOP_K:2 * TOP_K].T
    rank = route[2 * TOP_K:3 * TOP_K].T.astype(I32)
    counts = counts[N_GROUPS:N_GROUPS + N_EXPERTS, 0].astype(I32)
    padded = (counts + rows - 1) // rows * rows
    pad_end = jnp.cumsum(padded)
    pad_start = pad_end - padded
    experts = jnp.arange(N_EXPERTS, dtype=I32)
    start_of = jnp.sum(jnp.where(e_id[:, :, None] == experts, pad_start, 0), axis=-1)
    dest = start_of + rank
    n_blocks = (A + N_EXPERTS * (rows - 1)) // rows
    P = n_blocks * rows
    hole_lo = jnp.concatenate([pad_start + counts, pad_end[-1:]])
    hole_hi = jnp.concatenate([pad_end, jnp.full((1,), P, I32)])
    tok, w_cast_bf16 = _invert(hole_lo, hole_hi, dest.reshape(A), P, w_cast)
    n_used = pad_end[-1:] // rows
    block_start = jnp.minimum(jnp.arange(n_blocks, dtype=I32), n_used - 1) * rows
    block_e = jnp.sum((pad_end[None, :] <= block_start[:, None]).astype(I32), axis=1)
    return e_w, dest, tok, block_e, n_used, n_blocks, w_cast_bf16


def kernel(x, norm_mix_g, w_in, ret_norm_g, conv_w, conv_b, w_rg, b_rg, w_ig, b_ig, lru_lambda,
           lru_norm_g, w_out, norm_ffn_g, w_group, b_group, w_router, b_router, w_gate, w_up,
           w_down, norm_final_g):
    B, S, D = x.shape
    T = B * S
    depth = norm_mix_g.shape[0]
    assert depth == 1, "the combine kernel fuses the final norm, so only one layer is supported"
    H, d = RET_HEADS, HEAD_DIM

    half = d // 2
    inv = ROPE_BASE ** (-jnp.arange(half, dtype=F32) / half)
    ang = jnp.arange(S, dtype=F32)[:, None] * inv[None, :]
    cos_t = jnp.concatenate([jnp.cos(ang), jnp.cos(ang)], axis=-1)
    sin_t = jnp.concatenate([-jnp.sin(ang), jnp.sin(ang)], axis=-1)
    log_gamma = jnp.log1p(-(2.0 ** (-5.0 - jnp.arange(H, dtype=F32))))

    x2d = x.reshape(T, D)
    for l in range(depth):
        proj = _inproj(x2d, norm_mix_g[l][None, :], w_in[l].astype(BF16), INPROJ_TM, INPROJ_TN)
        proj3 = proj.reshape(B, S, proj.shape[1])
        ret, wg_bf16 = _retention(proj3, log_gamma, cos_t, sin_t, ret_norm_g[l][None, :], w_gate[l])
        w_gates = (-LOG2_E * jnp.concatenate([w_rg[l], w_ig[l]], axis=-1)).astype(BF16)
        lru, wu_bf16 = _rglru(proj3, conv_w[l], conv_b[l][None, :], w_gates, b_rg[l][None, :],
                              b_ig[l][None, :], lru_lambda[l][None, :], lru_norm_g[l][None, :],
                              w_up[l])

        n_route = N_GROUPS + N_EXPERTS
        wr = jnp.concatenate([w_group[l], w_router[l], jnp.zeros((D, LANES - n_route), F32)], axis=-1)
        br = jnp.concatenate([b_group[l], b_router[l], jnp.zeros((LANES - n_route,), F32)])[None, :]
        wr_hi = wr.astype(BF16)
        wr_lo = (wr - wr_hi.astype(F32)).astype(BF16)
        x1, h2, logits_t = _outproj(ret.reshape(T, -1), lru.reshape(T, -1), x2d, w_out[l].astype(BF16),
                                    norm_ffn_g[l][None, :], jnp.concatenate([wr_hi, wr_lo], axis=1),
                                    br, OUTPROJ_TM)
        route, counts = _route(logits_t, ROUTE_TM)

        e_w, dest, tok, block_e, n_used, n_blocks, wd_bf16 = _layout(route, counts, MOE_ROWS,
                                                                     w_down[l])
        y = _moe(block_e, n_used, tok.reshape(n_blocks, 1, MOE_ROWS), h2, wg_bf16, wu_bf16, wd_bf16)
        dest3 = dest.reshape(T // COMBINE_TM, COMBINE_TM, TOP_K).transpose(0, 2, 1).reshape(
            T // COMBINE_TM, 1, TOP_K * COMBINE_TM)
        x2d = _combine(dest3, y, x1, e_w, norm_final_g[None, :], COMBINE_TM)
    return x2d.reshape(B, S, D)
```

```python
import functools
import math

import jax
import jax.numpy as jnp
from jax import lax
from jax.experimental import pallas as pl
from jax.experimental.pallas import tpu as pltpu

F32 = jnp.float32
BF16 = jnp.bfloat16
I32 = jnp.int32

EPS = 1e-6
LOG2_E = 1.4426950408889634
RET_HEADS = 8
HEAD_DIM = 128
RET_CHUNK = 128
ROPE_BASE = 10000.0
LRU_BLOCKS = 8
LRU_BLOCK_DIM = 128
CONV_WIDTH = 4
LRU_C = 8.0
N_GROUPS = 4
EXPERTS_PER_GROUP = 8
N_EXPERTS = N_GROUPS * EXPERTS_PER_GROUP
TOP_K = 2

LANES = 128
SUBLANES = 8
VMEM_LIMIT = 56 * 1024 * 1024

INPROJ_TM = 1024
INPROJ_TN = 2048
LRU_ROWS = 128
OUTPROJ_TM = 512
ROUTE_TM = 2048
ROUTE_SUB = 256
ROUTE_ROWS = 40
INVERT_BLOCK = 2048
INVERT_UNROLL = 32
MOE_ROWS = 256
MOE_SLOTS = 4
MOE_UP_COLS = 256
MOE_DOWN_COLS = 512
COMBINE_TM = 256
COMBINE_SLOTS = 3
COMBINE_CHUNKS = 8


def _params(sem):
    return pltpu.CompilerParams(dimension_semantics=sem, vmem_limit_bytes=VMEM_LIMIT)


def _inproj_body(x_ref, g_ref, w_ref, o_ref, h_scr):
    @pl.when(pl.program_id(1) == 0)
    def _():
        x = x_ref[...]
        ms = jnp.mean(x * x, axis=-1, keepdims=True)
        h_scr[...] = (x * lax.rsqrt(ms + EPS) * g_ref[...]).astype(BF16)

    o_ref[...] = jnp.dot(h_scr[...], w_ref[...], preferred_element_type=F32).astype(o_ref.dtype)


def _inproj(x2d, g, w_bf16, tm, tn):
    T, D = x2d.shape
    N = w_bf16.shape[1]
    return pl.pallas_call(
        _inproj_body,
        grid=(T // tm, N // tn),
        in_specs=[
            pl.BlockSpec((tm, D), lambda i, j: (i, 0)),
            pl.BlockSpec((1, D), lambda i, j: (0, 0)),
            pl.BlockSpec((D, tn), lambda i, j: (0, j)),
        ],
        out_specs=pl.BlockSpec((tm, tn), lambda i, j: (i, j)),
        out_shape=jax.ShapeDtypeStruct((T, N), BF16),
        scratch_shapes=[pltpu.VMEM((tm, D), BF16)],
        compiler_params=_params(("arbitrary", "arbitrary")),
        name="inproj",
    )(x2d, g, w_bf16)


def _cast_plan(w, n_steps):
    E, R, C = w.shape
    if n_steps >= E:
        parts = n_steps // E
        assert n_steps == E * parts and R % parts == 0
        return (None, R // parts, C), (lambda s: (s // parts, s % parts, 0))
    per_step = E // n_steps
    assert E == per_step * n_steps
    return (per_step, R, C), (lambda s: (s, 0, 0))


def _retention_body(lg_ref, q_ref, k_ref, v_ref, g_ref, cos_ref, sin_ref, gn_ref, wsrc_ref, o_ref,
                    wdst_ref, mask_scr, qdec_scr, kdec_scr, qb_scr, kb_scr, qd_scr, kv_scr, sb_scr):
    wdst_ref[...] = wsrc_ref[...].astype(wdst_ref.dtype)
    C = RET_CHUNK
    d = HEAD_DIM
    S = q_ref.shape[0]
    lg = lg_ref[pl.program_id(1)]
    row = lax.broadcasted_iota(I32, (C, d), 0).astype(F32)
    col = lax.broadcasted_iota(I32, (C, d), 1).astype(F32)
    rel = row - col
    scale = d ** -0.5
    mask_scr[...] = jnp.where(rel >= 0, jnp.exp(jnp.maximum(rel, 0.0) * lg), 0.0) * scale
    qdec_scr[...] = jnp.exp((row + 1.0) * lg) * scale
    kdec_scr[...] = jnp.exp((C - 1.0 - row) * lg)
    c_dec = jnp.exp(jnp.full((1, d), float(C), F32) * lg)
    gn = gn_ref[...]
    n_chunks = S // C

    for n in range(n_chunks):
        sl = pl.ds(n * C, C)
        cos = cos_ref[sl, :]
        sin = sin_ref[sl, :]
        q = q_ref[sl, :].astype(F32)
        k = k_ref[sl, :].astype(F32)
        q = q * cos + pltpu.roll(q, d // 2, 1) * sin
        k = k * cos + pltpu.roll(k, d // 2, 1) * sin
        qb_scr[sl, :] = q.astype(BF16)
        kb_scr[sl, :] = k.astype(BF16)
        qd_scr[sl, :] = (q * qdec_scr[...]).astype(BF16)
        kd_t = (k * kdec_scr[...]).T.astype(BF16)
        kv_scr[n] = jnp.dot(kd_t, v_ref[sl, :], preferred_element_type=F32)
    state = jnp.zeros((d, d), F32)
    for n in range(n_chunks):
        sb_scr[n] = state.astype(BF16)
        state = c_dec * state + kv_scr[n]
    for n in range(n_chunks):
        sl = pl.ds(n * C, C)
        v = v_ref[sl, :]
        scores = lax.dot_general(qb_scr[sl, :], kb_scr[sl, :], (((1,), (1,)), ((), ())),
                                 preferred_element_type=F32) * mask_scr[...]
        o = jnp.dot(scores.astype(BF16), v, preferred_element_type=F32)
        o = o + jnp.dot(qd_scr[sl, :], sb_scr[n], preferred_element_type=F32)
        mu = jnp.mean(o, axis=-1, keepdims=True)
        var = jnp.maximum(jnp.mean(o * o, axis=-1, keepdims=True) - mu * mu, 0.0)
        on = (o - mu) * lax.rsqrt(var + EPS) * gn
        g = g_ref[sl, :].astype(F32)
        o_ref[sl, :] = ((g / (1.0 + jnp.exp2(g * (-LOG2_E)))) * on).astype(o_ref.dtype)


def _retention(proj3, log_gamma, cos_t, sin_t, ret_norm_g, w_cast):
    B, S, _ = proj3.shape
    H = RET_HEADS
    d = HEAD_DIM
    blk = lambda off: pl.BlockSpec((None, S, d), lambda b, h, off=off: (b, 0, off + h))
    wshape, wmap = _cast_plan(w_cast, B * H)
    wspec = pl.BlockSpec(wshape, lambda b, h: wmap(b * H + h))
    return pl.pallas_call(
        _retention_body,
        grid=(B, H),
        in_specs=[
            pl.BlockSpec(memory_space=pltpu.SMEM),
            blk(0), blk(H), blk(2 * H), blk(3 * H),
            pl.BlockSpec((S, d), lambda b, h: (0, 0)),
            pl.BlockSpec((S, d), lambda b, h: (0, 0)),
            pl.BlockSpec((1, d), lambda b, h: (0, h)),
            wspec,
        ],
        out_specs=[pl.BlockSpec((None, S, d), lambda b, h: (b, 0, h)), wspec],
        out_shape=[jax.ShapeDtypeStruct((B, S, H * d), BF16),
                   jax.ShapeDtypeStruct(w_cast.shape, BF16)],
        scratch_shapes=[pltpu.VMEM((RET_CHUNK, d), F32)] * 3 + [pltpu.VMEM((S, d), BF16)] * 3 + [
            pltpu.VMEM((S // RET_CHUNK, d, d), F32), pltpu.VMEM((S // RET_CHUNK, d, d), BF16)],
        compiler_params=_params(("arbitrary", "arbitrary")),
        name="retention",
    )(log_gamma, proj3, proj3, proj3, proj3, cos_t, sin_t, ret_norm_g, w_cast)


def _rglru_body(u_ref, z_ref, cw_ref, cb_ref, wg_ref, brg_ref, big_ref, lam_ref, gn_ref, wsrc_ref,
                o_ref, wdst_ref, uf_scr):
    wdst_ref[...] = wsrc_ref[...].astype(wdst_ref.dtype)
    S = u_ref.shape[0]
    W = LRU_BLOCK_DIM
    R = LRU_ROWS
    K = CONV_WIDTH
    nl = -lam_ref[...]
    softplus = jnp.maximum(nl, 0.0) + jnp.log1p(jnp.exp(-jnp.abs(nl)))
    coef = -LRU_C * LOG2_E * softplus
    cw = cw_ref[...]
    cb = cb_ref[...]
    brg = -LOG2_E * brg_ref[...]
    big = -LOG2_E * big_ref[...]
    gn = gn_ref[...]
    wg = wg_ref[...]
    row_in_tile = lax.broadcasted_iota(I32, (R, W), 0) & (SUBLANES - 1)
    shifts = (1, 2, 4)
    valids = [row_in_tile >= sh for sh in shifts]
    uf_scr[:SUBLANES, :] = jnp.zeros((SUBLANES, W), F32)
    carry = jnp.zeros((1, W), F32)

    for c in range(S // R):
        base = SUBLANES + c * R
        u = u_ref[c * R:(c + 1) * R, :].astype(F32)
        uf_scr[base:base + R, :] = u
        uc = cb + cw[K - 1:K, :] * u
        for j in range(1, K):
            uc = uc + cw[K - 1 - j:K - j, :] * uf_scr[base - j:base - j + R, :]
        gates = jnp.dot(uc.astype(BF16), wg, preferred_element_type=F32)
        r = 1.0 / (1.0 + jnp.exp2(gates[:, :W] + brg))
        i = 1.0 / (1.0 + jnp.exp2(gates[:, W:] + big))
        a = jnp.exp2(coef * r)
        t = 1.0 - a * a
        b = jnp.where(t > 0.0, t * lax.rsqrt(t), 0.0) * (i * uc)
        tiled = (R // SUBLANES, SUBLANES, W)
        for sh, valid in zip(shifts, valids):
            a_s = pltpu.roll(a.reshape(tiled), sh, 1).reshape(R, W)
            b_s = pltpu.roll(b.reshape(tiled), sh, 1).reshape(R, W)
            b = jnp.where(valid, a * b_s + b, b)
            a = jnp.where(valid, a * a_s, a)
        tiles = []
        for k in range(R // SUBLANES):
            rows = slice(k * SUBLANES, (k + 1) * SUBLANES)
            h_tile = a[rows, :] * carry + b[rows, :]
            carry = h_tile[SUBLANES - 1:SUBLANES, :]
            tiles.append(h_tile)
        h = jnp.concatenate(tiles, axis=0)
        ms = jnp.mean(h * h, axis=-1, keepdims=True)
        hl = h * lax.rsqrt(ms + EPS) * gn
        z = z_ref[c * R:(c + 1) * R, :].astype(F32)
        c0 = math.sqrt(2.0 / math.pi)
        gelu = 0.5 * z * (1.0 + jnp.tanh(z * (c0 + (c0 * 0.044715) * (z * z))))
        o_ref[c * R:(c + 1) * R, :] = (hl * gelu).astype(o_ref.dtype)


def _rglru(proj3, conv_w, conv_b, w_gates, b_rg, b_ig, lam, lru_norm_g, w_cast):
    B, S, _ = proj3.shape
    NB = LRU_BLOCKS
    W = LRU_BLOCK_DIM
    u_off = 4 * RET_HEADS
    z_off = u_off + NB
    vec = pl.BlockSpec((1, W), lambda b, n: (0, n))
    wshape, wmap = _cast_plan(w_cast, B * NB)
    wspec = pl.BlockSpec(wshape, lambda b, n: wmap(b * NB + n))
    return pl.pallas_call(
        _rglru_body,
        grid=(B, NB),
        in_specs=[
            pl.BlockSpec((None, S, W), lambda b, n: (b, 0, u_off + n)),
            pl.BlockSpec((None, S, W), lambda b, n: (b, 0, z_off + n)),
            pl.BlockSpec((CONV_WIDTH, W), lambda b, n: (0, n)),
            vec,
            pl.BlockSpec((None, W, 2 * W), lambda b, n: (n, 0, 0)),
            vec, vec, vec, vec,
            wspec,
        ],
        out_specs=[pl.BlockSpec((None, S, W), lambda b, n: (b, 0, n)), wspec],
        out_shape=[jax.ShapeDtypeStruct((B, S, NB * W), BF16),
                   jax.ShapeDtypeStruct(w_cast.shape, BF16)],
        scratch_shapes=[pltpu.VMEM((SUBLANES + S, W), F32)],
        compiler_params=_params(("arbitrary", "arbitrary")),
        name="rglru",
    )(proj3, proj3, conv_w, conv_b, w_gates, b_rg, b_ig, lam, lru_norm_g, w_cast)


def _rows_to_tokens(x2d):
    n = x2d.shape[1] // LANES
    parts = [x2d[:, s * LANES:(s + 1) * LANES] for s in range(n)]
    return jnp.swapaxes(jnp.stack(parts, axis=0), 0, 1)


def _tokens_to_rows(x3d):
    xt = jnp.swapaxes(x3d, 0, 1)
    return jnp.concatenate([xt[s] for s in range(x3d.shape[1])], axis=-1)


def _first_row_of_max(v, row):
    m = jnp.max(v, axis=0, keepdims=True)
    idx = jnp.min(jnp.where(v == m, row, v.shape[0]), axis=0, keepdims=True)
    return m, idx


def _outproj_body(ret_ref, lru_ref, x_ref, wo_ref, g_ref, wr_ref, br_ref, x1_ref, h2_ref, lg_ref):
    R = ret_ref.shape[1]
    acc = jnp.dot(ret_ref[...], wo_ref[:R, :], preferred_element_type=F32)
    acc = acc + jnp.dot(lru_ref[...], wo_ref[R:, :], preferred_element_type=F32)
    x1 = x_ref[...] + acc
    x1_ref[...] = x1
    ms = jnp.mean(x1 * x1, axis=-1, keepdims=True)
    h2 = x1 * lax.rsqrt(ms + EPS) * g_ref[...]
    h2_ref[...] = _rows_to_tokens(h2).astype(h2_ref.dtype)
    h_hi = h2.astype(BF16)
    h_lo = (h2 - h_hi.astype(F32)).astype(BF16)
    NR = lg_ref.shape[0]
    both = jnp.dot(h_hi, wr_ref[...], preferred_element_type=F32)
    lg = both[:, :NR] + jnp.dot(h_lo, wr_ref[:, :NR], preferred_element_type=F32)
    lg_ref[...] = (lg + both[:, NR:] + br_ref[...]).T


def _outproj(ret2d, lru2d, x2d, wo_bf16, g, wr_hi_lo, br, tm):
    T, D = x2d.shape
    R = ret2d.shape[1]
    L = lru2d.shape[1]
    NR = br.shape[1]
    const = lambda shape: pl.BlockSpec(shape, lambda i: (0, 0))
    return pl.pallas_call(
        _outproj_body,
        grid=(T // tm,),
        in_specs=[
            pl.BlockSpec((tm, R), lambda i: (i, 0)),
            pl.BlockSpec((tm, L), lambda i: (i, 0)),
            pl.BlockSpec((tm, D), lambda i: (i, 0)),
            const((R + L, D)), const((1, D)), const((D, 2 * NR)), const((1, NR)),
        ],
        out_specs=[
            pl.BlockSpec((tm, D), lambda i: (i, 0)),
            pl.BlockSpec((tm, D // LANES, LANES), lambda i: (i, 0, 0)),
            pl.BlockSpec((NR, tm), lambda i: (0, i)),
        ],
        out_shape=[
            jax.ShapeDtypeStruct((T, D), F32),
            jax.ShapeDtypeStruct((T, D // LANES, LANES), BF16),
            jax.ShapeDtypeStruct((NR, T), F32),
        ],
        compiler_params=_params(("arbitrary",)),
        name="outproj",
    )(ret2d, lru2d, x2d, wo_bf16, g, wr_hi_lo, br)


def _route_body(lg_ref, route_ref, counts_ref, run_scr):
    G, EG = N_GROUPS, EXPERTS_PER_GROUP
    R = ROUTE_ROWS
    sub = ROUTE_SUB
    neg = -jnp.inf

    @pl.when(pl.program_id(0) == 0)
    def _():
        run_scr[...] = jnp.zeros_like(run_scr)

    row = lax.broadcasted_iota(I32, (R, sub), 0)
    r_i = lax.broadcasted_iota(I32, (sub, sub), 0)
    c_i = lax.broadcasted_iota(I32, (sub, sub), 1)
    before = (r_i < c_i).astype(BF16)
    ones = jnp.ones((sub, LANES), BF16)
    run = run_scr[...]
    for s in range(lg_ref.shape[1] // sub):
        cols = slice(s * sub, (s + 1) * sub)
        lg = lg_ref[0:R, cols]
        gl = jnp.where(row < G, lg, neg)
        g_max, g_idx = _first_row_of_max(gl, row)
        g_p = 1.0 / jnp.sum(jnp.exp(gl - g_max), axis=0, keepdims=True)
        lo = G + EG * g_idx
        el = jnp.where((row >= lo) & (row < lo + EG), lg, neg)
        t1, i1 = _first_row_of_max(el, row)
        el2 = jnp.where(row == i1, neg, el)
        t2, i2 = _first_row_of_max(el2, row)
        p2 = jnp.exp(t2 - t1)
        w1 = g_p / (1.0 + p2)
        w2 = g_p * p2 / (1.0 + p2)
        oh1 = row == i1
        oh2 = row == i2
        oh = (oh1 | oh2).astype(BF16)
        prefix = jnp.dot(oh, before, preferred_element_type=F32) + jnp.tile(run, (1, sub // LANES))
        rank1 = jnp.sum(jnp.where(oh1, prefix, 0.0), axis=0, keepdims=True)
        rank2 = jnp.sum(jnp.where(oh2, prefix, 0.0), axis=0, keepdims=True)
        run = run + jnp.dot(oh, ones, preferred_element_type=F32)
        zero = jnp.zeros((1, sub), F32)
        route_ref[:, cols] = jnp.concatenate(
            [(i1 - G).astype(F32), (i2 - G).astype(F32), w1, w2, rank1, rank2, zero, zero], axis=0)
    run_scr[...] = run
    counts_ref[...] = run


def _route(logits_t, tm):
    NR, T = logits_t.shape
    R = ROUTE_ROWS
    return pl.pallas_call(
        _route_body,
        grid=(T // tm,),
        in_specs=[pl.BlockSpec((NR, tm), lambda i: (0, i))],
        out_specs=[pl.BlockSpec((SUBLANES, tm), lambda i: (0, i)),
                   pl.BlockSpec((R, LANES), lambda i: (0, 0))],
        out_shape=[jax.ShapeDtypeStruct((SUBLANES, T), F32), jax.ShapeDtypeStruct((R, LANES), F32)],
        scratch_shapes=[pltpu.VMEM((R, LANES), F32)],
        compiler_params=_params(("arbitrary",)),
        name="route",
    )(logits_t)


def _start_row_gather(src_hbm, idx_ref, dst, sem, lo, hi):
    for r in range(lo, hi):
        pltpu.make_async_copy(src_hbm.at[pl.ds(idx_ref[0, r], 1)], dst.at[pl.ds(r, 1)], sem).start(
            priority=r % 2)


def _zero_after(tile, n_rows, n_cols):
    bits = pltpu.bitcast(tile, jnp.uint32)
    zero = pltpu.bitcast((bits >> 16) >> 16, F32)
    return jnp.tile(zero, (n_rows // SUBLANES, n_cols // LANES))


def _pack_bf16_pair(lo, hi):
    lo_bits = pltpu.bitcast(lo.astype(BF16).astype(F32), jnp.uint32) >> 16
    hi_bits = pltpu.bitcast(hi.astype(BF16).astype(F32), jnp.uint32) & jnp.uint32(0xFFFF0000)
    return lo_bits | hi_bits


def _unpack_bf16_pair(words):
    lo = pltpu.bitcast(words << 16, F32)
    hi = pltpu.bitcast(words & jnp.uint32(0xFFFF0000), F32)
    return lo, hi


def _wait_rows(dst, sem):
    pltpu.make_async_copy(dst, dst, sem).wait()


def _moe_body(be_ref, nu_ref, tok0_ref, tok1_ref, tok2_ref, tokn_ref, h2_hbm, wg_ref, wu_ref, wd_ref,
              y_ref, rowbuf, xb, sem):
    del be_ref
    i = pl.program_id(0)
    last = nu_ref[0] - 1
    rows = rowbuf.shape[1]

    @pl.when(i > last)
    def _():
        y_ref[...] = jnp.zeros_like(y_ref)

    @pl.when(i <= last)
    def _():
        @pl.when(i == 0)
        def _():
            for b, tok_ref in enumerate((tok0_ref, tok1_ref, tok2_ref)):
                _start_row_gather(h2_hbm, tok_ref, rowbuf.at[b], sem.at[b], 0, rows)
            _wait_rows(rowbuf.at[0], sem.at[0])
            xb[0] = _tokens_to_rows(rowbuf[0].astype(F32)).astype(BF16)

        s1 = lax.rem(i + 1, MOE_SLOTS)
        s2 = lax.rem(i + 2, MOE_SLOTS)
        s3 = lax.rem(i + 3, MOE_SLOTS)
        _wait_rows(rowbuf.at[s1], sem.at[s1])
        xcur = xb.at[lax.rem(i, 2)]
        xnext = _tokens_to_rows(rowbuf[s1].astype(F32)).astype(BF16)

        DE = wg_ref.shape[1]
        D = wd_ref.shape[1]
        n_up, n_down = DE // MOE_UP_COLS, D // MOE_DOWN_COLS
        per = rows // (n_up + n_down)
        ready, nxt, nsem = rowbuf.at[s1], rowbuf.at[s3], sem.at[s3]
        hparts = []
        for c in range(n_up):
            cs = slice(c * MOE_UP_COLS, (c + 1) * MOE_UP_COLS)
            _start_row_gather(h2_hbm, tokn_ref, nxt, nsem, c * per, (c + 1) * per)
            zero = _zero_after(ready[0], rows, MOE_UP_COLS)
            gate = jnp.dot(xcur[...], wg_ref[:, cs], preferred_element_type=F32)
            up = jnp.dot(xcur[...], wu_ref[:, cs], preferred_element_type=F32) + zero
            hparts.append(((gate / (1.0 + jnp.exp(-gate))) * up).astype(BF16))
        hmid = jnp.concatenate(hparts, axis=-1)
        xb[lax.rem(i + 1, 2)] = xnext
        yparts = []
        for c in range(n_down):
            cs = slice(c * MOE_DOWN_COLS, (c + 1) * MOE_DOWN_COLS)
            g = n_up + c
            _start_row_gather(h2_hbm, tokn_ref, nxt, nsem, g * per,
                              rows if c == n_down - 1 else (g + 1) * per)
            zero = _zero_after(ready[0], rows, MOE_DOWN_COLS)
            yparts.append(jnp.dot(hmid, wd_ref[:, cs], preferred_element_type=F32) + zero)
        for c in range(n_down // 2):
            cs = slice(c * MOE_DOWN_COLS, (c + 1) * MOE_DOWN_COLS)
            y_ref[:, cs] = _pack_bf16_pair(yparts[c], yparts[c + n_down // 2])

        @pl.when(i == last)
        def _():
            _wait_rows(rowbuf.at[s2], sem.at[s2])
            _wait_rows(nxt, nsem)


def _moe(block_e, n_used, tok3, h2, wg_bf16, wu_bf16, wd_bf16):
    NB, _, rows = tok3.shape
    T, n_tiles, _ = h2.shape
    E, D, DE = wg_bf16.shape
    tok_spec = lambda blk: pl.BlockSpec((None, 1, rows), lambda i, be, nu: (blk(i, nu[0] - 1), 0, 0),
                                        memory_space=pltpu.SMEM)
    w_spec = lambda shape: pl.BlockSpec(shape, lambda i, be, nu: (be[i], 0, 0))
    grid_spec = pltpu.PrefetchScalarGridSpec(
        num_scalar_prefetch=2,
        grid=(NB,),
        in_specs=[
            tok_spec(lambda i, last: 0),
            tok_spec(lambda i, last: jnp.minimum(1, last)),
            tok_spec(lambda i, last: jnp.minimum(2, last)),
            tok_spec(lambda i, last: jnp.minimum(i + 3, last)),
            pl.BlockSpec(memory_space=pl.ANY),
            w_spec((None, D, DE)), w_spec((None, D, DE)), w_spec((None, DE, D)),
        ],
        out_specs=pl.BlockSpec((rows, D // 2), lambda i, be, nu: (i, 0)),
        scratch_shapes=[pltpu.VMEM((MOE_SLOTS, rows, n_tiles, LANES), h2.dtype),
                        pltpu.VMEM((2, rows, D), BF16),
                        pltpu.SemaphoreType.DMA((MOE_SLOTS,))],
    )
    return pl.pallas_call(
        _moe_body,
        grid_spec=grid_spec,
        out_shape=jax.ShapeDtypeStruct((NB * rows, D // 2), jnp.uint32),
        compiler_params=_params(("arbitrary",)),
        name="moe",
    )(block_e, n_used, tok3, tok3, tok3, tok3, h2, wg_bf16, wu_bf16, wd_bf16)


def _combine_body(d0_ref, d1_ref, dn_ref, y_hbm, x1_ref, w_ref, g_ref, o_ref, ybuf, sem):
    i = pl.program_id(0)
    n = pl.num_programs(0)
    slot = lax.rem(i, COMBINE_SLOTS)
    slot1 = lax.rem(i + 1, COMBINE_SLOTS)
    slot2 = lax.rem(i + 2, COMBINE_SLOTS)
    tm = x1_ref.shape[0]

    @pl.when(i == 0)
    def _():
        _start_row_gather(y_hbm, d0_ref, ybuf.at[0], sem.at[0], 0, 2 * tm)
        _start_row_gather(y_hbm, d1_ref, ybuf.at[1], sem.at[1], 0, 2 * tm)

    _wait_rows(ybuf.at[slot], sem.at[slot])
    g = g_ref[...]
    yb, nxt, nsem = ybuf.at[slot], ybuf.at[slot2], sem.at[slot2]
    rc = tm // COMBINE_CHUNKS
    per = 2 * tm // COMBINE_CHUNKS
    for c in range(COMBINE_CHUNKS):
        rows = slice(c * rc, (c + 1) * rc)
        w = w_ref[rows, :]
        y0 = jnp.concatenate(_unpack_bf16_pair(yb[c * rc:(c + 1) * rc, :]), axis=-1)
        y1 = jnp.concatenate(_unpack_bf16_pair(yb[tm + c * rc:tm + (c + 1) * rc, :]), axis=-1)
        x = x1_ref[rows, :] + (w[:, 0:1] * y0 + w[:, 1:2] * y1)
        ms = jnp.mean(x * x, axis=-1, keepdims=True)
        o_ref[rows, :] = x * lax.rsqrt(ms + EPS) * g
        _start_row_gather(y_hbm, dn_ref, nxt, nsem, c * per, (c + 1) * per)

    @pl.when(i == n - 1)
    def _():
        _wait_rows(ybuf.at[slot1], sem.at[slot1])
        _wait_rows(nxt, nsem)


def _combine(dest3, y, x1, e_w, g, tm):
    T, D = x1.shape
    NT = T // tm
    return pl.pallas_call(
        _combine_body,
        grid=(NT,),
        in_specs=[
            pl.BlockSpec((None, 1, 2 * tm), lambda i: (0, 0, 0), memory_space=pltpu.SMEM),
            pl.BlockSpec((None, 1, 2 * tm), lambda i: (min(1, NT - 1), 0, 0), memory_space=pltpu.SMEM),
            pl.BlockSpec((None, 1, 2 * tm), lambda i: (jnp.minimum(i + 2, NT - 1), 0, 0),
                         memory_space=pltpu.SMEM),
            pl.BlockSpec(memory_space=pl.ANY),
            pl.BlockSpec((tm, D), lambda i: (i, 0)),
            pl.BlockSpec((tm, TOP_K), lambda i: (i, 0)),
            pl.BlockSpec((1, D), lambda i: (0, 0)),
        ],
        out_specs=pl.BlockSpec((tm, D), lambda i: (i, 0)),
        out_shape=jax.ShapeDtypeStruct((T, D), F32),
        scratch_shapes=[pltpu.VMEM((COMBINE_SLOTS, 2 * tm, y.shape[1]), y.dtype),
                        pltpu.SemaphoreType.DMA((COMBINE_SLOTS,))],
        compiler_params=_params(("arbitrary",)),
        name="combine",
    )(dest3, dest3, dest3, y, x1, e_w, g)


def _invert_body(lo_ref, hi_ref, dest_ref, wsrc_ref, tok_ref, wdst_ref):
    wdst_ref[...] = wsrc_ref[...].astype(wdst_ref.dtype)
    i = pl.program_id(0)
    blk = dest_ref.shape[1]

    @pl.when(i == 0)
    def _():
        def clear(j, carry):
            tok_ref[j] = 0
            return carry

        for s in range(lo_ref.shape[0]):
            lax.fori_loop(lo_ref[s], hi_ref[s], clear, 0)

    shift = TOP_K.bit_length() - 1
    n = INVERT_UNROLL

    def place(c, carry):
        j0 = pl.multiple_of(c * n, n)
        tok0 = lax.shift_right_logical(i * blk + j0, shift)
        for u in range(n):
            tok_ref[dest_ref[0, j0 + u]] = tok0 + (u >> shift)
        return carry

    lax.fori_loop(0, blk // n, place, 0)


def _invert(hole_lo, hole_hi, dest_flat, n_rows, w_cast):
    assert TOP_K & (TOP_K - 1) == 0
    A = dest_flat.shape[0]
    blk = min(INVERT_BLOCK, A)
    wshape, wmap = _cast_plan(w_cast, A // blk)
    wspec = pl.BlockSpec(wshape, lambda i, lo, hi: wmap(i))
    grid_spec = pltpu.PrefetchScalarGridSpec(
        num_scalar_prefetch=2,
        grid=(A // blk,),
        in_specs=[pl.BlockSpec((None, 1, blk), lambda i, lo, hi: (i, 0, 0), memory_space=pltpu.SMEM),
                  wspec],
        out_specs=[pl.BlockSpec(memory_space=pltpu.SMEM), wspec],
    )
    return pl.pallas_call(
        _invert_body,
        grid_spec=grid_spec,
        out_shape=[jax.ShapeDtypeStruct((n_rows,), I32), jax.ShapeDtypeStruct(w_cast.shape, BF16)],
        compiler_params=_params(("arbitrary",)),
        name="invert",
    )(hole_lo, hole_hi, dest_flat.reshape(A // blk, 1, blk), w_cast)


def _layout(route, counts, rows, w_cast):
    T = route.shape[1]
    A = T * TOP_K
    e_id = route[0:TOP_K].T.astype(I32)
    e_w = route[TOP_K:2 * TOP_K].T
    rank = route[2 * TOP_K:3 * TOP_K].T.astype(I32)
    counts = counts[N_GROUPS:N_GROUPS + N_EXPERTS, 0].astype(I32)
    padded = (counts + rows - 1) // rows * rows
    pad_end = jnp.cumsum(padded)
    pad_start = pad_end - padded
    experts = jnp.arange(N_EXPERTS, dtype=I32)
    start_of = jnp.sum(jnp.where(e_id[:, :, None] == experts, pad_start, 0), axis=-1)
    dest = start_of + rank
    n_blocks = (A + N_EXPERTS * (rows - 1)) // rows
    P = n_blocks * rows
    hole_lo = jnp.concatenate([pad_start + counts, pad_end[-1:]])
    hole_hi = jnp.concatenate([pad_end, jnp.full((1,), P, I32)])
    tok, w_cast_bf16 = _invert(hole_lo, hole_hi, dest.reshape(A), P, w_cast)
    n_used = pad_end[-1:] // rows
    block_start = jnp.minimum(jnp.arange(n_blocks, dtype=I32), n_used - 1) * rows
    block_e = jnp.sum((pad_end[None, :] <= block_start[:, None]).astype(I32), axis=1)
    return e_w, dest, tok, block_e, n_used, n_blocks, w_cast_bf16


def kernel(x, norm_mix_g, w_in, ret_norm_g, conv_w, conv_b, w_rg, b_rg, w_ig, b_ig, lru_lambda,
           lru_norm_g, w_out, norm_ffn_g, w_group, b_group, w_router, b_router, w_gate, w_up,
           w_down, norm_final_g):
    B, S, D = x.shape
    T = B * S
    depth = norm_mix_g.shape[0]
    assert depth == 1, "the combine kernel fuses the final norm, so only one layer is supported"
    H, d = RET_HEADS, HEAD_DIM

    half = d // 2
    inv = ROPE_BASE ** (-jnp.arange(half, dtype=F32) / half)
    ang = jnp.arange(S, dtype=F32)[:, None] * inv[None, :]
    cos_t = jnp.concatenate([jnp.cos(ang), jnp.cos(ang)], axis=-1)
    sin_t = jnp.concatenate([-jnp.sin(ang), jnp.sin(ang)], axis=-1)
    log_gamma = jnp.log1p(-(2.0 ** (-5.0 - jnp.arange(H, dtype=F32))))

    x2d = x.reshape(T, D)
    for l in range(depth):
        proj = _inproj(x2d, norm_mix_g[l][None, :], w_in[l].astype(BF16), INPROJ_TM, INPROJ_TN)
        proj3 = proj.reshape(B, S, proj.shape[1])
        ret, wg_bf16 = _retention(proj3, log_gamma, cos_t, sin_t, ret_norm_g[l][None, :], w_gate[l])
        w_gates = (-LOG2_E * jnp.concatenate([w_rg[l], w_ig[l]], axis=-1)).astype(BF16)
        lru, wu_bf16 = _rglru(proj3, conv_w[l], conv_b[l][None, :], w_gates, b_rg[l][None, :],
                              b_ig[l][None, :], lru_lambda[l][None, :], lru_norm_g[l][None, :],
                              w_up[l])

        n_route = N_GROUPS + N_EXPERTS
        wr = jnp.concatenate([w_group[l], w_router[l], jnp.zeros((D, LANES - n_route), F32)], axis=-1)
        br = jnp.concatenate([b_group[l], b_router[l], jnp.zeros((LANES - n_route,), F32)])[None, :]
        wr_hi = wr.astype(BF16)
        wr_lo = (wr - wr_hi.astype(F32)).astype(BF16)
        x1, h2, logits_t = _outproj(ret.reshape(T, -1), lru.reshape(T, -1), x2d, w_out[l].astype(BF16),
                                    norm_ffn_g[l][None, :], jnp.concatenate([wr_hi, wr_lo], axis=1),
                                    br, OUTPROJ_TM)
        route, counts = _route(logits_t, ROUTE_TM)

        e_w, dest, tok, block_e, n_used, n_blocks, wd_bf16 = _layout(route, counts, MOE_ROWS,
                                                                     w_down[l])
        y = _moe(block_e, n_used, tok.reshape(n_blocks, 1, MOE_ROWS), h2, wg_bf16, wu_bf16, wd_bf16)
        dest3 = dest.reshape(T // COMBINE_TM, COMBINE_TM, TOP_K).transpose(0, 2, 1).reshape(
            T // COMBINE_TM, 1, TOP_K * COMBINE_TM)
        x2d = _combine(dest3, y, x1, e_w, norm_final_g[None, :], COMBINE_TM)
    return x2d.reshape(B, S, D)
```

```python
import functools
import math

import jax
import jax.numpy as jnp
from jax import lax
from jax.experimental import pallas as pl
from jax.experimental.pallas import tpu as pltpu

F32 = jnp.float32
BF16 = jnp.bfloat16
I32 = jnp.int32

EPS = 1e-6
LOG2_E = 1.4426950408889634
RET_HEADS = 8
HEAD_DIM = 128
RET_CHUNK = 128
ROPE_BASE = 10000.0
LRU_BLOCKS = 8
LRU_BLOCK_DIM = 128
CONV_WIDTH = 4
LRU_C = 8.0
N_GROUPS = 4
EXPERTS_PER_GROUP = 8
N_EXPERTS = N_GROUPS * EXPERTS_PER_GROUP
TOP_K = 2

LANES = 128
SUBLANES = 8
VMEM_LIMIT = 56 * 1024 * 1024

INPROJ_TM = 1024
INPROJ_TN = 2048
RET_HEADS_PER_STEP = 2
LRU_ROWS = 128
LRU_BLOCKS_PER_STEP = 2
OUTPROJ_TM = 512
ROUTE_TM = 2048
ROUTE_SUB = 256
ROUTE_ROWS = 40
INVERT_BLOCK = 2048
INVERT_UNROLL = 32
MOE_ROWS = 256
MOE_SLOTS = 4
MOE_UP_COLS = 256
MOE_DOWN_COLS = 512
COMBINE_TM = 256
COMBINE_SLOTS = 3
COMBINE_CHUNKS = 8


def _params(sem):
    return pltpu.CompilerParams(dimension_semantics=sem, vmem_limit_bytes=VMEM_LIMIT)


def _inproj_body(x_ref, g_ref, w_ref, o_ref, h_scr):
    @pl.when(pl.program_id(1) == 0)
    def _():
        x = x_ref[...]
        ms = jnp.mean(x * x, axis=-1, keepdims=True)
        h_scr[...] = (x * lax.rsqrt(ms + EPS) * g_ref[...]).astype(BF16)

    o_ref[...] = jnp.dot(h_scr[...], w_ref[...], preferred_element_type=F32).astype(o_ref.dtype)


def _inproj(x2d, g, w_bf16, tm, tn):
    T, D = x2d.shape
    N = w_bf16.shape[1]
    return pl.pallas_call(
        _inproj_body,
        grid=(T // tm, N // tn),
        in_specs=[
            pl.BlockSpec((tm, D), lambda i, j: (i, 0)),
            pl.BlockSpec((1, D), lambda i, j: (0, 0)),
            pl.BlockSpec((D, tn), lambda i, j: (0, j)),
        ],
        out_specs=pl.BlockSpec((tm, tn), lambda i, j: (i, j)),
        out_shape=jax.ShapeDtypeStruct((T, N), BF16),
        scratch_shapes=[pltpu.VMEM((tm, D), BF16)],
        compiler_params=_params(("arbitrary", "arbitrary")),
        name="inproj",
    )(x2d, g, w_bf16)


def _cast_plan(w, n_steps):
    E, R, C = w.shape
    if n_steps >= E:
        parts = n_steps // E
        assert n_steps == E * parts and R % parts == 0
        return (None, R // parts, C), (lambda s: (s // parts, s % parts, 0))
    per_step = E // n_steps
    assert E == per_step * n_steps
    return (per_step, R, C), (lambda s: (s, 0, 0))


def _retention_body(lg_ref, q_ref, k_ref, v_ref, g_ref, cos_ref, sin_ref, gn_ref, wsrc_ref, o_ref,
                    wdst_ref, mask_scr, qdec_scr, kdec_scr, qb_scr, kb_scr, qd_scr, kv_scr, sb_scr):
    wdst_ref[...] = wsrc_ref[...].astype(wdst_ref.dtype)
    C = RET_CHUNK
    d = HEAD_DIM
    S = q_ref.shape[0]
    row = lax.broadcasted_iota(I32, (C, d), 0).astype(F32)
    col = lax.broadcasted_iota(I32, (C, d), 1).astype(F32)
    rel = row - col
    scale = d ** -0.5
    n_chunks = S // C

    for hh in range(RET_HEADS_PER_STEP):
        hc = slice(hh * d, (hh + 1) * d)
        lg = lg_ref[pl.program_id(1) * RET_HEADS_PER_STEP + hh]
        mask_scr[...] = jnp.where(rel >= 0, jnp.exp(jnp.maximum(rel, 0.0) * lg), 0.0) * scale
        qdec_scr[...] = jnp.exp((row + 1.0) * lg) * scale
        kdec_scr[...] = jnp.exp((C - 1.0 - row) * lg)
        c_dec = jnp.exp(jnp.full((1, d), float(C), F32) * lg)
        gn = gn_ref[:, hc]

        for n in range(n_chunks):
            sl = pl.ds(n * C, C)
            cos = cos_ref[sl, :]
            sin = sin_ref[sl, :]
            q = q_ref[sl, hc].astype(F32)
            k = k_ref[sl, hc].astype(F32)
            q = q * cos + pltpu.roll(q, d // 2, 1) * sin
            k = k * cos + pltpu.roll(k, d // 2, 1) * sin
            qb_scr[sl, :] = q.astype(BF16)
            kb_scr[sl, :] = k.astype(BF16)
            qd_scr[sl, :] = (q * qdec_scr[...]).astype(BF16)
            kd_t = (k * kdec_scr[...]).T.astype(BF16)
            kv_scr[n] = jnp.dot(kd_t, v_ref[sl, hc], preferred_element_type=F32)
        state = jnp.zeros((d, d), F32)
        for n in range(n_chunks):
            sb_scr[n] = state.astype(BF16)
            state = c_dec * state + kv_scr[n]
        for n in range(n_chunks):
            sl = pl.ds(n * C, C)
            v = v_ref[sl, hc]
            scores = lax.dot_general(qb_scr[sl, :], kb_scr[sl, :], (((1,), (1,)), ((), ())),
                                     preferred_element_type=F32) * mask_scr[...]
            o = jnp.dot(scores.astype(BF16), v, preferred_element_type=F32)
            o = o + jnp.dot(qd_scr[sl, :], sb_scr[n], preferred_element_type=F32)
            mu = jnp.mean(o, axis=-1, keepdims=True)
            var = jnp.maximum(jnp.mean(o * o, axis=-1, keepdims=True) - mu * mu, 0.0)
            on = (o - mu) * lax.rsqrt(var + EPS) * gn
            g = g_ref[sl, hc].astype(F32)
            o_ref[sl, hc] = ((g / (1.0 + jnp.exp2(g * (-LOG2_E)))) * on).astype(o_ref.dtype)


def _retention(proj3, log_gamma, cos_t, sin_t, ret_norm_g, w_cast):
    B, S, _ = proj3.shape
    H = RET_HEADS
    d = HEAD_DIM
    hp = RET_HEADS_PER_STEP
    ng = H // hp
    blk = lambda off: pl.BlockSpec((None, S, hp * d), lambda b, h, off=off: (b, 0, off + h))
    wshape, wmap = _cast_plan(w_cast, B * ng)
    wspec = pl.BlockSpec(wshape, lambda b, h: wmap(b * ng + h))
    return pl.pallas_call(
        _retention_body,
        grid=(B, ng),
        in_specs=[
            pl.BlockSpec(memory_space=pltpu.SMEM),
            blk(0), blk(ng), blk(2 * ng), blk(3 * ng),
            pl.BlockSpec((S, d), lambda b, h: (0, 0)),
            pl.BlockSpec((S, d), lambda b, h: (0, 0)),
            pl.BlockSpec((1, hp * d), lambda b, h: (0, h)),
            wspec,
        ],
        out_specs=[pl.BlockSpec((None, S, hp * d), lambda b, h: (b, 0, h)), wspec],
        out_shape=[jax.ShapeDtypeStruct((B, S, H * d), BF16),
                   jax.ShapeDtypeStruct(w_cast.shape, BF16)],
        scratch_shapes=[pltpu.VMEM((RET_CHUNK, d), F32)] * 3 + [pltpu.VMEM((S, d), BF16)] * 3 + [
            pltpu.VMEM((S // RET_CHUNK, d, d), F32), pltpu.VMEM((S // RET_CHUNK, d, d), BF16)],
        compiler_params=_params(("arbitrary", "arbitrary")),
        name="retention",
    )(log_gamma, proj3, proj3, proj3, proj3, cos_t, sin_t, ret_norm_g, w_cast)


def _rglru_body(u_ref, z_ref, cw_ref, cb_ref, wg_ref, brg_ref, big_ref, lam_ref, gn_ref, wsrc_ref,
                o_ref, wdst_ref, uf_scr):
    wdst_ref[...] = wsrc_ref[...].astype(wdst_ref.dtype)
    W = LRU_BLOCK_DIM
    R = LRU_ROWS
    row_in_tile = lax.broadcasted_iota(I32, (R, W), 0) & (SUBLANES - 1)
    valids = [row_in_tile >= sh for sh in (1, 2, 4)]
    uf_scr[:SUBLANES, :] = jnp.zeros((SUBLANES, W), F32)
    for nb in range(LRU_BLOCKS_PER_STEP):
        _rglru_block(u_ref, z_ref, cw_ref, cb_ref, wg_ref.at[nb], brg_ref, big_ref, lam_ref, gn_ref,
                     o_ref, uf_scr, slice(nb * W, (nb + 1) * W), valids)


def _rglru_block(u_ref, z_ref, cw_ref, cb_ref, wg_ref, brg_ref, big_ref, lam_ref, gn_ref, o_ref,
                 uf_scr, lanes, valids):
    S = u_ref.shape[0]
    W = LRU_BLOCK_DIM
    R = LRU_ROWS
    K = CONV_WIDTH
    shifts = (1, 2, 4)
    nl = -lam_ref[:, lanes]
    softplus = jnp.maximum(nl, 0.0) + jnp.log1p(jnp.exp(-jnp.abs(nl)))
    coef = -LRU_C * LOG2_E * softplus
    cw = cw_ref[:, lanes]
    cb = cb_ref[:, lanes]
    brg = -LOG2_E * brg_ref[:, lanes]
    big = -LOG2_E * big_ref[:, lanes]
    gn = gn_ref[:, lanes]
    wg = wg_ref[...]
    carry = jnp.zeros((1, W), F32)

    for c in range(S // R):
        base = SUBLANES + c * R
        u = u_ref[c * R:(c + 1) * R, lanes].astype(F32)
        uf_scr[base:base + R, :] = u
        uc = cb + cw[K - 1:K, :] * u
        for j in range(1, K):
            uc = uc + cw[K - 1 - j:K - j, :] * uf_scr[base - j:base - j + R, :]
        gates = jnp.dot(uc.astype(BF16), wg, preferred_element_type=F32)
        r = 1.0 / (1.0 + jnp.exp2(gates[:, :W] + brg))
        i = 1.0 / (1.0 + jnp.exp2(gates[:, W:] + big))
        a = jnp.exp2(coef * r)
        t = 1.0 - a * a
        b = jnp.where(t > 0.0, t * lax.rsqrt(t), 0.0) * (i * uc)
        tiled = (R // SUBLANES, SUBLANES, W)
        for sh, valid in zip(shifts, valids):
            a_s = pltpu.roll(a.reshape(tiled), sh, 1).reshape(R, W)
            b_s = pltpu.roll(b.reshape(tiled), sh, 1).reshape(R, W)
            b = jnp.where(valid, a * b_s + b, b)
            a = jnp.where(valid, a * a_s, a)
        tiles = []
        for k in range(R // SUBLANES):
            rows = slice(k * SUBLANES, (k + 1) * SUBLANES)
            h_tile = a[rows, :] * carry + b[rows, :]
            carry = h_tile[SUBLANES - 1:SUBLANES, :]
            tiles.append(h_tile)
        h = jnp.concatenate(tiles, axis=0)
        ms = jnp.mean(h * h, axis=-1, keepdims=True)
        hl = h * lax.rsqrt(ms + EPS) * gn
        z = z_ref[c * R:(c + 1) * R, lanes].astype(F32)
        c0 = math.sqrt(2.0 / math.pi)
        gelu = 0.5 * z * (1.0 + jnp.tanh(z * (c0 + (c0 * 0.044715) * (z * z))))
        o_ref[c * R:(c + 1) * R, lanes] = (hl * gelu).astype(o_ref.dtype)


def _rglru(proj3, conv_w, conv_b, w_gates, b_rg, b_ig, lam, lru_norm_g, w_cast):
    B, S, _ = proj3.shape
    NB = LRU_BLOCKS
    W = LRU_BLOCK_DIM
    bp = LRU_BLOCKS_PER_STEP
    ng = NB // bp
    u_off = 4 * RET_HEADS * HEAD_DIM // (bp * W)
    z_off = u_off + ng
    vec = pl.BlockSpec((1, bp * W), lambda b, n: (0, n))
    wshape, wmap = _cast_plan(w_cast, B * ng)
    wspec = pl.BlockSpec(wshape, lambda b, n: wmap(b * ng + n))
    return pl.pallas_call(
        _rglru_body,
        grid=(B, ng),
        in_specs=[
            pl.BlockSpec((None, S, bp * W), lambda b, n: (b, 0, u_off + n)),
            pl.BlockSpec((None, S, bp * W), lambda b, n: (b, 0, z_off + n)),
            pl.BlockSpec((CONV_WIDTH, bp * W), lambda b, n: (0, n)),
            vec,
            pl.BlockSpec((bp, W, 2 * W), lambda b, n: (n, 0, 0)),
            vec, vec, vec, vec,
            wspec,
        ],
        out_specs=[pl.BlockSpec((None, S, bp * W), lambda b, n: (b, 0, n)), wspec],
        out_shape=[jax.ShapeDtypeStruct((B, S, NB * W), BF16),
                   jax.ShapeDtypeStruct(w_cast.shape, BF16)],
        scratch_shapes=[pltpu.VMEM((SUBLANES + S, W), F32)],
        compiler_params=_params(("arbitrary", "arbitrary")),
        name="rglru",
    )(proj3, proj3, conv_w, conv_b, w_gates, b_rg, b_ig, lam, lru_norm_g, w_cast)


def _rows_to_tokens(x2d):
    n = x2d.shape[1] // LANES
    parts = [x2d[:, s * LANES:(s + 1) * LANES] for s in range(n)]
    return jnp.swapaxes(jnp.stack(parts, axis=0), 0, 1)


def _tokens_to_rows(x3d):
    xt = jnp.swapaxes(x3d, 0, 1)
    return jnp.concatenate([xt[s] for s in range(x3d.shape[1])], axis=-1)


def _first_row_of_max(v, row):
    m = jnp.max(v, axis=0, keepdims=True)
    idx = jnp.min(jnp.where(v == m, row, v.shape[0]), axis=0, keepdims=True)
    return m, idx


def _outproj_body(ret_ref, lru_ref, x_ref, wo_ref, g_ref, wr_ref, br_ref, x1_ref, h2_ref, lg_ref):
    R = ret_ref.shape[1]
    acc = jnp.dot(ret_ref[...], wo_ref[:R, :], preferred_element_type=F32)
    acc = acc + jnp.dot(lru_ref[...], wo_ref[R:, :], preferred_element_type=F32)
    x1 = x_ref[...] + acc
    x1_ref[...] = x1
    ms = jnp.mean(x1 * x1, axis=-1, keepdims=True)
    h2 = x1 * lax.rsqrt(ms + EPS) * g_ref[...]
    h2_ref[...] = _rows_to_tokens(h2).astype(h2_ref.dtype)
    h_hi = h2.astype(BF16)
    h_lo = (h2 - h_hi.astype(F32)).astype(BF16)
    NR = lg_ref.shape[0]
    both = jnp.dot(h_hi, wr_ref[...], preferred_element_type=F32)
    lg = both[:, :NR] + jnp.dot(h_lo, wr_ref[:, :NR], preferred_element_type=F32)
    lg_ref[...] = (lg + both[:, NR:] + br_ref[...]).T


def _outproj(ret2d, lru2d, x2d, wo_bf16, g, wr_hi_lo, br, tm):
    T, D = x2d.shape
    R = ret2d.shape[1]
    L = lru2d.shape[1]
    NR = br.shape[1]
    const = lambda shape: pl.BlockSpec(shape, lambda i: (0, 0))
    return pl.pallas_call(
        _outproj_body,
        grid=(T // tm,),
        in_specs=[
            pl.BlockSpec((tm, R), lambda i: (i, 0)),
            pl.BlockSpec((tm, L), lambda i: (i, 0)),
            pl.BlockSpec((tm, D), lambda i: (i, 0)),
            const((R + L, D)), const((1, D)), const((D, 2 * NR)), const((1, NR)),
        ],
        out_specs=[
            pl.BlockSpec((tm, D), lambda i: (i, 0)),
            pl.BlockSpec((tm, D // LANES, LANES), lambda i: (i, 0, 0)),
            pl.BlockSpec((NR, tm), lambda i: (0, i)),
        ],
        out_shape=[
            jax.ShapeDtypeStruct((T, D), F32),
            jax.ShapeDtypeStruct((T, D // LANES, LANES), BF16),
            jax.ShapeDtypeStruct((NR, T), F32),
        ],
        compiler_params=_params(("arbitrary",)),
        name="outproj",
    )(ret2d, lru2d, x2d, wo_bf16, g, wr_hi_lo, br)


def _route_body(lg_ref, route_ref, counts_ref, run_scr):
    G, EG = N_GROUPS, EXPERTS_PER_GROUP
    R = ROUTE_ROWS
    sub = ROUTE_SUB
    neg = -jnp.inf

    @pl.when(pl.program_id(0) == 0)
    def _():
        run_scr[...] = jnp.zeros_like(run_scr)

    row = lax.broadcasted_iota(I32, (R, sub), 0)
    r_i = lax.broadcasted_iota(I32, (sub, sub), 0)
    c_i = lax.broadcasted_iota(I32, (sub, sub), 1)
    before = (r_i < c_i).astype(BF16)
    ones = jnp.ones((sub, LANES), BF16)
    run = run_scr[...]
    for s in range(lg_ref.shape[1] // sub):
        cols = slice(s * sub, (s + 1) * sub)
        lg = lg_ref[0:R, cols]
        gl = jnp.where(row < G, lg, neg)
        g_max, g_idx = _first_row_of_max(gl, row)
        g_p = 1.0 / jnp.sum(jnp.exp(gl - g_max), axis=0, keepdims=True)
        lo = G + EG * g_idx
        el = jnp.where((row >= lo) & (row < lo + EG), lg, neg)
        t1, i1 = _first_row_of_max(el, row)
        el2 = jnp.where(row == i1, neg, el)
        t2, i2 = _first_row_of_max(el2, row)
        p2 = jnp.exp(t2 - t1)
        w1 = g_p / (1.0 + p2)
        w2 = g_p * p2 / (1.0 + p2)
        oh1 = row == i1
        oh2 = row == i2
        oh = (oh1 | oh2).astype(BF16)
        prefix = jnp.dot(oh, before, preferred_element_type=F32) + jnp.tile(run, (1, sub // LANES))
        rank1 = jnp.sum(jnp.where(oh1, prefix, 0.0), axis=0, keepdims=True)
        rank2 = jnp.sum(jnp.where(oh2, prefix, 0.0), axis=0, keepdims=True)
        run = run + jnp.dot(oh, ones, preferred_element_type=F32)
        zero = jnp.zeros((1, sub), F32)
        route_ref[:, cols] = jnp.concatenate(
            [(i1 - G).astype(F32), (i2 - G).astype(F32), w1, w2, rank1, rank2, zero, zero], axis=0)
    run_scr[...] = run
    counts_ref[...] = run


def _route(logits_t, tm):
    NR, T = logits_t.shape
    R = ROUTE_ROWS
    return pl.pallas_call(
        _route_body,
        grid=(T // tm,),
        in_specs=[pl.BlockSpec((NR, tm), lambda i: (0, i))],
        out_specs=[pl.BlockSpec((SUBLANES, tm), lambda i: (0, i)),
                   pl.BlockSpec((R, LANES), lambda i: (0, 0))],
        out_shape=[jax.ShapeDtypeStruct((SUBLANES, T), F32), jax.ShapeDtypeStruct((R, LANES), F32)],
        scratch_shapes=[pltpu.VMEM((R, LANES), F32)],
        compiler_params=_params(("arbitrary",)),
        name="route",
    )(logits_t)


def _start_row_gather(src_hbm, idx_ref, dst, sem, lo, hi):
    for r in range(lo, hi):
        pltpu.make_async_copy(src_hbm.at[pl.ds(idx_ref[0, r], 1)], dst.at[pl.ds(r, 1)], sem).start(
            priority=r % 2)


def _zero_after(tile, n_rows, n_cols):
    bits = pltpu.bitcast(tile, jnp.uint32)
    zero = pltpu.bitcast((bits >> 16) >> 16, F32)
    return jnp.tile(zero, (n_rows // SUBLANES, n_cols // LANES))


def _pack_bf16_pair(lo, hi):
    lo_bits = pltpu.bitcast(lo.astype(BF16).astype(F32), jnp.uint32) >> 16
    hi_bits = pltpu.bitcast(hi.astype(BF16).astype(F32), jnp.uint32) & jnp.uint32(0xFFFF0000)
    return lo_bits | hi_bits


def _unpack_bf16_pair(words):
    lo = pltpu.bitcast(words << 16, F32)
    hi = pltpu.bitcast(words & jnp.uint32(0xFFFF0000), F32)
    return lo, hi


def _wait_rows(dst, sem):
    pltpu.make_async_copy(dst, dst, sem).wait()


def _moe_body(be_ref, nu_ref, tok0_ref, tok1_ref, tok2_ref, tokn_ref, h2_hbm, wg_ref, wu_ref, wd_ref,
              y_ref, rowbuf, xb, sem):
    del be_ref
    i = pl.program_id(0)
    last = nu_ref[0] - 1
    rows = rowbuf.shape[1]

    @pl.when(i > last)
    def _():
        y_ref[...] = jnp.zeros_like(y_ref)

    @pl.when(i <= last)
    def _():
        @pl.when(i == 0)
        def _():
            for b, tok_ref in enumerate((tok0_ref, tok1_ref, tok2_ref)):
                _start_row_gather(h2_hbm, tok_ref, rowbuf.at[b], sem.at[b], 0, rows)
            _wait_rows(rowbuf.at[0], sem.at[0])
            xb[0] = _tokens_to_rows(rowbuf[0].astype(F32)).astype(BF16)

        s1 = lax.rem(i + 1, MOE_SLOTS)
        s2 = lax.rem(i + 2, MOE_SLOTS)
        s3 = lax.rem(i + 3, MOE_SLOTS)
        _wait_rows(rowbuf.at[s1], sem.at[s1])
        xcur = xb.at[lax.rem(i, 2)]
        xnext = _tokens_to_rows(rowbuf[s1].astype(F32)).astype(BF16)

        DE = wg_ref.shape[1]
        D = wd_ref.shape[1]
        n_up, n_down = DE // MOE_UP_COLS, D // MOE_DOWN_COLS
        per = rows // (n_up + n_down)
        ready, nxt, nsem = rowbuf.at[s1], rowbuf.at[s3], sem.at[s3]
        hparts = []
        for c in range(n_up):
            cs = slice(c * MOE_UP_COLS, (c + 1) * MOE_UP_COLS)
            _start_row_gather(h2_hbm, tokn_ref, nxt, nsem, c * per, (c + 1) * per)
            zero = _zero_after(ready[0], rows, MOE_UP_COLS)
            gate = jnp.dot(xcur[...], wg_ref[:, cs], preferred_element_type=F32)
            up = jnp.dot(xcur[...], wu_ref[:, cs], preferred_element_type=F32) + zero
            hparts.append(((gate / (1.0 + jnp.exp(-gate))) * up).astype(BF16))
        hmid = jnp.concatenate(hparts, axis=-1)
        xb[lax.rem(i + 1, 2)] = xnext
        yparts = []
        for c in range(n_down):
            cs = slice(c * MOE_DOWN_COLS, (c + 1) * MOE_DOWN_COLS)
            g = n_up + c
            _start_row_gather(h2_hbm, tokn_ref, nxt, nsem, g * per,
                              rows if c == n_down - 1 else (g + 1) * per)
            zero = _zero_after(ready[0], rows, MOE_DOWN_COLS)
            yparts.append(jnp.dot(hmid, wd_ref[:, cs], preferred_element_type=F32) + zero)
        for c in range(n_down // 2):
            cs = slice(c * MOE_DOWN_COLS, (c + 1) * MOE_DOWN_COLS)
            y_ref[:, cs] = _pack_bf16_pair(yparts[c], yparts[c + n_down // 2])

        @pl.when(i == last)
        def _():
            _wait_rows(rowbuf.at[s2], sem.at[s2])
            _wait_rows(nxt, nsem)


def _moe(block_e, n_used, tok3, h2, wg_bf16, wu_bf16, wd_bf16):
    NB, _, rows = tok3.shape
    T, n_tiles, _ = h2.shape
    E, D, DE = wg_bf16.shape
    tok_spec = lambda blk: pl.BlockSpec((None, 1, rows), lambda i, be, nu: (blk(i, nu[0] - 1), 0, 0),
                                        memory_space=pltpu.SMEM)
    w_spec = lambda shape: pl.BlockSpec(shape, lambda i, be, nu: (be[i], 0, 0))
    grid_spec = pltpu.PrefetchScalarGridSpec(
        num_scalar_prefetch=2,
        grid=(NB,),
        in_specs=[
            tok_spec(lambda i, last: 0),
            tok_spec(lambda i, last: jnp.minimum(1, last)),
            tok_spec(lambda i, last: jnp.minimum(2, last)),
            tok_spec(lambda i, last: jnp.minimum(i + 3, last)),
            pl.BlockSpec(memory_space=pl.ANY),
            w_spec((None, D, DE)), w_spec((None, D, DE)), w_spec((None, DE, D)),
        ],
        out_specs=pl.BlockSpec((rows, D // 2), lambda i, be, nu: (i, 0)),
        scratch_shapes=[pltpu.VMEM((MOE_SLOTS, rows, n_tiles, LANES), h2.dtype),
                        pltpu.VMEM((2, rows, D), BF16),
                        pltpu.SemaphoreType.DMA((MOE_SLOTS,))],
    )
    return pl.pallas_call(
        _moe_body,
        grid_spec=grid_spec,
        out_shape=jax.ShapeDtypeStruct((NB * rows, D // 2), jnp.uint32),
        compiler_params=_params(("arbitrary",)),
        name="moe",
    )(block_e, n_used, tok3, tok3, tok3, tok3, h2, wg_bf16, wu_bf16, wd_bf16)


def _combine_body(d0_ref, d1_ref, dn_ref, y_hbm, x1_ref, w_ref, g_ref, o_ref, ybuf, sem):
    i = pl.program_id(0)
    n = pl.num_programs(0)
    slot = lax.rem(i, COMBINE_SLOTS)
    slot1 = lax.rem(i + 1, COMBINE_SLOTS)
    slot2 = lax.rem(i + 2, COMBINE_SLOTS)
    tm = x1_ref.shape[0]

    @pl.when(i == 0)
    def _():
        _start_row_gather(y_hbm, d0_ref, ybuf.at[0], sem.at[0], 0, 2 * tm)
        _start_row_gather(y_hbm, d1_ref, ybuf.at[1], sem.at[1], 0, 2 * tm)

    _wait_rows(ybuf.at[slot], sem.at[slot])
    g = g_ref[...]
    yb, nxt, nsem = ybuf.at[slot], ybuf.at[slot2], sem.at[slot2]
    rc = tm // COMBINE_CHUNKS
    per = 2 * tm // COMBINE_CHUNKS
    for c in range(COMBINE_CHUNKS):
        rows = slice(c * rc, (c + 1) * rc)
        w = w_ref[rows, :]
        y0 = jnp.concatenate(_unpack_bf16_pair(yb[c * rc:(c + 1) * rc, :]), axis=-1)
        y1 = jnp.concatenate(_unpack_bf16_pair(yb[tm + c * rc:tm + (c + 1) * rc, :]), axis=-1)
        x = x1_ref[rows, :] + (w[:, 0:1] * y0 + w[:, 1:2] * y1)
        ms = jnp.mean(x * x, axis=-1, keepdims=True)
        o_ref[rows, :] = x * lax.rsqrt(ms + EPS) * g
        _start_row_gather(y_hbm, dn_ref, nxt, nsem, c * per, (c + 1) * per)

    @pl.when(i == n - 1)
    def _():
        _wait_rows(ybuf.at[slot1], sem.at[slot1])
        _wait_rows(nxt, nsem)


def _combine(dest3, y, x1, e_w, g, tm):
    T, D = x1.shape
    NT = T // tm
    return pl.pallas_call(
        _combine_body,
        grid=(NT,),
        in_specs=[
            pl.BlockSpec((None, 1, 2 * tm), lambda i: (0, 0, 0), memory_space=pltpu.SMEM),
            pl.BlockSpec((None, 1, 2 * tm), lambda i: (min(1, NT - 1), 0, 0), memory_space=pltpu.SMEM),
            pl.BlockSpec((None, 1, 2 * tm), lambda i: (jnp.minimum(i + 2, NT - 1), 0, 0),
                         memory_space=pltpu.SMEM),
            pl.BlockSpec(memory_space=pl.ANY),
            pl.BlockSpec((tm, D), lambda i: (i, 0)),
            pl.BlockSpec((tm, TOP_K), lambda i: (i, 0)),
            pl.BlockSpec((1, D), lambda i: (0, 0)),
        ],
        out_specs=pl.BlockSpec((tm, D), lambda i: (i, 0)),
        out_shape=jax.ShapeDtypeStruct((T, D), F32),
        scratch_shapes=[pltpu.VMEM((COMBINE_SLOTS, 2 * tm, y.shape[1]), y.dtype),
                        pltpu.SemaphoreType.DMA((COMBINE_SLOTS,))],
        compiler_params=_params(("arbitrary",)),
        name="combine",
    )(dest3, dest3, dest3, y, x1, e_w, g)


def _invert_body(lo_ref, hi_ref, dest_ref, wsrc_ref, tok_ref, wdst_ref):
    wdst_ref[...] = wsrc_ref[...].astype(wdst_ref.dtype)
    i = pl.program_id(0)
    blk = dest_ref.shape[1]

    @pl.when(i == 0)
    def _():
        def clear(j, carry):
            tok_ref[j] = 0
            return carry

        for s in range(lo_ref.shape[0]):
            lax.fori_loop(lo_ref[s], hi_ref[s], clear, 0)

    shift = TOP_K.bit_length() - 1
    n = INVERT_UNROLL

    def place(c, carry):
        j0 = pl.multiple_of(c * n, n)
        tok0 = lax.shift_right_logical(i * blk + j0, shift)
        for u in range(n):
            tok_ref[dest_ref[0, j0 + u]] = tok0 + (u >> shift)
        return carry

    lax.fori_loop(0, blk // n, place, 0)


def _invert(hole_lo, hole_hi, dest_flat, n_rows, w_cast):
    assert TOP_K & (TOP_K - 1) == 0
    A = dest_flat.shape[0]
    blk = min(INVERT_BLOCK, A)
    wshape, wmap = _cast_plan(w_cast, A // blk)
    wspec = pl.BlockSpec(wshape, lambda i, lo, hi: wmap(i))
    grid_spec = pltpu.PrefetchScalarGridSpec(
        num_scalar_prefetch=2,
        grid=(A // blk,),
        in_specs=[pl.BlockSpec((None, 1, blk), lambda i, lo, hi: (i, 0, 0), memory_space=pltpu.SMEM),
                  wspec],
        out_specs=[pl.BlockSpec(memory_space=pltpu.SMEM), wspec],
    )
    return pl.pallas_call(
        _invert_body,
        grid_spec=grid_spec,
        out_shape=[jax.ShapeDtypeStruct((n_rows,), I32), jax.ShapeDtypeStruct(w_cast.shape, BF16)],
        compiler_params=_params(("arbitrary",)),
        name="invert",
    )(hole_lo, hole_hi, dest_flat.reshape(A // blk, 1, blk), w_cast)


def _layout(route, counts, rows, w_cast):
    T = route.shape[1]
    A = T * TOP_K
    e_id = route[0:TOP_K].T.astype(I32)
    e_w = route[TOP_K:2 * TOP_K].T
    rank = route[2 * TOP_K:3 * TOP_K].T.astype(I32)
    counts = counts[N_GROUPS:N_GROUPS + N_EXPERTS, 0].astype(I32)
    padded = (counts + rows - 1) // rows * rows
    pad_end = jnp.cumsum(padded)
    pad_start = pad_end - padded
    experts = jnp.arange(N_EXPERTS, dtype=I32)
    start_of = jnp.sum(jnp.where(e_id[:, :, None] == experts, pad_start, 0), axis=-1)
    dest = start_of + rank
    n_blocks = (A + N_EXPERTS * (rows - 1)) // rows
    P = n_blocks * rows
    hole_lo = jnp.concatenate([pad_start + counts, pad_end[-1:]])
    hole_hi = jnp.concatenate([pad_end, jnp.full((1,), P, I32)])
    tok, w_cast_bf16 = _invert(hole_lo, hole_hi, dest.reshape(A), P, w_cast)
    n_used = pad_end[-1:] // rows
    block_start = jnp.minimum(jnp.arange(n_blocks, dtype=I32), n_used - 1) * rows
    block_e = jnp.sum((pad_end[None, :] <= block_start[:, None]).astype(I32), axis=1)
    return e_w, dest, tok, block_e, n_used, n_blocks, w_cast_bf16


def kernel(x, norm_mix_g, w_in, ret_norm_g, conv_w, conv_b, w_rg, b_rg, w_ig, b_ig, lru_lambda,
           lru_norm_g, w_out, norm_ffn_g, w_group, b_group, w_router, b_router, w_gate, w_up,
           w_down, norm_final_g):
    B, S, D = x.shape
    T = B * S
    depth = norm_mix_g.shape[0]
    assert depth == 1, "the combine kernel fuses the final norm, so only one layer is supported"
    H, d = RET_HEADS, HEAD_DIM

    half = d // 2
    inv = ROPE_BASE ** (-jnp.arange(half, dtype=F32) / half)
    ang = jnp.arange(S, dtype=F32)[:, None] * inv[None, :]
    cos_t = jnp.concatenate([jnp.cos(ang), jnp.cos(ang)], axis=-1)
    sin_t = jnp.concatenate([-jnp.sin(ang), jnp.sin(ang)], axis=-1)
    log_gamma = jnp.log1p(-(2.0 ** (-5.0 - jnp.arange(H, dtype=F32))))

    x2d = x.reshape(T, D)
    for l in range(depth):
        proj = _inproj(x2d, norm_mix_g[l][None, :], w_in[l].astype(BF16), INPROJ_TM, INPROJ_TN)
        proj3 = proj.reshape(B, S, proj.shape[1])
        ret, wg_bf16 = _retention(proj3, log_gamma, cos_t, sin_t, ret_norm_g[l][None, :], w_gate[l])
        w_gates = (-LOG2_E * jnp.concatenate([w_rg[l], w_ig[l]], axis=-1)).astype(BF16)
        lru, wu_bf16 = _rglru(proj3, conv_w[l], conv_b[l][None, :], w_gates, b_rg[l][None, :],
                              b_ig[l][None, :], lru_lambda[l][None, :], lru_norm_g[l][None, :],
                              w_up[l])

        n_route = N_GROUPS + N_EXPERTS
        wr = jnp.concatenate([w_group[l], w_router[l], jnp.zeros((D, LANES - n_route), F32)], axis=-1)
        br = jnp.concatenate([b_group[l], b_router[l], jnp.zeros((LANES - n_route,), F32)])[None, :]
        wr_hi = wr.astype(BF16)
        wr_lo = (wr - wr_hi.astype(F32)).astype(BF16)
        x1, h2, logits_t = _outproj(ret.reshape(T, -1), lru.reshape(T, -1), x2d, w_out[l].astype(BF16),
                                    norm_ffn_g[l][None, :], jnp.concatenate([wr_hi, wr_lo], axis=1),
                                    br, OUTPROJ_TM)
        route, counts = _route(logits_t, ROUTE_TM)

        e_w, dest, tok, block_e, n_used, n_blocks, wd_bf16 = _layout(route, counts, MOE_ROWS,
                                                                     w_down[l])
        y = _moe(block_e, n_used, tok.reshape(n_blocks, 1, MOE_ROWS), h2, wg_bf16, wu_bf16, wd_bf16)
        dest3 = dest.reshape(T // COMBINE_TM, COMBINE_TM, TOP_K).transpose(0, 2, 1).reshape(
            T // COMBINE_TM, 1, TOP_K * COMBINE_TM)
        x2d = _combine(dest3, y, x1, e_w, norm_final_g[None, :], COMBINE_TM)
    return x2d.reshape(B, S, D)
```

```python
import functools
import math

import jax
import jax.numpy as jnp
from jax import lax
from jax.experimental import pallas as pl
from jax.experimental.pallas import tpu as pltpu

F32 = jnp.float32
BF16 = jnp.bfloat16
I32 = jnp.int32

EPS = 1e-6
LOG2_E = 1.4426950408889634
RET_HEADS = 8
HEAD_DIM = 128
RET_CHUNK = 128
ROPE_BASE = 10000.0
LRU_BLOCKS = 8
LRU_BLOCK_DIM = 128
CONV_WIDTH = 4
LRU_C = 8.0
N_GROUPS = 4
EXPERTS_PER_GROUP = 8
N_EXPERTS = N_GROUPS * EXPERTS_PER_GROUP
TOP_K = 2

LANES = 128
SUBLANES = 8
VMEM_LIMIT = 56 * 1024 * 1024

INPROJ_TM = 1024
INPROJ_TN = 2048
RET_HEADS_PER_STEP = 4
LRU_ROWS = 128
LRU_BLOCKS_PER_STEP = 4
OUTPROJ_TM = 512
ROUTE_TM = 2048
ROUTE_SUB = 256
ROUTE_ROWS = 40
INVERT_BLOCK = 2048
INVERT_UNROLL = 32
MOE_ROWS = 256
MOE_SLOTS = 4
MOE_UP_COLS = 256
MOE_DOWN_COLS = 512
COMBINE_TM = 256
COMBINE_SLOTS = 3
COMBINE_CHUNKS = 8


def _params(sem):
    return pltpu.CompilerParams(dimension_semantics=sem, vmem_limit_bytes=VMEM_LIMIT)


def _inproj_body(x_ref, g_ref, w_ref, o_ref, h_scr):
    @pl.when(pl.program_id(1) == 0)
    def _():
        x = x_ref[...]
        ms = jnp.mean(x * x, axis=-1, keepdims=True)
        h_scr[...] = (x * lax.rsqrt(ms + EPS) * g_ref[...]).astype(BF16)

    o_ref[...] = jnp.dot(h_scr[...], w_ref[...], preferred_element_type=F32).astype(o_ref.dtype)


def _inproj(x2d, g, w_bf16, tm, tn):
    T, D = x2d.shape
    N = w_bf16.shape[1]
    return pl.pallas_call(
        _inproj_body,
        grid=(T // tm, N // tn),
        in_specs=[
            pl.BlockSpec((tm, D), lambda i, j: (i, 0)),
            pl.BlockSpec((1, D), lambda i, j: (0, 0)),
            pl.BlockSpec((D, tn), lambda i, j: (0, j)),
        ],
        out_specs=pl.BlockSpec((tm, tn), lambda i, j: (i, j)),
        out_shape=jax.ShapeDtypeStruct((T, N), BF16),
        scratch_shapes=[pltpu.VMEM((tm, D), BF16)],
        compiler_params=_params(("arbitrary", "arbitrary")),
        name="inproj",
    )(x2d, g, w_bf16)


def _cast_plan(w, n_steps):
    E, R, C = w.shape
    if n_steps >= E:
        parts = n_steps // E
        assert n_steps == E * parts and R % parts == 0
        return (None, R // parts, C), (lambda s: (s // parts, s % parts, 0))
    per_step = E // n_steps
    assert E == per_step * n_steps
    return (per_step, R, C), (lambda s: (s, 0, 0))


def _retention_body(lg_ref, q_ref, k_ref, v_ref, g_ref, cos_ref, sin_ref, gn_ref, wsrc_ref, o_ref,
                    wdst_ref, mask_scr, qdec_scr, kdec_scr, qb_scr, kb_scr, qd_scr, kv_scr, sb_scr):
    wdst_ref[...] = wsrc_ref[...].astype(wdst_ref.dtype)
    C = RET_CHUNK
    d = HEAD_DIM
    S = q_ref.shape[0]
    row = lax.broadcasted_iota(I32, (C, d), 0).astype(F32)
    col = lax.broadcasted_iota(I32, (C, d), 1).astype(F32)
    rel = row - col
    scale = d ** -0.5
    n_chunks = S // C

    for hh in range(RET_HEADS_PER_STEP):
        hc = slice(hh * d, (hh + 1) * d)
        lg = lg_ref[pl.program_id(1) * RET_HEADS_PER_STEP + hh]
        mask_scr[...] = jnp.where(rel >= 0, jnp.exp(jnp.maximum(rel, 0.0) * lg), 0.0) * scale
        qdec_scr[...] = jnp.exp((row + 1.0) * lg) * scale
        kdec_scr[...] = jnp.exp((C - 1.0 - row) * lg)
        c_dec = jnp.exp(jnp.full((1, d), float(C), F32) * lg)
        gn = gn_ref[:, hc]

        for n in range(n_chunks):
            sl = pl.ds(n * C, C)
            cos = cos_ref[sl, :]
            sin = sin_ref[sl, :]
            q = q_ref[sl, hc].astype(F32)
            k = k_ref[sl, hc].astype(F32)
            q = q * cos + pltpu.roll(q, d // 2, 1) * sin
            k = k * cos + pltpu.roll(k, d // 2, 1) * sin
            qb_scr[sl, :] = q.astype(BF16)
            kb_scr[sl, :] = k.astype(BF16)
            qd_scr[sl, :] = (q * qdec_scr[...]).astype(BF16)
            kd_t = (k * kdec_scr[...]).T.astype(BF16)
            kv_scr[n] = jnp.dot(kd_t, v_ref[sl, hc], preferred_element_type=F32)
        state = jnp.zeros((d, d), F32)
        for n in range(n_chunks):
            sb_scr[n] = state.astype(BF16)
            state = c_dec * state + kv_scr[n]
        for n in range(n_chunks):
            sl = pl.ds(n * C, C)
            v = v_ref[sl, hc]
            scores = lax.dot_general(qb_scr[sl, :], kb_scr[sl, :], (((1,), (1,)), ((), ())),
                                     preferred_element_type=F32) * mask_scr[...]
            o = jnp.dot(scores.astype(BF16), v, preferred_element_type=F32)
            o = o + jnp.dot(qd_scr[sl, :], sb_scr[n], preferred_element_type=F32)
            mu = jnp.mean(o, axis=-1, keepdims=True)
            var = jnp.maximum(jnp.mean(o * o, axis=-1, keepdims=True) - mu * mu, 0.0)
            on = (o - mu) * lax.rsqrt(var + EPS) * gn
            g = g_ref[sl, hc].astype(F32)
            o_ref[sl, hc] = ((g / (1.0 + jnp.exp2(g * (-LOG2_E)))) * on).astype(o_ref.dtype)


def _retention(proj3, log_gamma, cos_t, sin_t, ret_norm_g, w_cast):
    B, S, _ = proj3.shape
    H = RET_HEADS
    d = HEAD_DIM
    hp = RET_HEADS_PER_STEP
    ng = H // hp
    blk = lambda off: pl.BlockSpec((None, S, hp * d), lambda b, h, off=off: (b, 0, off + h))
    wshape, wmap = _cast_plan(w_cast, B * ng)
    wspec = pl.BlockSpec(wshape, lambda b, h: wmap(b * ng + h))
    return pl.pallas_call(
        _retention_body,
        grid=(B, ng),
        in_specs=[
            pl.BlockSpec(memory_space=pltpu.SMEM),
            blk(0), blk(ng), blk(2 * ng), blk(3 * ng),
            pl.BlockSpec((S, d), lambda b, h: (0, 0)),
            pl.BlockSpec((S, d), lambda b, h: (0, 0)),
            pl.BlockSpec((1, hp * d), lambda b, h: (0, h)),
            wspec,
        ],
        out_specs=[pl.BlockSpec((None, S, hp * d), lambda b, h: (b, 0, h)), wspec],
        out_shape=[jax.ShapeDtypeStruct((B, S, H * d), BF16),
                   jax.ShapeDtypeStruct(w_cast.shape, BF16)],
        scratch_shapes=[pltpu.VMEM((RET_CHUNK, d), F32)] * 3 + [pltpu.VMEM((S, d), BF16)] * 3 + [
            pltpu.VMEM((S // RET_CHUNK, d, d), F32), pltpu.VMEM((S // RET_CHUNK, d, d), BF16)],
        compiler_params=_params(("arbitrary", "arbitrary")),
        name="retention",
    )(log_gamma, proj3, proj3, proj3, proj3, cos_t, sin_t, ret_norm_g, w_cast)


def _rglru_body(u_ref, z_ref, cw_ref, cb_ref, wg_ref, brg_ref, big_ref, lam_ref, gn_ref, wsrc_ref,
                o_ref, wdst_ref, uf_scr):
    wdst_ref[...] = wsrc_ref[...].astype(wdst_ref.dtype)
    W = LRU_BLOCK_DIM
    R = LRU_ROWS
    row_in_tile = lax.broadcasted_iota(I32, (R, W), 0) & (SUBLANES - 1)
    valids = [row_in_tile >= sh for sh in (1, 2, 4)]
    uf_scr[:SUBLANES, :] = jnp.zeros((SUBLANES, W), F32)
    for nb in range(LRU_BLOCKS_PER_STEP):
        _rglru_block(u_ref, z_ref, cw_ref, cb_ref, wg_ref.at[nb], brg_ref, big_ref, lam_ref, gn_ref,
                     o_ref, uf_scr, slice(nb * W, (nb + 1) * W), valids)


def _rglru_block(u_ref, z_ref, cw_ref, cb_ref, wg_ref, brg_ref, big_ref, lam_ref, gn_ref, o_ref,
                 uf_scr, lanes, valids):
    S = u_ref.shape[0]
    W = LRU_BLOCK_DIM
    R = LRU_ROWS
    K = CONV_WIDTH
    shifts = (1, 2, 4)
    nl = -lam_ref[:, lanes]
    softplus = jnp.maximum(nl, 0.0) + jnp.log1p(jnp.exp(-jnp.abs(nl)))
    coef = -LRU_C * LOG2_E * softplus
    cw = cw_ref[:, lanes]
    cb = cb_ref[:, lanes]
    brg = -LOG2_E * brg_ref[:, lanes]
    big = -LOG2_E * big_ref[:, lanes]
    gn = gn_ref[:, lanes]
    wg = wg_ref[...]
    carry = jnp.zeros((1, W), F32)

    for c in range(S // R):
        base = SUBLANES + c * R
        u = u_ref[c * R:(c + 1) * R, lanes].astype(F32)
        uf_scr[base:base + R, :] = u
        uc = cb + cw[K - 1:K, :] * u
        for j in range(1, K):
            uc = uc + cw[K - 1 - j:K - j, :] * uf_scr[base - j:base - j + R, :]
        gates = jnp.dot(uc.astype(BF16), wg, preferred_element_type=F32)
        r = 1.0 / (1.0 + jnp.exp2(gates[:, :W] + brg))
        i = 1.0 / (1.0 + jnp.exp2(gates[:, W:] + big))
        a = jnp.exp2(coef * r)
        t = 1.0 - a * a
        b = jnp.where(t > 0.0, t * lax.rsqrt(t), 0.0) * (i * uc)
        tiled = (R // SUBLANES, SUBLANES, W)
        for sh, valid in zip(shifts, valids):
            a_s = pltpu.roll(a.reshape(tiled), sh, 1).reshape(R, W)
            b_s = pltpu.roll(b.reshape(tiled), sh, 1).reshape(R, W)
            b = jnp.where(valid, a * b_s + b, b)
            a = jnp.where(valid, a * a_s, a)
        tiles = []
        for k in range(R // SUBLANES):
            rows = slice(k * SUBLANES, (k + 1) * SUBLANES)
            h_tile = a[rows, :] * carry + b[rows, :]
            carry = h_tile[SUBLANES - 1:SUBLANES, :]
            tiles.append(h_tile)
        h = jnp.concatenate(tiles, axis=0)
        ms = jnp.mean(h * h, axis=-1, keepdims=True)
        hl = h * lax.rsqrt(ms + EPS) * gn
        z = z_ref[c * R:(c + 1) * R, lanes].astype(F32)
        c0 = math.sqrt(2.0 / math.pi)
        gelu = 0.5 * z * (1.0 + jnp.tanh(z * (c0 + (c0 * 0.044715) * (z * z))))
        o_ref[c * R:(c + 1) * R, lanes] = (hl * gelu).astype(o_ref.dtype)


def _rglru(proj3, conv_w, conv_b, w_gates, b_rg, b_ig, lam, lru_norm_g, w_cast):
    B, S, _ = proj3.shape
    NB = LRU_BLOCKS
    W = LRU_BLOCK_DIM
    bp = LRU_BLOCKS_PER_STEP
    ng = NB // bp
    u_off = 4 * RET_HEADS * HEAD_DIM // (bp * W)
    z_off = u_off + ng
    vec = pl.BlockSpec((1, bp * W), lambda b, n: (0, n))
    wshape, wmap = _cast_plan(w_cast, B * ng)
    wspec = pl.BlockSpec(wshape, lambda b, n: wmap(b * ng + n))
    return pl.pallas_call(
        _rglru_body,
        grid=(B, ng),
        in_specs=[
            pl.BlockSpec((None, S, bp * W), lambda b, n: (b, 0, u_off + n)),
            pl.BlockSpec((None, S, bp * W), lambda b, n: (b, 0, z_off + n)),
            pl.BlockSpec((CONV_WIDTH, bp * W), lambda b, n: (0, n)),
            vec,
            pl.BlockSpec((bp, W, 2 * W), lambda b, n: (n, 0, 0)),
            vec, vec, vec, vec,
            wspec,
        ],
        out_specs=[pl.BlockSpec((None, S, bp * W), lambda b, n: (b, 0, n)), wspec],
        out_shape=[jax.ShapeDtypeStruct((B, S, NB * W), BF16),
                   jax.ShapeDtypeStruct(w_cast.shape, BF16)],
        scratch_shapes=[pltpu.VMEM((SUBLANES + S, W), F32)],
        compiler_params=_params(("arbitrary", "arbitrary")),
        name="rglru",
    )(proj3, proj3, conv_w, conv_b, w_gates, b_rg, b_ig, lam, lru_norm_g, w_cast)


def _rows_to_tokens(x2d):
    n = x2d.shape[1] // LANES
    parts = [x2d[:, s * LANES:(s + 1) * LANES] for s in range(n)]
    return jnp.swapaxes(jnp.stack(parts, axis=0), 0, 1)


def _tokens_to_rows(x3d):
    xt = jnp.swapaxes(x3d, 0, 1)
    return jnp.concatenate([xt[s] for s in range(x3d.shape[1])], axis=-1)


def _first_row_of_max(v, row):
    m = jnp.max(v, axis=0, keepdims=True)
    idx = jnp.min(jnp.where(v == m, row, v.shape[0]), axis=0, keepdims=True)
    return m, idx


def _outproj_body(ret_ref, lru_ref, x_ref, wo_ref, g_ref, wr_ref, br_ref, x1_ref, h2_ref, lg_ref):
    R = ret_ref.shape[1]
    acc = jnp.dot(ret_ref[...], wo_ref[:R, :], preferred_element_type=F32)
    acc = acc + jnp.dot(lru_ref[...], wo_ref[R:, :], preferred_element_type=F32)
    x1 = x_ref[...] + acc
    x1_ref[...] = x1
    ms = jnp.mean(x1 * x1, axis=-1, keepdims=True)
    h2 = x1 * lax.rsqrt(ms + EPS) * g_ref[...]
    h2_ref[...] = _rows_to_tokens(h2).astype(h2_ref.dtype)
    h_hi = h2.astype(BF16)
    h_lo = (h2 - h_hi.astype(F32)).astype(BF16)
    NR = lg_ref.shape[0]
    both = jnp.dot(h_hi, wr_ref[...], preferred_element_type=F32)
    lg = both[:, :NR] + jnp.dot(h_lo, wr_ref[:, :NR], preferred_element_type=F32)
    lg_ref[...] = (lg + both[:, NR:] + br_ref[...]).T


def _outproj(ret2d, lru2d, x2d, wo_bf16, g, wr_hi_lo, br, tm):
    T, D = x2d.shape
    R = ret2d.shape[1]
    L = lru2d.shape[1]
    NR = br.shape[1]
    const = lambda shape: pl.BlockSpec(shape, lambda i: (0, 0))
    return pl.pallas_call(
        _outproj_body,
        grid=(T // tm,),
        in_specs=[
            pl.BlockSpec((tm, R), lambda i: (i, 0)),
            pl.BlockSpec((tm, L), lambda i: (i, 0)),
            pl.BlockSpec((tm, D), lambda i: (i, 0)),
            const((R + L, D)), const((1, D)), const((D, 2 * NR)), const((1, NR)),
        ],
        out_specs=[
            pl.BlockSpec((tm, D), lambda i: (i, 0)),
            pl.BlockSpec((tm, D // LANES, LANES), lambda i: (i, 0, 0)),
            pl.BlockSpec((NR, tm), lambda i: (0, i)),
        ],
        out_shape=[
            jax.ShapeDtypeStruct((T, D), F32),
            jax.ShapeDtypeStruct((T, D // LANES, LANES), BF16),
            jax.ShapeDtypeStruct((NR, T), F32),
        ],
        compiler_params=_params(("arbitrary",)),
        name="outproj",
    )(ret2d, lru2d, x2d, wo_bf16, g, wr_hi_lo, br)


def _route_body(lg_ref, route_ref, counts_ref, run_scr):
    G, EG = N_GROUPS, EXPERTS_PER_GROUP
    R = ROUTE_ROWS
    sub = ROUTE_SUB
    neg = -jnp.inf

    @pl.when(pl.program_id(0) == 0)
    def _():
        run_scr[...] = jnp.zeros_like(run_scr)

    row = lax.broadcasted_iota(I32, (R, sub), 0)
    r_i = lax.broadcasted_iota(I32, (sub, sub), 0)
    c_i = lax.broadcasted_iota(I32, (sub, sub), 1)
    before = (r_i < c_i).astype(BF16)
    ones = jnp.ones((sub, LANES), BF16)
    run = run_scr[...]
    for s in range(lg_ref.shape[1] // sub):
        cols = slice(s * sub, (s + 1) * sub)
        lg = lg_ref[0:R, cols]
        gl = jnp.where(row < G, lg, neg)
        g_max, g_idx = _first_row_of_max(gl, row)
        g_p = 1.0 / jnp.sum(jnp.exp(gl - g_max), axis=0, keepdims=True)
        lo = G + EG * g_idx
        el = jnp.where((row >= lo) & (row < lo + EG), lg, neg)
        t1, i1 = _first_row_of_max(el, row)
        el2 = jnp.where(row == i1, neg, el)
        t2, i2 = _first_row_of_max(el2, row)
        p2 = jnp.exp(t2 - t1)
        w1 = g_p / (1.0 + p2)
        w2 = g_p * p2 / (1.0 + p2)
        oh1 = row == i1
        oh2 = row == i2
        oh = (oh1 | oh2).astype(BF16)
        prefix = jnp.dot(oh, before, preferred_element_type=F32) + jnp.tile(run, (1, sub // LANES))
        rank1 = jnp.sum(jnp.where(oh1, prefix, 0.0), axis=0, keepdims=True)
        rank2 = jnp.sum(jnp.where(oh2, prefix, 0.0), axis=0, keepdims=True)
        run = run + jnp.dot(oh, ones, preferred_element_type=F32)
        zero = jnp.zeros((1, sub), F32)
        route_ref[:, cols] = jnp.concatenate(
            [(i1 - G).astype(F32), (i2 - G).astype(F32), w1, w2, rank1, rank2, zero, zero], axis=0)
    run_scr[...] = run
    counts_ref[...] = run


def _route(logits_t, tm):
    NR, T = logits_t.shape
    R = ROUTE_ROWS
    return pl.pallas_call(
        _route_body,
        grid=(T // tm,),
        in_specs=[pl.BlockSpec((NR, tm), lambda i: (0, i))],
        out_specs=[pl.BlockSpec((SUBLANES, tm), lambda i: (0, i)),
                   pl.BlockSpec((R, LANES), lambda i: (0, 0))],
        out_shape=[jax.ShapeDtypeStruct((SUBLANES, T), F32), jax.ShapeDtypeStruct((R, LANES), F32)],
        scratch_shapes=[pltpu.VMEM((R, LANES), F32)],
        compiler_params=_params(("arbitrary",)),
        name="route",
    )(logits_t)


def _start_row_gather(src_hbm, idx_ref, dst, sem, lo, hi):
    for r in range(lo, hi):
        pltpu.make_async_copy(src_hbm.at[pl.ds(idx_ref[0, r], 1)], dst.at[pl.ds(r, 1)], sem).start(
            priority=r % 2)


def _zero_after(tile, n_rows, n_cols):
    bits = pltpu.bitcast(tile, jnp.uint32)
    zero = pltpu.bitcast((bits >> 16) >> 16, F32)
    return jnp.tile(zero, (n_rows // SUBLANES, n_cols // LANES))


def _pack_bf16_pair(lo, hi):
    lo_bits = pltpu.bitcast(lo.astype(BF16).astype(F32), jnp.uint32) >> 16
    hi_bits = pltpu.bitcast(hi.astype(BF16).astype(F32), jnp.uint32) & jnp.uint32(0xFFFF0000)
    return lo_bits | hi_bits


def _unpack_bf16_pair(words):
    lo = pltpu.bitcast(words << 16, F32)
    hi = pltpu.bitcast(words & jnp.uint32(0xFFFF0000), F32)
    return lo, hi


def _wait_rows(dst, sem):
    pltpu.make_async_copy(dst, dst, sem).wait()


def _moe_body(be_ref, nu_ref, tok0_ref, tok1_ref, tok2_ref, tokn_ref, h2_hbm, wg_ref, wu_ref, wd_ref,
              y_ref, rowbuf, xb, sem):
    del be_ref
    i = pl.program_id(0)
    last = nu_ref[0] - 1
    rows = rowbuf.shape[1]

    @pl.when(i > last)
    def _():
        y_ref[...] = jnp.zeros_like(y_ref)

    @pl.when(i <= last)
    def _():
        @pl.when(i == 0)
        def _():
            for b, tok_ref in enumerate((tok0_ref, tok1_ref, tok2_ref)):
                _start_row_gather(h2_hbm, tok_ref, rowbuf.at[b], sem.at[b], 0, rows)
            _wait_rows(rowbuf.at[0], sem.at[0])
            xb[0] = _tokens_to_rows(rowbuf[0].astype(F32)).astype(BF16)

        s1 = lax.rem(i + 1, MOE_SLOTS)
        s2 = lax.rem(i + 2, MOE_SLOTS)
        s3 = lax.rem(i + 3, MOE_SLOTS)
        _wait_rows(rowbuf.at[s1], sem.at[s1])
        xcur = xb.at[lax.rem(i, 2)]
        xnext = _tokens_to_rows(rowbuf[s1].astype(F32)).astype(BF16)

        DE = wg_ref.shape[1]
        D = wd_ref.shape[1]
        n_up, n_down = DE // MOE_UP_COLS, D // MOE_DOWN_COLS
        per = rows // (n_up + n_down)
        ready, nxt, nsem = rowbuf.at[s1], rowbuf.at[s3], sem.at[s3]
        hparts = []
        for c in range(n_up):
            cs = slice(c * MOE_UP_COLS, (c + 1) * MOE_UP_COLS)
            _start_row_gather(h2_hbm, tokn_ref, nxt, nsem, c * per, (c + 1) * per)
            zero = _zero_after(ready[0], rows, MOE_UP_COLS)
            gate = jnp.dot(xcur[...], wg_ref[:, cs], preferred_element_type=F32)
            up = jnp.dot(xcur[...], wu_ref[:, cs], preferred_element_type=F32) + zero
            hparts.append(((gate / (1.0 + jnp.exp(-gate))) * up).astype(BF16))
        hmid = jnp.concatenate(hparts, axis=-1)
        xb[lax.rem(i + 1, 2)] = xnext
        yparts = []
        for c in range(n_down):
            cs = slice(c * MOE_DOWN_COLS, (c + 1) * MOE_DOWN_COLS)
            g = n_up + c
            _start_row_gather(h2_hbm, tokn_ref, nxt, nsem, g * per,
                              rows if c == n_down - 1 else (g + 1) * per)
            zero = _zero_after(ready[0], rows, MOE_DOWN_COLS)
            yparts.append(jnp.dot(hmid, wd_ref[:, cs], preferred_element_type=F32) + zero)
        for c in range(n_down // 2):
            cs = slice(c * MOE_DOWN_COLS, (c + 1) * MOE_DOWN_COLS)
            y_ref[:, cs] = _pack_bf16_pair(yparts[c], yparts[c + n_down // 2])

        @pl.when(i == last)
        def _():
            _wait_rows(rowbuf.at[s2], sem.at[s2])
            _wait_rows(nxt, nsem)


def _moe(block_e, n_used, tok3, h2, wg_bf16, wu_bf16, wd_bf16):
    NB, _, rows = tok3.shape
    T, n_tiles, _ = h2.shape
    E, D, DE = wg_bf16.shape
    tok_spec = lambda blk: pl.BlockSpec((None, 1, rows), lambda i, be, nu: (blk(i, nu[0] - 1), 0, 0),
                                        memory_space=pltpu.SMEM)
    w_spec = lambda shape: pl.BlockSpec(shape, lambda i, be, nu: (be[i], 0, 0))
    grid_spec = pltpu.PrefetchScalarGridSpec(
        num_scalar_prefetch=2,
        grid=(NB,),
        in_specs=[
            tok_spec(lambda i, last: 0),
            tok_spec(lambda i, last: jnp.minimum(1, last)),
            tok_spec(lambda i, last: jnp.minimum(2, last)),
            tok_spec(lambda i, last: jnp.minimum(i + 3, last)),
            pl.BlockSpec(memory_space=pl.ANY),
            w_spec((None, D, DE)), w_spec((None, D, DE)), w_spec((None, DE, D)),
        ],
        out_specs=pl.BlockSpec((rows, D // 2), lambda i, be, nu: (i, 0)),
        scratch_shapes=[pltpu.VMEM((MOE_SLOTS, rows, n_tiles, LANES), h2.dtype),
                        pltpu.VMEM((2, rows, D), BF16),
                        pltpu.SemaphoreType.DMA((MOE_SLOTS,))],
    )
    return pl.pallas_call(
        _moe_body,
        grid_spec=grid_spec,
        out_shape=jax.ShapeDtypeStruct((NB * rows, D // 2), jnp.uint32),
        compiler_params=_params(("arbitrary",)),
        name="moe",
    )(block_e, n_used, tok3, tok3, tok3, tok3, h2, wg_bf16, wu_bf16, wd_bf16)


def _combine_body(d0_ref, d1_ref, dn_ref, y_hbm, x1_ref, w_ref, g_ref, o_ref, ybuf, sem):
    i = pl.program_id(0)
    n = pl.num_programs(0)
    slot = lax.rem(i, COMBINE_SLOTS)
    slot1 = lax.rem(i + 1, COMBINE_SLOTS)
    slot2 = lax.rem(i + 2, COMBINE_SLOTS)
    tm = x1_ref.shape[0]

    @pl.when(i == 0)
    def _():
        _start_row_gather(y_hbm, d0_ref, ybuf.at[0], sem.at[0], 0, 2 * tm)
        _start_row_gather(y_hbm, d1_ref, ybuf.at[1], sem.at[1], 0, 2 * tm)

    _wait_rows(ybuf.at[slot], sem.at[slot])
    g = g_ref[...]
    yb, nxt, nsem = ybuf.at[slot], ybuf.at[slot2], sem.at[slot2]
    rc = tm // COMBINE_CHUNKS
    per = 2 * tm // COMBINE_CHUNKS
    for c in range(COMBINE_CHUNKS):
        rows = slice(c * rc, (c + 1) * rc)
        w = w_ref[rows, :]
        y0 = jnp.concatenate(_unpack_bf16_pair(yb[c * rc:(c + 1) * rc, :]), axis=-1)
        y1 = jnp.concatenate(_unpack_bf16_pair(yb[tm + c * rc:tm + (c + 1) * rc, :]), axis=-1)
        x = x1_ref[rows, :] + (w[:, 0:1] * y0 + w[:, 1:2] * y1)
        ms = jnp.mean(x * x, axis=-1, keepdims=True)
        o_ref[rows, :] = x * lax.rsqrt(ms + EPS) * g
        _start_row_gather(y_hbm, dn_ref, nxt, nsem, c * per, (c + 1) * per)

    @pl.when(i == n - 1)
    def _():
        _wait_rows(ybuf.at[slot1], sem.at[slot1])
        _wait_rows(nxt, nsem)


def _combine(dest3, y, x1, e_w, g, tm):
    T, D = x1.shape
    NT = T // tm
    return pl.pallas_call(
        _combine_body,
        grid=(NT,),
        in_specs=[
            pl.BlockSpec((None, 1, 2 * tm), lambda i: (0, 0, 0), memory_space=pltpu.SMEM),
            pl.BlockSpec((None, 1, 2 * tm), lambda i: (min(1, NT - 1), 0, 0), memory_space=pltpu.SMEM),
            pl.BlockSpec((None, 1, 2 * tm), lambda i: (jnp.minimum(i + 2, NT - 1), 0, 0),
                         memory_space=pltpu.SMEM),
            pl.BlockSpec(memory_space=pl.ANY),
            pl.BlockSpec((tm, D), lambda i: (i, 0)),
            pl.BlockSpec((tm, TOP_K), lambda i: (i, 0)),
            pl.BlockSpec((1, D), lambda i: (0, 0)),
        ],
        out_specs=pl.BlockSpec((tm, D), lambda i: (i, 0)),
        out_shape=jax.ShapeDtypeStruct((T, D), F32),
        scratch_shapes=[pltpu.VMEM((COMBINE_SLOTS, 2 * tm, y.shape[1]), y.dtype),
                        pltpu.SemaphoreType.DMA((COMBINE_SLOTS,))],
        compiler_params=_params(("arbitrary",)),
        name="combine",
    )(dest3, dest3, dest3, y, x1, e_w, g)


def _invert_body(lo_ref, hi_ref, dest_ref, wsrc_ref, tok_ref, wdst_ref):
    wdst_ref[...] = wsrc_ref[...].astype(wdst_ref.dtype)
    i = pl.program_id(0)
    blk = dest_ref.shape[1]

    @pl.when(i == 0)
    def _():
        def clear(j, carry):
            tok_ref[j] = 0
            return carry

        for s in range(lo_ref.shape[0]):
            lax.fori_loop(lo_ref[s], hi_ref[s], clear, 0)

    shift = TOP_K.bit_length() - 1
    n = INVERT_UNROLL

    def place(c, carry):
        j0 = pl.multiple_of(c * n, n)
        tok0 = lax.shift_right_logical(i * blk + j0, shift)
        for u in range(n):
            tok_ref[dest_ref[0, j0 + u]] = tok0 + (u >> shift)
        return carry

    lax.fori_loop(0, blk // n, place, 0)


def _invert(hole_lo, hole_hi, dest_flat, n_rows, w_cast):
    assert TOP_K & (TOP_K - 1) == 0
    A = dest_flat.shape[0]
    blk = min(INVERT_BLOCK, A)
    wshape, wmap = _cast_plan(w_cast, A // blk)
    wspec = pl.BlockSpec(wshape, lambda i, lo, hi: wmap(i))
    grid_spec = pltpu.PrefetchScalarGridSpec(
        num_scalar_prefetch=2,
        grid=(A // blk,),
        in_specs=[pl.BlockSpec((None, 1, blk), lambda i, lo, hi: (i, 0, 0), memory_space=pltpu.SMEM),
                  wspec],
        out_specs=[pl.BlockSpec(memory_space=pltpu.SMEM), wspec],
    )
    return pl.pallas_call(
        _invert_body,
        grid_spec=grid_spec,
        out_shape=[jax.ShapeDtypeStruct((n_rows,), I32), jax.ShapeDtypeStruct(w_cast.shape, BF16)],
        compiler_params=_params(("arbitrary",)),
        name="invert",
    )(hole_lo, hole_hi, dest_flat.reshape(A // blk, 1, blk), w_cast)


def _layout(route, counts, rows, w_cast):
    T = route.shape[1]
    A = T * TOP_K
    e_id = route[0:TOP_K].T.astype(I32)
    e_w = route[TOP_K:2 * TOP_K].T
    rank = route[2 * TOP_K:3 * TOP_K].T.astype(I32)
    counts = counts[N_GROUPS:N_GROUPS + N_EXPERTS, 0].astype(I32)
    padded = (counts + rows - 1) // rows * rows
    pad_end = jnp.cumsum(padded)
    pad_start = pad_end - padded
    experts = jnp.arange(N_EXPERTS, dtype=I32)
    start_of = jnp.sum(jnp.where(e_id[:, :, None] == experts, pad_start, 0), axis=-1)
    dest = start_of + rank
    n_blocks = (A + N_EXPERTS * (rows - 1)) // rows
    P = n_blocks * rows
    hole_lo = jnp.concatenate([pad_start + counts, pad_end[-1:]])
    hole_hi = jnp.concatenate([pad_end, jnp.full((1,), P, I32)])
    tok, w_cast_bf16 = _invert(hole_lo, hole_hi, dest.reshape(A), P, w_cast)
    n_used = pad_end[-1:] // rows
    block_start = jnp.minimum(jnp.arange(n_blocks, dtype=I32), n_used - 1) * rows
    block_e = jnp.sum((pad_end[None, :] <= block_start[:, None]).astype(I32), axis=1)
    return e_w, dest, tok, block_e, n_used, n_blocks, w_cast_bf16


def kernel(x, norm_mix_g, w_in, ret_norm_g, conv_w, conv_b, w_rg, b_rg, w_ig, b_ig, lru_lambda,
           lru_norm_g, w_out, norm_ffn_g, w_group, b_group, w_router, b_router, w_gate, w_up,
           w_down, norm_final_g):
    B, S, D = x.shape
    T = B * S
    depth = norm_mix_g.shape[0]
    assert depth == 1, "the combine kernel fuses the final norm, so only one layer is supported"
    H, d = RET_HEADS, HEAD_DIM

    half = d // 2
    inv = ROPE_BASE ** (-jnp.arange(half, dtype=F32) / half)
    ang = jnp.arange(S, dtype=F32)[:, None] * inv[None, :]
    cos_t = jnp.concatenate([jnp.cos(ang), jnp.cos(ang)], axis=-1)
    sin_t = jnp.concatenate([-jnp.sin(ang), jnp.sin(ang)], axis=-1)
    log_gamma = jnp.log1p(-(2.0 ** (-5.0 - jnp.arange(H, dtype=F32))))

    x2d = x.reshape(T, D)
    for l in range(depth):
        proj = _inproj(x2d, norm_mix_g[l][None, :], w_in[l].astype(BF16), INPROJ_TM, INPROJ_TN)
        proj3 = proj.reshape(B, S, proj.shape[1])
        ret, wg_bf16 = _retention(proj3, log_gamma, cos_t, sin_t, ret_norm_g[l][None, :], w_gate[l])
        w_gates = (-LOG2_E * jnp.concatenate([w_rg[l], w_ig[l]], axis=-1)).astype(BF16)
        lru, wu_bf16 = _rglru(proj3, conv_w[l], conv_b[l][None, :], w_gates, b_rg[l][None, :],
                              b_ig[l][None, :], lru_lambda[l][None, :], lru_norm_g[l][None, :],
                              w_up[l])

        n_route = N_GROUPS + N_EXPERTS
        wr = jnp.concatenate([w_group[l], w_router[l], jnp.zeros((D, LANES - n_route), F32)], axis=-1)
        br = jnp.concatenate([b_group[l], b_router[l], jnp.zeros((LANES - n_route,), F32)])[None, :]
        wr_hi = wr.astype(BF16)
        wr_lo = (wr - wr_hi.astype(F32)).astype(BF16)
        x1, h2, logits_t = _outproj(ret.reshape(T, -1), lru.reshape(T, -1), x2d, w_out[l].astype(BF16),
                                    norm_ffn_g[l][None, :], jnp.concatenate([wr_hi, wr_lo], axis=1),
                                    br, OUTPROJ_TM)
        route, counts = _route(logits_t, ROUTE_TM)

        e_w, dest, tok, block_e, n_used, n_blocks, wd_bf16 = _layout(route, counts, MOE_ROWS,
                                                                     w_down[l])
        y = _moe(block_e, n_used, tok.reshape(n_blocks, 1, MOE_ROWS), h2, wg_bf16, wu_bf16, wd_bf16)
        dest3 = dest.reshape(T // COMBINE_TM, COMBINE_TM, TOP_K).transpose(0, 2, 1).reshape(
            T // COMBINE_TM, 1, TOP_K * COMBINE_TM)
        x2d = _combine(dest3, y, x1, e_w, norm_final_g[None, :], COMBINE_TM)
    return x2d.reshape(B, S, D)
```

```python
import math

import jax
import jax.numpy as jnp
from jax import lax
from jax.experimental import pallas as pl
from jax.experimental.pallas import tpu as pltpu

F32 = jnp.float32
BF16 = jnp.bfloat16
I32 = jnp.int32

EPS = 1e-6
LOG2_E = 1.4426950408889634
RET_HEADS = 8
HEAD_DIM = 128
RET_CHUNK = 128
ROPE_BASE = 10000.0
LRU_BLOCKS = 8
LRU_BLOCK_DIM = 128
CONV_WIDTH = 4
LRU_C = 8.0
N_GROUPS = 4
EXPERTS_PER_GROUP = 8
N_EXPERTS = N_GROUPS * EXPERTS_PER_GROUP
TOP_K = 2

LANES = 128
SUBLANES = 8
VMEM_LIMIT = 56 * 1024 * 1024

INPROJ_TM = 1024
INPROJ_TN = 2048
RET_HEADS_PER_STEP = 4
LRU_ROWS = 128
LRU_BLOCKS_PER_STEP = 2
OUTPROJ_TM = 512
ROUTE_TM = 2048
ROUTE_SUB = 256
ROUTE_ROWS = 40
INVERT_BLOCK = 2048
INVERT_UNROLL = 32
MOE_ROWS = 256
MOE_SLOTS = 4
MOE_UP_COLS = 256
MOE_DOWN_COLS = 512
COMBINE_TM = 256
COMBINE_SLOTS = 3
COMBINE_CHUNKS = 8


def _params(sem):
    return pltpu.CompilerParams(dimension_semantics=sem, vmem_limit_bytes=VMEM_LIMIT)


def _inproj_body(x_ref, g_ref, w_ref, o_ref, h_scr):
    @pl.when(pl.program_id(1) == 0)
    def _():
        x = x_ref[...]
        ms = jnp.mean(x * x, axis=-1, keepdims=True)
        h_scr[...] = (x * lax.rsqrt(ms + EPS) * g_ref[...]).astype(BF16)

    o_ref[...] = jnp.dot(h_scr[...], w_ref[...], preferred_element_type=F32).astype(o_ref.dtype)


def _inproj(x2d, g, w_bf16, tm, tn):
    T, D = x2d.shape
    N = w_bf16.shape[1]
    return pl.pallas_call(
        _inproj_body,
        grid=(T // tm, N // tn),
        in_specs=[
            pl.BlockSpec((tm, D), lambda i, j: (i, 0)),
            pl.BlockSpec((1, D), lambda i, j: (0, 0)),
            pl.BlockSpec((D, tn), lambda i, j: (0, j)),
        ],
        out_specs=pl.BlockSpec((tm, tn), lambda i, j: (i, j)),
        out_shape=jax.ShapeDtypeStruct((T, N), BF16),
        scratch_shapes=[pltpu.VMEM((tm, D), BF16)],
        compiler_params=_params(("arbitrary", "arbitrary")),
        name="inproj",
    )(x2d, g, w_bf16)


def _cast_plan(w, n_steps):
    E, R, C = w.shape
    if n_steps >= E:
        parts = n_steps // E
        assert n_steps == E * parts and R % parts == 0
        return (None, R // parts, C), (lambda s: (s // parts, s % parts, 0))
    per_step = E // n_steps
    assert E == per_step * n_steps
    return (per_step, R, C), (lambda s: (s, 0, 0))


def _retention_body(lg_ref, q_ref, k_ref, v_ref, g_ref, cos_ref, sin_ref, gn_ref, wsrc_ref, o_ref,
                    wdst_ref, mask_scr, qdec_scr, kdec_scr, qb_scr, kb_scr, qd_scr, kv_scr, sb_scr):
    wdst_ref[...] = wsrc_ref[...].astype(wdst_ref.dtype)
    C = RET_CHUNK
    d = HEAD_DIM
    S = q_ref.shape[0]
    row = lax.broadcasted_iota(I32, (C, d), 0).astype(F32)
    col = lax.broadcasted_iota(I32, (C, d), 1).astype(F32)
    rel = row - col
    scale = d ** -0.5
    n_chunks = S // C

    for hh in range(RET_HEADS_PER_STEP):
        hc = slice(hh * d, (hh + 1) * d)
        lg = lg_ref[pl.program_id(1) * RET_HEADS_PER_STEP + hh]
        mask_scr[...] = jnp.where(rel >= 0, jnp.exp(jnp.maximum(rel, 0.0) * lg), 0.0) * scale
        qdec_scr[...] = jnp.exp((row + 1.0) * lg) * scale
        kdec_scr[...] = jnp.exp((C - 1.0 - row) * lg)
        c_dec = jnp.exp(jnp.full((1, d), float(C), F32) * lg)
        gn = gn_ref[:, hc]

        for n in range(n_chunks):
            sl = pl.ds(n * C, C)
            cos = cos_ref[sl, :]
            sin = sin_ref[sl, :]
            q = q_ref[sl, hc].astype(F32)
            k = k_ref[sl, hc].astype(F32)
            q = q * cos + pltpu.roll(q, d // 2, 1) * sin
            k = k * cos + pltpu.roll(k, d // 2, 1) * sin
            qb_scr[sl, :] = q.astype(BF16)
            kb_scr[sl, :] = k.astype(BF16)
            qd_scr[sl, :] = (q * qdec_scr[...]).astype(BF16)
            kd_t = (k * kdec_scr[...]).T.astype(BF16)
            kv_scr[n] = jnp.dot(kd_t, v_ref[sl, hc], preferred_element_type=F32)
        state = jnp.zeros((d, d), F32)
        for n in range(n_chunks):
            sb_scr[n] = state.astype(BF16)
            state = c_dec * state + kv_scr[n]
        for n in range(n_chunks):
            sl = pl.ds(n * C, C)
            v = v_ref[sl, hc]
            scores = lax.dot_general(qb_scr[sl, :], kb_scr[sl, :], (((1,), (1,)), ((), ())),
                                     preferred_element_type=F32) * mask_scr[...]
            o = jnp.dot(scores.astype(BF16), v, preferred_element_type=F32)
            o = o + jnp.dot(qd_scr[sl, :], sb_scr[n], preferred_element_type=F32)
            mu = jnp.mean(o, axis=-1, keepdims=True)
            var = jnp.maximum(jnp.mean(o * o, axis=-1, keepdims=True) - mu * mu, 0.0)
            on = (o - mu) * lax.rsqrt(var + EPS) * gn
            g = g_ref[sl, hc].astype(F32)
            o_ref[sl, hc] = ((g / (1.0 + jnp.exp2(g * (-LOG2_E)))) * on).astype(o_ref.dtype)


def _retention(proj3, log_gamma, cos_t, sin_t, ret_norm_g, w_cast):
    B, S, _ = proj3.shape
    H = RET_HEADS
    d = HEAD_DIM
    hp = RET_HEADS_PER_STEP
    ng = H // hp
    blk = lambda off: pl.BlockSpec((None, S, hp * d), lambda b, h, off=off: (b, 0, off + h))
    wshape, wmap = _cast_plan(w_cast, B * ng)
    wspec = pl.BlockSpec(wshape, lambda b, h: wmap(b * ng + h))
    return pl.pallas_call(
        _retention_body,
        grid=(B, ng),
        in_specs=[
            pl.BlockSpec(memory_space=pltpu.SMEM),
            blk(0), blk(ng), blk(2 * ng), blk(3 * ng),
            pl.BlockSpec((S, d), lambda b, h: (0, 0)),
            pl.BlockSpec((S, d), lambda b, h: (0, 0)),
            pl.BlockSpec((1, hp * d), lambda b, h: (0, h)),
            wspec,
        ],
        out_specs=[pl.BlockSpec((None, S, hp * d), lambda b, h: (b, 0, h)), wspec],
        out_shape=[jax.ShapeDtypeStruct((B, S, H * d), BF16),
                   jax.ShapeDtypeStruct(w_cast.shape, BF16)],
        scratch_shapes=[pltpu.VMEM((RET_CHUNK, d), F32)] * 3 + [pltpu.VMEM((S, d), BF16)] * 3 + [
            pltpu.VMEM((S // RET_CHUNK, d, d), F32), pltpu.VMEM((S // RET_CHUNK, d, d), BF16)],
        compiler_params=_params(("arbitrary", "arbitrary")),
        name="retention",
    )(log_gamma, proj3, proj3, proj3, proj3, cos_t, sin_t, ret_norm_g, w_cast)


def _rglru_body(u_ref, z_ref, cw_ref, cb_ref, wg_ref, brg_ref, big_ref, lam_ref, gn_ref, wsrc_ref,
                o_ref, wdst_ref, uf_scr):
    wdst_ref[...] = wsrc_ref[...].astype(wdst_ref.dtype)
    W = LRU_BLOCK_DIM
    R = LRU_ROWS
    row_in_tile = lax.broadcasted_iota(I32, (R, W), 0) & (SUBLANES - 1)
    valids = [row_in_tile >= sh for sh in (1, 2, 4)]
    uf_scr[:SUBLANES, :] = jnp.zeros((SUBLANES, W), F32)
    for nb in range(LRU_BLOCKS_PER_STEP):
        _rglru_block(u_ref, z_ref, cw_ref, cb_ref, wg_ref.at[nb], brg_ref, big_ref, lam_ref, gn_ref,
                     o_ref, uf_scr, slice(nb * W, (nb + 1) * W), valids)


def _rglru_block(u_ref, z_ref, cw_ref, cb_ref, wg_ref, brg_ref, big_ref, lam_ref, gn_ref, o_ref,
                 uf_scr, lanes, valids):
    S = u_ref.shape[0]
    W = LRU_BLOCK_DIM
    R = LRU_ROWS
    K = CONV_WIDTH
    shifts = (1, 2, 4)
    nl = -lam_ref[:, lanes]
    softplus = jnp.maximum(nl, 0.0) + jnp.log1p(jnp.exp(-jnp.abs(nl)))
    coef = -LRU_C * LOG2_E * softplus
    cw = cw_ref[:, lanes]
    cb = cb_ref[:, lanes]
    brg = -LOG2_E * brg_ref[:, lanes]
    big = -LOG2_E * big_ref[:, lanes]
    gn = gn_ref[:, lanes]
    wg = wg_ref[...]
    carry = jnp.zeros((1, W), F32)

    for c in range(S // R):
        base = SUBLANES + c * R
        u = u_ref[c * R:(c + 1) * R, lanes].astype(F32)
        uf_scr[base:base + R, :] = u
        uc = cb + cw[K - 1:K, :] * u
        for j in range(1, K):
            uc = uc + cw[K - 1 - j:K - j, :] * uf_scr[base - j:base - j + R, :]
        gates = jnp.dot(uc.astype(BF16), wg, preferred_element_type=F32)
        r = 1.0 / (1.0 + jnp.exp2(gates[:, :W] + brg))
        i = 1.0 / (1.0 + jnp.exp2(gates[:, W:] + big))
        a = jnp.exp2(coef * r)
        t = 1.0 - a * a
        b = jnp.where(t > 0.0, t * lax.rsqrt(t), 0.0) * (i * uc)
        tiled = (R // SUBLANES, SUBLANES, W)
        for sh, valid in zip(shifts, valids):
            a_s = pltpu.roll(a.reshape(tiled), sh, 1).reshape(R, W)
            b_s = pltpu.roll(b.reshape(tiled), sh, 1).reshape(R, W)
            b = jnp.where(valid, a * b_s + b, b)
            a = jnp.where(valid, a * a_s, a)
        tiles = []
        for k in range(R // SUBLANES):
            rows = slice(k * SUBLANES, (k + 1) * SUBLANES)
            h_tile = a[rows, :] * carry + b[rows, :]
            carry = h_tile[SUBLANES - 1:SUBLANES, :]
            tiles.append(h_tile)
        h = jnp.concatenate(tiles, axis=0)
        ms = jnp.mean(h * h, axis=-1, keepdims=True)
        hl = h * lax.rsqrt(ms + EPS) * gn
        z = z_ref[c * R:(c + 1) * R, lanes].astype(F32)
        c0 = math.sqrt(2.0 / math.pi)
        gelu = 0.5 * z * (1.0 + jnp.tanh(z * (c0 + (c0 * 0.044715) * (z * z))))
        o_ref[c * R:(c + 1) * R, lanes] = (hl * gelu).astype(o_ref.dtype)


def _rglru(proj3, conv_w, conv_b, w_gates, b_rg, b_ig, lam, lru_norm_g, w_cast):
    B, S, _ = proj3.shape
    NB = LRU_BLOCKS
    W = LRU_BLOCK_DIM
    bp = LRU_BLOCKS_PER_STEP
    ng = NB // bp
    u_off = 4 * RET_HEADS * HEAD_DIM // (bp * W)
    z_off = u_off + ng
    vec = pl.BlockSpec((1, bp * W), lambda b, n: (0, n))
    wshape, wmap = _cast_plan(w_cast, B * ng)
    wspec = pl.BlockSpec(wshape, lambda b, n: wmap(b * ng + n))
    return pl.pallas_call(
        _rglru_body,
        grid=(B, ng),
        in_specs=[
            pl.BlockSpec((None, S, bp * W), lambda b, n: (b, 0, u_off + n)),
            pl.BlockSpec((None, S, bp * W), lambda b, n: (b, 0, z_off + n)),
            pl.BlockSpec((CONV_WIDTH, bp * W), lambda b, n: (0, n)),
            vec,
            pl.BlockSpec((bp, W, 2 * W), lambda b, n: (n, 0, 0)),
            vec, vec, vec, vec,
            wspec,
        ],
        out_specs=[pl.BlockSpec((None, S, bp * W), lambda b, n: (b, 0, n)), wspec],
        out_shape=[jax.ShapeDtypeStruct((B, S, NB * W), BF16),
                   jax.ShapeDtypeStruct(w_cast.shape, BF16)],
        scratch_shapes=[pltpu.VMEM((SUBLANES + S, W), F32)],
        compiler_params=_params(("arbitrary", "arbitrary")),
        name="rglru",
    )(proj3, proj3, conv_w, conv_b, w_gates, b_rg, b_ig, lam, lru_norm_g, w_cast)


def _rows_to_tokens(x2d):
    n = x2d.shape[1] // LANES
    parts = [x2d[:, s * LANES:(s + 1) * LANES] for s in range(n)]
    return jnp.swapaxes(jnp.stack(parts, axis=0), 0, 1)


def _tokens_to_rows(x3d):
    xt = jnp.swapaxes(x3d, 0, 1)
    return jnp.concatenate([xt[s] for s in range(x3d.shape[1])], axis=-1)


def _first_row_of_max(v, row):
    m = jnp.max(v, axis=0, keepdims=True)
    idx = jnp.min(jnp.where(v == m, row, v.shape[0]), axis=0, keepdims=True)
    return m, idx


def _outproj_body(ret_ref, lru_ref, x_ref, wo_ref, g_ref, wr_ref, br_ref, x1_ref, h2_ref, lg_ref):
    R = ret_ref.shape[1]
    acc = jnp.dot(ret_ref[...], wo_ref[:R, :], preferred_element_type=F32)
    acc = acc + jnp.dot(lru_ref[...], wo_ref[R:, :], preferred_element_type=F32)
    x1 = x_ref[...] + acc
    x1_ref[...] = x1
    ms = jnp.mean(x1 * x1, axis=-1, keepdims=True)
    h2 = x1 * lax.rsqrt(ms + EPS) * g_ref[...]
    h2_ref[...] = _rows_to_tokens(h2).astype(h2_ref.dtype)
    h_hi = h2.astype(BF16)
    h_lo = (h2 - h_hi.astype(F32)).astype(BF16)
    NR = lg_ref.shape[0]
    both = jnp.dot(h_hi, wr_ref[...], preferred_element_type=F32)
    lg = both[:, :NR] + jnp.dot(h_lo, wr_ref[:, :NR], preferred_element_type=F32)
    lg_ref[...] = (lg + both[:, NR:] + br_ref[...]).T


def _outproj(ret2d, lru2d, x2d, wo_bf16, g, wr_hi_lo, br, tm):
    T, D = x2d.shape
    R = ret2d.shape[1]
    L = lru2d.shape[1]
    NR = br.shape[1]
    const = lambda shape: pl.BlockSpec(shape, lambda i: (0, 0))
    return pl.pallas_call(
        _outproj_body,
        grid=(T // tm,),
        in_specs=[
            pl.BlockSpec((tm, R), lambda i: (i, 0)),
            pl.BlockSpec((tm, L), lambda i: (i, 0)),
            pl.BlockSpec((tm, D), lambda i: (i, 0)),
            const((R + L, D)), const((1, D)), const((D, 2 * NR)), const((1, NR)),
        ],
        out_specs=[
            pl.BlockSpec((tm, D), lambda i: (i, 0)),
            pl.BlockSpec((tm, D // LANES, LANES), lambda i: (i, 0, 0)),
            pl.BlockSpec((NR, tm), lambda i: (0, i)),
        ],
        out_shape=[
            jax.ShapeDtypeStruct((T, D), F32),
            jax.ShapeDtypeStruct((T, D // LANES, LANES), BF16),
            jax.ShapeDtypeStruct((NR, T), F32),
        ],
        compiler_params=_params(("arbitrary",)),
        name="outproj",
    )(ret2d, lru2d, x2d, wo_bf16, g, wr_hi_lo, br)


def _route_body(lg_ref, route_ref, counts_ref, run_scr):
    G, EG = N_GROUPS, EXPERTS_PER_GROUP
    R = ROUTE_ROWS
    sub = ROUTE_SUB
    neg = -jnp.inf

    @pl.when(pl.program_id(0) == 0)
    def _():
        run_scr[...] = jnp.zeros_like(run_scr)

    row = lax.broadcasted_iota(I32, (R, sub), 0)
    r_i = lax.broadcasted_iota(I32, (sub, sub), 0)
    c_i = lax.broadcasted_iota(I32, (sub, sub), 1)
    before = (r_i < c_i).astype(BF16)
    ones = jnp.ones((sub, LANES), BF16)
    run = run_scr[...]
    for s in range(lg_ref.shape[1] // sub):
        cols = slice(s * sub, (s + 1) * sub)
        lg = lg_ref[0:R, cols]
        gl = jnp.where(row < G, lg, neg)
        g_max, g_idx = _first_row_of_max(gl, row)
        g_p = 1.0 / jnp.sum(jnp.exp(gl - g_max), axis=0, keepdims=True)
        lo = G + EG * g_idx
        el = jnp.where((row >= lo) & (row < lo + EG), lg, neg)
        t1, i1 = _first_row_of_max(el, row)
        el2 = jnp.where(row == i1, neg, el)
        t2, i2 = _first_row_of_max(el2, row)
        p2 = jnp.exp(t2 - t1)
        w1 = g_p / (1.0 + p2)
        w2 = g_p * p2 / (1.0 + p2)
        oh1 = row == i1
        oh2 = row == i2
        oh = (oh1 | oh2).astype(BF16)
        prefix = jnp.dot(oh, before, preferred_element_type=F32) + jnp.tile(run, (1, sub // LANES))
        rank1 = jnp.sum(jnp.where(oh1, prefix, 0.0), axis=0, keepdims=True)
        rank2 = jnp.sum(jnp.where(oh2, prefix, 0.0), axis=0, keepdims=True)
        run = run + jnp.dot(oh, ones, preferred_element_type=F32)
        zero = jnp.zeros((1, sub), F32)
        route_ref[:, cols] = jnp.concatenate(
            [(i1 - G).astype(F32), (i2 - G).astype(F32), w1, w2, rank1, rank2, zero, zero], axis=0)
    run_scr[...] = run
    counts_ref[...] = run


def _route(logits_t, tm):
    NR, T = logits_t.shape
    R = ROUTE_ROWS
    return pl.pallas_call(
        _route_body,
        grid=(T // tm,),
        in_specs=[pl.BlockSpec((NR, tm), lambda i: (0, i))],
        out_specs=[pl.BlockSpec((SUBLANES, tm), lambda i: (0, i)),
                   pl.BlockSpec((R, LANES), lambda i: (0, 0))],
        out_shape=[jax.ShapeDtypeStruct((SUBLANES, T), F32), jax.ShapeDtypeStruct((R, LANES), F32)],
        scratch_shapes=[pltpu.VMEM((R, LANES), F32)],
        compiler_params=_params(("arbitrary",)),
        name="route",
    )(logits_t)


def _start_row_gather(src_hbm, idx_ref, dst, sem, lo, hi):
    for r in range(lo, hi):
        pltpu.make_async_copy(src_hbm.at[pl.ds(idx_ref[0, r], 1)], dst.at[pl.ds(r, 1)], sem).start(
            priority=r % 2)


def _zero_after(tile, n_rows, n_cols):
    bits = pltpu.bitcast(tile, jnp.uint32)
    zero = pltpu.bitcast((bits >> 16) >> 16, F32)
    return jnp.tile(zero, (n_rows // SUBLANES, n_cols // LANES))


def _pack_bf16_pair(lo, hi):
    lo_bits = pltpu.bitcast(lo.astype(BF16).astype(F32), jnp.uint32) >> 16
    hi_bits = pltpu.bitcast(hi.astype(BF16).astype(F32), jnp.uint32) & jnp.uint32(0xFFFF0000)
    return lo_bits | hi_bits


def _unpack_bf16_pair(words):
    lo = pltpu.bitcast(words << 16, F32)
    hi = pltpu.bitcast(words & jnp.uint32(0xFFFF0000), F32)
    return lo, hi


def _wait_rows(dst, sem):
    pltpu.make_async_copy(dst, dst, sem).wait()


def _moe_body(be_ref, nu_ref, tok0_ref, tok1_ref, tok2_ref, tokn_ref, h2_hbm, wg_ref, wu_ref, wd_ref,
              y_ref, rowbuf, xb, sem):
    del be_ref
    i = pl.program_id(0)
    last = nu_ref[0] - 1
    rows = rowbuf.shape[1]

    @pl.when(i > last)
    def _():
        y_ref[...] = jnp.zeros_like(y_ref)

    @pl.when(i <= last)
    def _():
        @pl.when(i == 0)
        def _():
            for b, tok_ref in enumerate((tok0_ref, tok1_ref, tok2_ref)):
                _start_row_gather(h2_hbm, tok_ref, rowbuf.at[b], sem.at[b], 0, rows)
            _wait_rows(rowbuf.at[0], sem.at[0])
            xb[0] = _tokens_to_rows(rowbuf[0].astype(F32)).astype(BF16)

        s1 = lax.rem(i + 1, MOE_SLOTS)
        s2 = lax.rem(i + 2, MOE_SLOTS)
        s3 = lax.rem(i + 3, MOE_SLOTS)
        _wait_rows(rowbuf.at[s1], sem.at[s1])
        xcur = xb.at[lax.rem(i, 2)]
        xnext = _tokens_to_rows(rowbuf[s1].astype(F32)).astype(BF16)

        DE = wg_ref.shape[1]
        D = wd_ref.shape[1]
        n_up, n_down = DE // MOE_UP_COLS, D // MOE_DOWN_COLS
        per = rows // (n_up + n_down)
        ready, nxt, nsem = rowbuf.at[s1], rowbuf.at[s3], sem.at[s3]
        hparts = []
        for c in range(n_up):
            cs = slice(c * MOE_UP_COLS, (c + 1) * MOE_UP_COLS)
            _start_row_gather(h2_hbm, tokn_ref, nxt, nsem, c * per, (c + 1) * per)
            zero = _zero_after(ready[0], rows, MOE_UP_COLS)
            gate = jnp.dot(xcur[...], wg_ref[:, cs], preferred_element_type=F32)
            up = jnp.dot(xcur[...], wu_ref[:, cs], preferred_element_type=F32) + zero
            hparts.append(((gate / (1.0 + jnp.exp(-gate))) * up).astype(BF16))
        hmid = jnp.concatenate(hparts, axis=-1)
        xb[lax.rem(i + 1, 2)] = xnext
        yparts = []
        for c in range(n_down):
            cs = slice(c * MOE_DOWN_COLS, (c + 1) * MOE_DOWN_COLS)
            g = n_up + c
            _start_row_gather(h2_hbm, tokn_ref, nxt, nsem, g * per,
                              rows if c == n_down - 1 else (g + 1) * per)
            zero = _zero_after(ready[0], rows, MOE_DOWN_COLS)
            yparts.append(jnp.dot(hmid, wd_ref[:, cs], preferred_element_type=F32) + zero)
        for c in range(n_down // 2):
            cs = slice(c * MOE_DOWN_COLS, (c + 1) * MOE_DOWN_COLS)
            y_ref[:, cs] = _pack_bf16_pair(yparts[c], yparts[c + n_down // 2])

        @pl.when(i == last)
        def _():
            _wait_rows(rowbuf.at[s2], sem.at[s2])
            _wait_rows(nxt, nsem)


def _moe(block_e, n_used, tok3, h2, wg_bf16, wu_bf16, wd_bf16):
    NB, _, rows = tok3.shape
    T, n_tiles, _ = h2.shape
    E, D, DE = wg_bf16.shape
    tok_spec = lambda blk: pl.BlockSpec((None, 1, rows), lambda i, be, nu: (blk(i, nu[0] - 1), 0, 0),
                                        memory_space=pltpu.SMEM)
    w_spec = lambda shape: pl.BlockSpec(shape, lambda i, be, nu: (be[i], 0, 0))
    grid_spec = pltpu.PrefetchScalarGridSpec(
        num_scalar_prefetch=2,
        grid=(NB,),
        in_specs=[
            tok_spec(lambda i, last: 0),
            tok_spec(lambda i, last: jnp.minimum(1, last)),
            tok_spec(lambda i, last: jnp.minimum(2, last)),
            tok_spec(lambda i, last: jnp.minimum(i + 3, last)),
            pl.BlockSpec(memory_space=pl.ANY),
            w_spec((None, D, DE)), w_spec((None, D, DE)), w_spec((None, DE, D)),
        ],
        out_specs=pl.BlockSpec((rows, D // 2), lambda i, be, nu: (i, 0)),
        scratch_shapes=[pltpu.VMEM((MOE_SLOTS, rows, n_tiles, LANES), h2.dtype),
                        pltpu.VMEM((2, rows, D), BF16),
                        pltpu.SemaphoreType.DMA((MOE_SLOTS,))],
    )
    return pl.pallas_call(
        _moe_body,
        grid_spec=grid_spec,
        out_shape=jax.ShapeDtypeStruct((NB * rows, D // 2), jnp.uint32),
        compiler_params=_params(("arbitrary",)),
        name="moe",
    )(block_e, n_used, tok3, tok3, tok3, tok3, h2, wg_bf16, wu_bf16, wd_bf16)


def _combine_body(d0_ref, d1_ref, dn_ref, y_hbm, x1_ref, w_ref, g_ref, o_ref, ybuf, sem):
    i = pl.program_id(0)
    n = pl.num_programs(0)
    slot = lax.rem(i, COMBINE_SLOTS)
    slot1 = lax.rem(i + 1, COMBINE_SLOTS)
    slot2 = lax.rem(i + 2, COMBINE_SLOTS)
    tm = x1_ref.shape[0]

    @pl.when(i == 0)
    def _():
        _start_row_gather(y_hbm, d0_ref, ybuf.at[0], sem.at[0], 0, 2 * tm)
        _start_row_gather(y_hbm, d1_ref, ybuf.at[1], sem.at[1], 0, 2 * tm)

    _wait_rows(ybuf.at[slot], sem.at[slot])
    g = g_ref[...]
    yb, nxt, nsem = ybuf.at[slot], ybuf.at[slot2], sem.at[slot2]
    rc = tm // COMBINE_CHUNKS
    per = 2 * tm // COMBINE_CHUNKS
    for c in range(COMBINE_CHUNKS):
        rows = slice(c * rc, (c + 1) * rc)
        w = w_ref[rows, :]
        y0 = jnp.concatenate(_unpack_bf16_pair(yb[c * rc:(c + 1) * rc, :]), axis=-1)
        y1 = jnp.concatenate(_unpack_bf16_pair(yb[tm + c * rc:tm + (c + 1) * rc, :]), axis=-1)
        x = x1_ref[rows, :] + (w[:, 0:1] * y0 + w[:, 1:2] * y1)
        ms = jnp.mean(x * x, axis=-1, keepdims=True)
        o_ref[rows, :] = x * lax.rsqrt(ms + EPS) * g
        _start_row_gather(y_hbm, dn_ref, nxt, nsem, c * per, (c + 1) * per)

    @pl.when(i == n - 1)
    def _():
        _wait_rows(ybuf.at[slot1], sem.at[slot1])
        _wait_rows(nxt, nsem)


def _combine(dest3, y, x1, e_w, g, tm):
    T, D = x1.shape
    NT = T // tm
    return pl.pallas_call(
        _combine_body,
        grid=(NT,),
        in_specs=[
            pl.BlockSpec((None, 1, 2 * tm), lambda i: (0, 0, 0), memory_space=pltpu.SMEM),
            pl.BlockSpec((None, 1, 2 * tm), lambda i: (min(1, NT - 1), 0, 0), memory_space=pltpu.SMEM),
            pl.BlockSpec((None, 1, 2 * tm), lambda i: (jnp.minimum(i + 2, NT - 1), 0, 0),
                         memory_space=pltpu.SMEM),
            pl.BlockSpec(memory_space=pl.ANY),
            pl.BlockSpec((tm, D), lambda i: (i, 0)),
            pl.BlockSpec((tm, TOP_K), lambda i: (i, 0)),
            pl.BlockSpec((1, D), lambda i: (0, 0)),
        ],
        out_specs=pl.BlockSpec((tm, D), lambda i: (i, 0)),
        out_shape=jax.ShapeDtypeStruct((T, D), F32),
        scratch_shapes=[pltpu.VMEM((COMBINE_SLOTS, 2 * tm, y.shape[1]), y.dtype),
                        pltpu.SemaphoreType.DMA((COMBINE_SLOTS,))],
        compiler_params=_params(("arbitrary",)),
        name="combine",
    )(dest3, dest3, dest3, y, x1, e_w, g)


def _invert_body(lo_ref, hi_ref, dest_ref, wsrc_ref, tok_ref, wdst_ref):
    wdst_ref[...] = wsrc_ref[...].astype(wdst_ref.dtype)
    i = pl.program_id(0)
    blk = dest_ref.shape[1]

    @pl.when(i == 0)
    def _():
        def clear(j, carry):
            tok_ref[j] = 0
            return carry

        for s in range(lo_ref.shape[0]):
            lax.fori_loop(lo_ref[s], hi_ref[s], clear, 0)

    shift = TOP_K.bit_length() - 1
    n = INVERT_UNROLL

    def place(c, carry):
        j0 = pl.multiple_of(c * n, n)
        tok0 = lax.shift_right_logical(i * blk + j0, shift)
        for u in range(n):
            tok_ref[dest_ref[0, j0 + u]] = tok0 + (u >> shift)
        return carry

    lax.fori_loop(0, blk // n, place, 0)


def _invert(hole_lo, hole_hi, dest_flat, n_rows, w_cast):
    assert TOP_K & (TOP_K - 1) == 0
    A = dest_flat.shape[0]
    blk = min(INVERT_BLOCK, A)
    wshape, wmap = _cast_plan(w_cast, A // blk)
    wspec = pl.BlockSpec(wshape, lambda i, lo, hi: wmap(i))
    grid_spec = pltpu.PrefetchScalarGridSpec(
        num_scalar_prefetch=2,
        grid=(A // blk,),
        in_specs=[pl.BlockSpec((None, 1, blk), lambda i, lo, hi: (i, 0, 0), memory_space=pltpu.SMEM),
                  wspec],
        out_specs=[pl.BlockSpec(memory_space=pltpu.SMEM), wspec],
    )
    return pl.pallas_call(
        _invert_body,
        grid_spec=grid_spec,
        out_shape=[jax.ShapeDtypeStruct((n_rows,), I32), jax.ShapeDtypeStruct(w_cast.shape, BF16)],
        compiler_params=_params(("arbitrary",)),
        name="invert",
    )(hole_lo, hole_hi, dest_flat.reshape(A // blk, 1, blk), w_cast)


def _layout(route, counts, rows, w_cast):
    T = route.shape[1]
    A = T * TOP_K
    e_id = route[0:TOP_K].T.astype(I32)
    e_w = route[TOP_K:2 * TOP_K].T
    rank = route[2 * TOP_K:3 * TOP_K].T.astype(I32)
    counts = counts[N_GROUPS:N_GROUPS + N_EXPERTS, 0].astype(I32)
    padded = (counts + rows - 1) // rows * rows
    pad_end = jnp.cumsum(padded)
    pad_start = pad_end - padded
    experts = jnp.arange(N_EXPERTS, dtype=I32)
    start_of = jnp.sum(jnp.where(e_id[:, :, None] == experts, pad_start, 0), axis=-1)
    dest = start_of + rank
    n_blocks = (A + N_EXPERTS * (rows - 1)) // rows
    P = n_blocks * rows
    hole_lo = jnp.concatenate([pad_start + counts, pad_end[-1:]])
    hole_hi = jnp.concatenate([pad_end, jnp.full((1,), P, I32)])
    tok, w_cast_bf16 = _invert(hole_lo, hole_hi, dest.reshape(A), P, w_cast)
    n_used = pad_end[-1:] // rows
    block_start = jnp.minimum(jnp.arange(n_blocks, dtype=I32), n_used - 1) * rows
    block_e = jnp.sum((pad_end[None, :] <= block_start[:, None]).astype(I32), axis=1)
    return e_w, dest, tok, block_e, n_used, n_blocks, w_cast_bf16


def kernel(x, norm_mix_g, w_in, ret_norm_g, conv_w, conv_b, w_rg, b_rg, w_ig, b_ig, lru_lambda,
           lru_norm_g, w_out, norm_ffn_g, w_group, b_group, w_router, b_router, w_gate, w_up,
           w_down, norm_final_g):
    B, S, D = x.shape
    T = B * S
    depth = norm_mix_g.shape[0]
    assert depth == 1, "the combine kernel fuses the final norm, so only one layer is supported"
    H, d = RET_HEADS, HEAD_DIM

    half = d // 2
    inv = ROPE_BASE ** (-jnp.arange(half, dtype=F32) / half)
    ang = jnp.arange(S, dtype=F32)[:, None] * inv[None, :]
    cos_t = jnp.concatenate([jnp.cos(ang), jnp.cos(ang)], axis=-1)
    sin_t = jnp.concatenate([-jnp.sin(ang), jnp.sin(ang)], axis=-1)
    log_gamma = jnp.log1p(-(2.0 ** (-5.0 - jnp.arange(H, dtype=F32))))

    x2d = x.reshape(T, D)
    for l in range(depth):
        proj = _inproj(x2d, norm_mix_g[l][None, :], w_in[l].astype(BF16), INPROJ_TM, INPROJ_TN)
        proj3 = proj.reshape(B, S, proj.shape[1])
        ret, wg_bf16 = _retention(proj3, log_gamma, cos_t, sin_t, ret_norm_g[l][None, :], w_gate[l])
        w_gates = (-LOG2_E * jnp.concatenate([w_rg[l], w_ig[l]], axis=-1)).astype(BF16)
        lru, wu_bf16 = _rglru(proj3, conv_w[l], conv_b[l][None, :], w_gates, b_rg[l][None, :],
                              b_ig[l][None, :], lru_lambda[l][None, :], lru_norm_g[l][None, :],
                              w_up[l])

        n_route = N_GROUPS + N_EXPERTS
        wr = jnp.concatenate([w_group[l], w_router[l], jnp.zeros((D, LANES - n_route), F32)], axis=-1)
        br = jnp.concatenate([b_group[l], b_router[l], jnp.zeros((LANES - n_route,), F32)])[None, :]
        wr_hi = wr.astype(BF16)
        wr_lo = (wr - wr_hi.astype(F32)).astype(BF16)
        x1, h2, logits_t = _outproj(ret.reshape(T, -1), lru.reshape(T, -1), x2d, w_out[l].astype(BF16),
                                    norm_ffn_g[l][None, :], jnp.concatenate([wr_hi, wr_lo], axis=1),
                                    br, OUTPROJ_TM)
        route, counts = _route(logits_t, ROUTE_TM)

        e_w, dest, tok, block_e, n_used, n_blocks, wd_bf16 = _layout(route, counts, MOE_ROWS,
                                                                     w_down[l])
        y = _moe(block_e, n_used, tok.reshape(n_blocks, 1, MOE_ROWS), h2, wg_bf16, wu_bf16, wd_bf16)
        dest3 = dest.reshape(T // COMBINE_TM, COMBINE_TM, TOP_K).transpose(0, 2, 1).reshape(
            T // COMBINE_TM, 1, TOP_K * COMBINE_TM)
        x2d = _combine(dest3, y, x1, e_w, norm_final_g[None, :], COMBINE_TM)
    return x2d.reshape(B, S, D)
```

```python
import math

import jax
import jax.numpy as jnp
from jax import lax
from jax.experimental import pallas as pl
from jax.experimental.pallas import tpu as pltpu

F32 = jnp.float32
BF16 = jnp.bfloat16
I32 = jnp.int32

EPS = 1e-6
LOG2_E = 1.4426950408889634
RET_HEADS = 8
HEAD_DIM = 128
RET_CHUNK = 128
ROPE_BASE = 10000.0
LRU_BLOCKS = 8
LRU_BLOCK_DIM = 128
CONV_WIDTH = 4
LRU_C = 8.0
N_GROUPS = 4
EXPERTS_PER_GROUP = 8
N_EXPERTS = N_GROUPS * EXPERTS_PER_GROUP
TOP_K = 2

LANES = 128
SUBLANES = 8
VMEM_LIMIT = 56 * 1024 * 1024

INPROJ_TM = 1024
INPROJ_TN = 2048
RET_HEADS_PER_STEP = 4
LRU_ROWS = 128
LRU_BLOCKS_PER_STEP = 2
OUTPROJ_TM = 512
ROUTE_TM = 2048
ROUTE_SUB = 256
ROUTE_ROWS = 40
INVERT_BLOCK = 2048
INVERT_UNROLL = 32
MOE_ROWS = 256
MOE_SLOTS = 4
MOE_UP_COLS = 256
MOE_DOWN_COLS = 512
COMBINE_TM = 512
COMBINE_SLOTS = 3
COMBINE_CHUNKS = 16


def _params(sem):
    return pltpu.CompilerParams(dimension_semantics=sem, vmem_limit_bytes=VMEM_LIMIT)


def _inproj_body(x_ref, g_ref, w_ref, o_ref, h_scr):
    @pl.when(pl.program_id(1) == 0)
    def _():
        x = x_ref[...]
        ms = jnp.mean(x * x, axis=-1, keepdims=True)
        h_scr[...] = (x * lax.rsqrt(ms + EPS) * g_ref[...]).astype(BF16)

    o_ref[...] = jnp.dot(h_scr[...], w_ref[...], preferred_element_type=F32).astype(o_ref.dtype)


def _inproj(x2d, g, w_bf16, tm, tn):
    T, D = x2d.shape
    N = w_bf16.shape[1]
    return pl.pallas_call(
        _inproj_body,
        grid=(T // tm, N // tn),
        in_specs=[
            pl.BlockSpec((tm, D), lambda i, j: (i, 0)),
            pl.BlockSpec((1, D), lambda i, j: (0, 0)),
            pl.BlockSpec((D, tn), lambda i, j: (0, j)),
        ],
        out_specs=pl.BlockSpec((tm, tn), lambda i, j: (i, j)),
        out_shape=jax.ShapeDtypeStruct((T, N), BF16),
        scratch_shapes=[pltpu.VMEM((tm, D), BF16)],
        compiler_params=_params(("arbitrary", "arbitrary")),
        name="inproj",
    )(x2d, g, w_bf16)


def _cast_plan(w, n_steps):
    E, R, C = w.shape
    if n_steps >= E:
        parts = n_steps // E
        assert n_steps == E * parts and R % parts == 0
        return (None, R // parts, C), (lambda s: (s // parts, s % parts, 0))
    per_step = E // n_steps
    assert E == per_step * n_steps
    return (per_step, R, C), (lambda s: (s, 0, 0))


def _retention_body(lg_ref, q_ref, k_ref, v_ref, g_ref, cos_ref, sin_ref, gn_ref, wsrc_ref, o_ref,
                    wdst_ref, mask_scr, qdec_scr, kdec_scr, qb_scr, kb_scr, qd_scr, kv_scr, sb_scr):
    wdst_ref[...] = wsrc_ref[...].astype(wdst_ref.dtype)
    C = RET_CHUNK
    d = HEAD_DIM
    S = q_ref.shape[0]
    row = lax.broadcasted_iota(I32, (C, d), 0).astype(F32)
    col = lax.broadcasted_iota(I32, (C, d), 1).astype(F32)
    rel = row - col
    scale = d ** -0.5
    n_chunks = S // C

    for hh in range(RET_HEADS_PER_STEP):
        hc = slice(hh * d, (hh + 1) * d)
        lg = lg_ref[pl.program_id(1) * RET_HEADS_PER_STEP + hh]
        mask_scr[...] = jnp.where(rel >= 0, jnp.exp(jnp.maximum(rel, 0.0) * lg), 0.0) * scale
        qdec_scr[...] = jnp.exp((row + 1.0) * lg) * scale
        kdec_scr[...] = jnp.exp((C - 1.0 - row) * lg)
        c_dec = jnp.exp(jnp.full((1, d), float(C), F32) * lg)
        gn = gn_ref[:, hc]

        for n in range(n_chunks):
            sl = pl.ds(n * C, C)
            cos = cos_ref[sl, :]
            sin = sin_ref[sl, :]
            q = q_ref[sl, hc].astype(F32)
            k = k_ref[sl, hc].astype(F32)
            q = q * cos + pltpu.roll(q, d // 2, 1) * sin
            k = k * cos + pltpu.roll(k, d // 2, 1) * sin
            qb_scr[sl, :] = q.astype(BF16)
            kb_scr[sl, :] = k.astype(BF16)
            qd_scr[sl, :] = (q * qdec_scr[...]).astype(BF16)
            kd_t = (k * kdec_scr[...]).T.astype(BF16)
            kv_scr[n] = jnp.dot(kd_t, v_ref[sl, hc], preferred_element_type=F32)
        state = jnp.zeros((d, d), F32)
        for n in range(n_chunks):
            sb_scr[n] = state.astype(BF16)
            state = c_dec * state + kv_scr[n]
        for n in range(n_chunks):
            sl = pl.ds(n * C, C)
            v = v_ref[sl, hc]
            scores = lax.dot_general(qb_scr[sl, :], kb_scr[sl, :], (((1,), (1,)), ((), ())),
                                     preferred_element_type=F32) * mask_scr[...]
            o = jnp.dot(scores.astype(BF16), v, preferred_element_type=F32)
            o = o + jnp.dot(qd_scr[sl, :], sb_scr[n], preferred_element_type=F32)
            mu = jnp.mean(o, axis=-1, keepdims=True)
            var = jnp.maximum(jnp.mean(o * o, axis=-1, keepdims=True) - mu * mu, 0.0)
            on = (o - mu) * lax.rsqrt(var + EPS) * gn
            g = g_ref[sl, hc].astype(F32)
            o_ref[sl, hc] = ((g / (1.0 + jnp.exp2(g * (-LOG2_E)))) * on).astype(o_ref.dtype)


def _retention(proj3, log_gamma, cos_t, sin_t, ret_norm_g, w_cast):
    B, S, _ = proj3.shape
    H = RET_HEADS
    d = HEAD_DIM
    hp = RET_HEADS_PER_STEP
    ng = H // hp
    blk = lambda off: pl.BlockSpec((None, S, hp * d), lambda b, h, off=off: (b, 0, off + h))
    wshape, wmap = _cast_plan(w_cast, B * ng)
    wspec = pl.BlockSpec(wshape, lambda b, h: wmap(b * ng + h))
    return pl.pallas_call(
        _retention_body,
        grid=(B, ng),
        in_specs=[
            pl.BlockSpec(memory_space=pltpu.SMEM),
            blk(0), blk(ng), blk(2 * ng), blk(3 * ng),
            pl.BlockSpec((S, d), lambda b, h: (0, 0)),
            pl.BlockSpec((S, d), lambda b, h: (0, 0)),
            pl.BlockSpec((1, hp * d), lambda b, h: (0, h)),
            wspec,
        ],
        out_specs=[pl.BlockSpec((None, S, hp * d), lambda b, h: (b, 0, h)), wspec],
        out_shape=[jax.ShapeDtypeStruct((B, S, H * d), BF16),
                   jax.ShapeDtypeStruct(w_cast.shape, BF16)],
        scratch_shapes=[pltpu.VMEM((RET_CHUNK, d), F32)] * 3 + [pltpu.VMEM((S, d), BF16)] * 3 + [
            pltpu.VMEM((S // RET_CHUNK, d, d), F32), pltpu.VMEM((S // RET_CHUNK, d, d), BF16)],
        compiler_params=_params(("arbitrary", "arbitrary")),
        name="retention",
    )(log_gamma, proj3, proj3, proj3, proj3, cos_t, sin_t, ret_norm_g, w_cast)


def _rglru_body(u_ref, z_ref, cw_ref, cb_ref, wg_ref, brg_ref, big_ref, lam_ref, gn_ref, wsrc_ref,
                o_ref, wdst_ref, uf_scr):
    wdst_ref[...] = wsrc_ref[...].astype(wdst_ref.dtype)
    W = LRU_BLOCK_DIM
    R = LRU_ROWS
    row_in_tile = lax.broadcasted_iota(I32, (R, W), 0) & (SUBLANES - 1)
    valids = [row_in_tile >= sh for sh in (1, 2, 4)]
    uf_scr[:SUBLANES, :] = jnp.zeros((SUBLANES, W), F32)
    for nb in range(LRU_BLOCKS_PER_STEP):
        _rglru_block(u_ref, z_ref, cw_ref, cb_ref, wg_ref.at[nb], brg_ref, big_ref, lam_ref, gn_ref,
                     o_ref, uf_scr, slice(nb * W, (nb + 1) * W), valids)


def _rglru_block(u_ref, z_ref, cw_ref, cb_ref, wg_ref, brg_ref, big_ref, lam_ref, gn_ref, o_ref,
                 uf_scr, lanes, valids):
    S = u_ref.shape[0]
    W = LRU_BLOCK_DIM
    R = LRU_ROWS
    K = CONV_WIDTH
    shifts = (1, 2, 4)
    nl = -lam_ref[:, lanes]
    softplus = jnp.maximum(nl, 0.0) + jnp.log1p(jnp.exp(-jnp.abs(nl)))
    coef = -LRU_C * LOG2_E * softplus
    cw = cw_ref[:, lanes]
    cb = cb_ref[:, lanes]
    brg = -LOG2_E * brg_ref[:, lanes]
    big = -LOG2_E * big_ref[:, lanes]
    gn = gn_ref[:, lanes]
    wg = wg_ref[...]
    carry = jnp.zeros((1, W), F32)

    for c in range(S // R):
        base = SUBLANES + c * R
        u = u_ref[c * R:(c + 1) * R, lanes].astype(F32)
        uf_scr[base:base + R, :] = u
        uc = cb + cw[K - 1:K, :] * u
        for j in range(1, K):
            uc = uc + cw[K - 1 - j:K - j, :] * uf_scr[base - j:base - j + R, :]
        gates = jnp.dot(uc.astype(BF16), wg, preferred_element_type=F32)
        r = 1.0 / (1.0 + jnp.exp2(gates[:, :W] + brg))
        i = 1.0 / (1.0 + jnp.exp2(gates[:, W:] + big))
        a = jnp.exp2(coef * r)
        t = 1.0 - a * a
        b = jnp.where(t > 0.0, t * lax.rsqrt(t), 0.0) * (i * uc)
        tiled = (R // SUBLANES, SUBLANES, W)
        for sh, valid in zip(shifts, valids):
            a_s = pltpu.roll(a.reshape(tiled), sh, 1).reshape(R, W)
            b_s = pltpu.roll(b.reshape(tiled), sh, 1).reshape(R, W)
            b = jnp.where(valid, a * b_s + b, b)
            a = jnp.where(valid, a * a_s, a)
        tiles = []
        for k in range(R // SUBLANES):
            rows = slice(k * SUBLANES, (k + 1) * SUBLANES)
            h_tile = a[rows, :] * carry + b[rows, :]
            carry = h_tile[SUBLANES - 1:SUBLANES, :]
            tiles.append(h_tile)
        h = jnp.concatenate(tiles, axis=0)
        ms = jnp.mean(h * h, axis=-1, keepdims=True)
        hl = h * lax.rsqrt(ms + EPS) * gn
        z = z_ref[c * R:(c + 1) * R, lanes].astype(F32)
        c0 = math.sqrt(2.0 / math.pi)
        gelu = 0.5 * z * (1.0 + jnp.tanh(z * (c0 + (c0 * 0.044715) * (z * z))))
        o_ref[c * R:(c + 1) * R, lanes] = (hl * gelu).astype(o_ref.dtype)


def _rglru(proj3, conv_w, conv_b, w_gates, b_rg, b_ig, lam, lru_norm_g, w_cast):
    B, S, _ = proj3.shape
    NB = LRU_BLOCKS
    W = LRU_BLOCK_DIM
    bp = LRU_BLOCKS_PER_STEP
    ng = NB // bp
    u_off = 4 * RET_HEADS * HEAD_DIM // (bp * W)
    z_off = u_off + ng
    vec = pl.BlockSpec((1, bp * W), lambda b, n: (0, n))
    wshape, wmap = _cast_plan(w_cast, B * ng)
    wspec = pl.BlockSpec(wshape, lambda b, n: wmap(b * ng + n))
    return pl.pallas_call(
        _rglru_body,
        grid=(B, ng),
        in_specs=[
            pl.BlockSpec((None, S, bp * W), lambda b, n: (b, 0, u_off + n)),
            pl.BlockSpec((None, S, bp * W), lambda b, n: (b, 0, z_off + n)),
            pl.BlockSpec((CONV_WIDTH, bp * W), lambda b, n: (0, n)),
            vec,
            pl.BlockSpec((bp, W, 2 * W), lambda b, n: (n, 0, 0)),
            vec, vec, vec, vec,
            wspec,
        ],
        out_specs=[pl.BlockSpec((None, S, bp * W), lambda b, n: (b, 0, n)), wspec],
        out_shape=[jax.ShapeDtypeStruct((B, S, NB * W), BF16),
                   jax.ShapeDtypeStruct(w_cast.shape, BF16)],
        scratch_shapes=[pltpu.VMEM((SUBLANES + S, W), F32)],
        compiler_params=_params(("arbitrary", "arbitrary")),
        name="rglru",
    )(proj3, proj3, conv_w, conv_b, w_gates, b_rg, b_ig, lam, lru_norm_g, w_cast)


def _rows_to_tokens(x2d):
    n = x2d.shape[1] // LANES
    parts = [x2d[:, s * LANES:(s + 1) * LANES] for s in range(n)]
    return jnp.swapaxes(jnp.stack(parts, axis=0), 0, 1)


def _tokens_to_rows(x3d):
    xt = jnp.swapaxes(x3d, 0, 1)
    return jnp.concatenate([xt[s] for s in range(x3d.shape[1])], axis=-1)


def _first_row_of_max(v, row):
    m = jnp.max(v, axis=0, keepdims=True)
    idx = jnp.min(jnp.where(v == m, row, v.shape[0]), axis=0, keepdims=True)
    return m, idx


def _outproj_body(ret_ref, lru_ref, x_ref, wo_ref, g_ref, wr_ref, br_ref, x1_ref, h2_ref, lg_ref):
    R = ret_ref.shape[1]
    acc = jnp.dot(ret_ref[...], wo_ref[:R, :], preferred_element_type=F32)
    acc = acc + jnp.dot(lru_ref[...], wo_ref[R:, :], preferred_element_type=F32)
    x1 = x_ref[...] + acc
    x1_ref[...] = x1
    ms = jnp.mean(x1 * x1, axis=-1, keepdims=True)
    h2 = x1 * lax.rsqrt(ms + EPS) * g_ref[...]
    h2_ref[...] = _rows_to_tokens(h2).astype(h2_ref.dtype)
    h_hi = h2.astype(BF16)
    h_lo = (h2 - h_hi.astype(F32)).astype(BF16)
    NR = lg_ref.shape[0]
    both = jnp.dot(h_hi, wr_ref[...], preferred_element_type=F32)
    lg = both[:, :NR] + jnp.dot(h_lo, wr_ref[:, :NR], preferred_element_type=F32)
    lg_ref[...] = (lg + both[:, NR:] + br_ref[...]).T


def _outproj(ret2d, lru2d, x2d, wo_bf16, g, wr_hi_lo, br, tm):
    T, D = x2d.shape
    R = ret2d.shape[1]
    L = lru2d.shape[1]
    NR = br.shape[1]
    const = lambda shape: pl.BlockSpec(shape, lambda i: (0, 0))
    return pl.pallas_call(
        _outproj_body,
        grid=(T // tm,),
        in_specs=[
            pl.BlockSpec((tm, R), lambda i: (i, 0)),
            pl.BlockSpec((tm, L), lambda i: (i, 0)),
            pl.BlockSpec((tm, D), lambda i: (i, 0)),
            const((R + L, D)), const((1, D)), const((D, 2 * NR)), const((1, NR)),
        ],
        out_specs=[
            pl.BlockSpec((tm, D), lambda i: (i, 0)),
            pl.BlockSpec((tm, D // LANES, LANES), lambda i: (i, 0, 0)),
            pl.BlockSpec((NR, tm), lambda i: (0, i)),
        ],
        out_shape=[
            jax.ShapeDtypeStruct((T, D), F32),
            jax.ShapeDtypeStruct((T, D // LANES, LANES), BF16),
            jax.ShapeDtypeStruct((NR, T), F32),
        ],
        compiler_params=_params(("arbitrary",)),
        name="outproj",
    )(ret2d, lru2d, x2d, wo_bf16, g, wr_hi_lo, br)


def _route_body(lg_ref, route_ref, counts_ref, run_scr):
    G, EG = N_GROUPS, EXPERTS_PER_GROUP
    R = ROUTE_ROWS
    sub = ROUTE_SUB
    neg = -jnp.inf

    @pl.when(pl.program_id(0) == 0)
    def _():
        run_scr[...] = jnp.zeros_like(run_scr)

    row = lax.broadcasted_iota(I32, (R, sub), 0)
    r_i = lax.broadcasted_iota(I32, (sub, sub), 0)
    c_i = lax.broadcasted_iota(I32, (sub, sub), 1)
    before = (r_i < c_i).astype(BF16)
    ones = jnp.ones((sub, LANES), BF16)
    run = run_scr[...]
    for s in range(lg_ref.shape[1] // sub):
        cols = slice(s * sub, (s + 1) * sub)
        lg = lg_ref[0:R, cols]
        gl = jnp.where(row < G, lg, neg)
        g_max, g_idx = _first_row_of_max(gl, row)
        g_p = 1.0 / jnp.sum(jnp.exp(gl - g_max), axis=0, keepdims=True)
        lo = G + EG * g_idx
        el = jnp.where((row >= lo) & (row < lo + EG), lg, neg)
        t1, i1 = _first_row_of_max(el, row)
        el2 = jnp.where(row == i1, neg, el)
        t2, i2 = _first_row_of_max(el2, row)
        p2 = jnp.exp(t2 - t1)
        w1 = g_p / (1.0 + p2)
        w2 = g_p * p2 / (1.0 + p2)
        oh1 = row == i1
        oh2 = row == i2
        oh = (oh1 | oh2).astype(BF16)
        prefix = jnp.dot(oh, before, preferred_element_type=F32) + jnp.tile(run, (1, sub // LANES))
        rank1 = jnp.sum(jnp.where(oh1, prefix, 0.0), axis=0, keepdims=True)
        rank2 = jnp.sum(jnp.where(oh2, prefix, 0.0), axis=0, keepdims=True)
        run = run + jnp.dot(oh, ones, preferred_element_type=F32)
        zero = jnp.zeros((1, sub), F32)
        route_ref[:, cols] = jnp.concatenate(
            [(i1 - G).astype(F32), (i2 - G).astype(F32), w1, w2, rank1, rank2, zero, zero], axis=0)
    run_scr[...] = run
    counts_ref[...] = run


def _route(logits_t, tm):
    NR, T = logits_t.shape
    R = ROUTE_ROWS
    return pl.pallas_call(
        _route_body,
        grid=(T // tm,),
        in_specs=[pl.BlockSpec((NR, tm), lambda i: (0, i))],
        out_specs=[pl.BlockSpec((SUBLANES, tm), lambda i: (0, i)),
                   pl.BlockSpec((R, LANES), lambda i: (0, 0))],
        out_shape=[jax.ShapeDtypeStruct((SUBLANES, T), F32), jax.ShapeDtypeStruct((R, LANES), F32)],
        scratch_shapes=[pltpu.VMEM((R, LANES), F32)],
        compiler_params=_params(("arbitrary",)),
        name="route",
    )(logits_t)


def _start_row_gather(src_hbm, idx_ref, dst, sem, lo, hi):
    for r in range(lo, hi):
        pltpu.make_async_copy(src_hbm.at[pl.ds(idx_ref[0, r], 1)], dst.at[pl.ds(r, 1)], sem).start(
            priority=r % 2)


def _zero_after(tile, n_rows, n_cols):
    bits = pltpu.bitcast(tile, jnp.uint32)
    zero = pltpu.bitcast((bits >> 16) >> 16, F32)
    return jnp.tile(zero, (n_rows // SUBLANES, n_cols // LANES))


def _pack_bf16_pair(lo, hi):
    lo_bits = pltpu.bitcast(lo.astype(BF16).astype(F32), jnp.uint32) >> 16
    hi_bits = pltpu.bitcast(hi.astype(BF16).astype(F32), jnp.uint32) & jnp.uint32(0xFFFF0000)
    return lo_bits | hi_bits


def _unpack_bf16_pair(words):
    lo = pltpu.bitcast(words << 16, F32)
    hi = pltpu.bitcast(words & jnp.uint32(0xFFFF0000), F32)
    return lo, hi


def _wait_rows(dst, sem):
    pltpu.make_async_copy(dst, dst, sem).wait()


def _moe_body(be_ref, nu_ref, tok0_ref, tok1_ref, tok2_ref, tokn_ref, h2_hbm, wg_ref, wu_ref, wd_ref,
              y_ref, rowbuf, xb, sem):
    del be_ref
    i = pl.program_id(0)
    last = nu_ref[0] - 1
    rows = rowbuf.shape[1]

    @pl.when(i > last)
    def _():
        y_ref[...] = jnp.zeros_like(y_ref)

    @pl.when(i <= last)
    def _():
        @pl.when(i == 0)
        def _():
            for b, tok_ref in enumerate((tok0_ref, tok1_ref, tok2_ref)):
                _start_row_gather(h2_hbm, tok_ref, rowbuf.at[b], sem.at[b], 0, rows)
            _wait_rows(rowbuf.at[0], sem.at[0])
            xb[0] = _tokens_to_rows(rowbuf[0].astype(F32)).astype(BF16)

        s1 = lax.rem(i + 1, MOE_SLOTS)
        s2 = lax.rem(i + 2, MOE_SLOTS)
        s3 = lax.rem(i + 3, MOE_SLOTS)
        _wait_rows(rowbuf.at[s1], sem.at[s1])
        xcur = xb.at[lax.rem(i, 2)]
        xnext = _tokens_to_rows(rowbuf[s1].astype(F32)).astype(BF16)

        DE = wg_ref.shape[1]
        D = wd_ref.shape[1]
        n_up, n_down = DE // MOE_UP_COLS, D // MOE_DOWN_COLS
        per = rows // (n_up + n_down)
        ready, nxt, nsem = rowbuf.at[s1], rowbuf.at[s3], sem.at[s3]
        hparts = []
        for c in range(n_up):
            cs = slice(c * MOE_UP_COLS, (c + 1) * MOE_UP_COLS)
            _start_row_gather(h2_hbm, tokn_ref, nxt, nsem, c * per, (c + 1) * per)
            zero = _zero_after(ready[0], rows, MOE_UP_COLS)
            gate = jnp.dot(xcur[...], wg_ref[:, cs], preferred_element_type=F32)
            up = jnp.dot(xcur[...], wu_ref[:, cs], preferred_element_type=F32) + zero
            hparts.append(((gate / (1.0 + jnp.exp(-gate))) * up).astype(BF16))
        hmid = jnp.concatenate(hparts, axis=-1)
        xb[lax.rem(i + 1, 2)] = xnext
        yparts = []
        for c in range(n_down):
            cs = slice(c * MOE_DOWN_COLS, (c + 1) * MOE_DOWN_COLS)
            g = n_up + c
            _start_row_gather(h2_hbm, tokn_ref, nxt, nsem, g * per,
                              rows if c == n_down - 1 else (g + 1) * per)
            zero = _zero_after(ready[0], rows, MOE_DOWN_COLS)
            yparts.append(jnp.dot(hmid, wd_ref[:, cs], preferred_element_type=F32) + zero)
        for c in range(n_down // 2):
            cs = slice(c * MOE_DOWN_COLS, (c + 1) * MOE_DOWN_COLS)
            y_ref[:, cs] = _pack_bf16_pair(yparts[c], yparts[c + n_down // 2])

        @pl.when(i == last)
        def _():
            _wait_rows(rowbuf.at[s2], sem.at[s2])
            _wait_rows(nxt, nsem)


def _moe(block_e, n_used, tok3, h2, wg_bf16, wu_bf16, wd_bf16):
    NB, _, rows = tok3.shape
    T, n_tiles, _ = h2.shape
    E, D, DE = wg_bf16.shape
    tok_spec = lambda blk: pl.BlockSpec((None, 1, rows), lambda i, be, nu: (blk(i, nu[0] - 1), 0, 0),
                                        memory_space=pltpu.SMEM)
    w_spec = lambda shape: pl.BlockSpec(shape, lambda i, be, nu: (be[i], 0, 0))
    grid_spec = pltpu.PrefetchScalarGridSpec(
        num_scalar_prefetch=2,
        grid=(NB,),
        in_specs=[
            tok_spec(lambda i, last: 0),
            tok_spec(lambda i, last: jnp.minimum(1, last)),
            tok_spec(lambda i, last: jnp.minimum(2, last)),
            tok_spec(lambda i, last: jnp.minimum(i + 3, last)),
            pl.BlockSpec(memory_space=pl.ANY),
            w_spec((None, D, DE)), w_spec((None, D, DE)), w_spec((None, DE, D)),
        ],
        out_specs=pl.BlockSpec((rows, D // 2), lambda i, be, nu: (i, 0)),
        scratch_shapes=[pltpu.VMEM((MOE_SLOTS, rows, n_tiles, LANES), h2.dtype),
                        pltpu.VMEM((2, rows, D), BF16),
                        pltpu.SemaphoreType.DMA((MOE_SLOTS,))],
    )
    return pl.pallas_call(
        _moe_body,
        grid_spec=grid_spec,
        out_shape=jax.ShapeDtypeStruct((NB * rows, D // 2), jnp.uint32),
        compiler_params=_params(("arbitrary",)),
        name="moe",
    )(block_e, n_used, tok3, tok3, tok3, tok3, h2, wg_bf16, wu_bf16, wd_bf16)


def _combine_body(d0_ref, d1_ref, dn_ref, y_hbm, x1_ref, w_ref, g_ref, o_ref, ybuf, sem):
    i = pl.program_id(0)
    n = pl.num_programs(0)
    slot = lax.rem(i, COMBINE_SLOTS)
    slot1 = lax.rem(i + 1, COMBINE_SLOTS)
    slot2 = lax.rem(i + 2, COMBINE_SLOTS)
    tm = x1_ref.shape[0]

    @pl.when(i == 0)
    def _():
        _start_row_gather(y_hbm, d0_ref, ybuf.at[0], sem.at[0], 0, 2 * tm)
        _start_row_gather(y_hbm, d1_ref, ybuf.at[1], sem.at[1], 0, 2 * tm)

    _wait_rows(ybuf.at[slot], sem.at[slot])
    g = g_ref[...]
    yb, nxt, nsem = ybuf.at[slot], ybuf.at[slot2], sem.at[slot2]
    rc = tm // COMBINE_CHUNKS
    per = 2 * tm // COMBINE_CHUNKS
    for c in range(COMBINE_CHUNKS):
        rows = slice(c * rc, (c + 1) * rc)
        w = w_ref[rows, :]
        y0 = jnp.concatenate(_unpack_bf16_pair(yb[c * rc:(c + 1) * rc, :]), axis=-1)
        y1 = jnp.concatenate(_unpack_bf16_pair(yb[tm + c * rc:tm + (c + 1) * rc, :]), axis=-1)
        x = x1_ref[rows, :] + (w[:, 0:1] * y0 + w[:, 1:2] * y1)
        ms = jnp.mean(x * x, axis=-1, keepdims=True)
        o_ref[rows, :] = x * lax.rsqrt(ms + EPS) * g
        _start_row_gather(y_hbm, dn_ref, nxt, nsem, c * per, (c + 1) * per)

    @pl.when(i == n - 1)
    def _():
        _wait_rows(ybuf.at[slot1], sem.at[slot1])
        _wait_rows(nxt, nsem)


def _combine(dest3, y, x1, e_w, g, tm):
    T, D = x1.shape
    NT = T // tm
    return pl.pallas_call(
        _combine_body,
        grid=(NT,),
        in_specs=[
            pl.BlockSpec((None, 1, 2 * tm), lambda i: (0, 0, 0), memory_space=pltpu.SMEM),
            pl.BlockSpec((None, 1, 2 * tm), lambda i: (min(1, NT - 1), 0, 0), memory_space=pltpu.SMEM),
            pl.BlockSpec((None, 1, 2 * tm), lambda i: (jnp.minimum(i + 2, NT - 1), 0, 0),
                         memory_space=pltpu.SMEM),
            pl.BlockSpec(memory_space=pl.ANY),
            pl.BlockSpec((tm, D), lambda i: (i, 0)),
            pl.BlockSpec((tm, TOP_K), lambda i: (i, 0)),
            pl.BlockSpec((1, D), lambda i: (0, 0)),
        ],
        out_specs=pl.BlockSpec((tm, D), lambda i: (i, 0)),
        out_shape=jax.ShapeDtypeStruct((T, D), F32),
        scratch_shapes=[pltpu.VMEM((COMBINE_SLOTS, 2 * tm, y.shape[1]), y.dtype),
                        pltpu.SemaphoreType.DMA((COMBINE_SLOTS,))],
        compiler_params=_params(("arbitrary",)),
        name="combine",
    )(dest3, dest3, dest3, y, x1, e_w, g)


def _invert_body(lo_ref, hi_ref, dest_ref, wsrc_ref, tok_ref, wdst_ref):
    wdst_ref[...] = wsrc_ref[...].astype(wdst_ref.dtype)
    i = pl.program_id(0)
    blk = dest_ref.shape[1]

    @pl.when(i == 0)
    def _():
        def clear(j, carry):
            tok_ref[j] = 0
            return carry

        for s in range(lo_ref.shape[0]):
            lax.fori_loop(lo_ref[s], hi_ref[s], clear, 0)

    shift = TOP_K.bit_length() - 1
    n = INVERT_UNROLL

    def place(c, carry):
        j0 = pl.multiple_of(c * n, n)
        tok0 = lax.shift_right_logical(i * blk + j0, shift)
        for u in range(n):
            tok_ref[dest_ref[0, j0 + u]] = tok0 + (u >> shift)
        return carry

    lax.fori_loop(0, blk // n, place, 0)


def _invert(hole_lo, hole_hi, dest_flat, n_rows, w_cast):
    assert TOP_K & (TOP_K - 1) == 0
    A = dest_flat.shape[0]
    blk = min(INVERT_BLOCK, A)
    wshape, wmap = _cast_plan(w_cast, A // blk)
    wspec = pl.BlockSpec(wshape, lambda i, lo, hi: wmap(i))
    grid_spec = pltpu.PrefetchScalarGridSpec(
        num_scalar_prefetch=2,
        grid=(A // blk,),
        in_specs=[pl.BlockSpec((None, 1, blk), lambda i, lo, hi: (i, 0, 0), memory_space=pltpu.SMEM),
                  wspec],
        out_specs=[pl.BlockSpec(memory_space=pltpu.SMEM), wspec],
    )
    return pl.pallas_call(
        _invert_body,
        grid_spec=grid_spec,
        out_shape=[jax.ShapeDtypeStruct((n_rows,), I32), jax.ShapeDtypeStruct(w_cast.shape, BF16)],
        compiler_params=_params(("arbitrary",)),
        name="invert",
    )(hole_lo, hole_hi, dest_flat.reshape(A // blk, 1, blk), w_cast)


def _layout(route, counts, rows, w_cast):
    T = route.shape[1]
    A = T * TOP_K
    e_id = route[0:TOP_K].T.astype(I32)
    e_w = route[TOP_K:2 * TOP_K].T
    rank = route[2 * TOP_K:3 * TOP_K].T.astype(I32)
    counts = counts[N_GROUPS:N_GROUPS + N_EXPERTS, 0].astype(I32)
    padded = (counts + rows - 1) // rows * rows
    pad_end = jnp.cumsum(padded)
    pad_start = pad_end - padded
    experts = jnp.arange(N_EXPERTS, dtype=I32)
    start_of = jnp.sum(jnp.where(e_id[:, :, None] == experts, pad_start, 0), axis=-1)
    dest = start_of + rank
    n_blocks = (A + N_EXPERTS * (rows - 1)) // rows
    P = n_blocks * rows
    hole_lo = jnp.concatenate([pad_start + counts, pad_end[-1:]])
    hole_hi = jnp.concatenate([pad_end, jnp.full((1,), P, I32)])
    tok, w_cast_bf16 = _invert(hole_lo, hole_hi, dest.reshape(A), P, w_cast)
    n_used = pad_end[-1:] // rows
    block_start = jnp.minimum(jnp.arange(n_blocks, dtype=I32), n_used - 1) * rows
    block_e = jnp.sum((pad_end[None, :] <= block_start[:, None]).astype(I32), axis=1)
    return e_w, dest, tok, block_e, n_used, n_blocks, w_cast_bf16


def kernel(x, norm_mix_g, w_in, ret_norm_g, conv_w, conv_b, w_rg, b_rg, w_ig, b_ig, lru_lambda,
           lru_norm_g, w_out, norm_ffn_g, w_group, b_group, w_router, b_router, w_gate, w_up,
           w_down, norm_final_g):
    B, S, D = x.shape
    T = B * S
    depth = norm_mix_g.shape[0]
    assert depth == 1, "the combine kernel fuses the final norm, so only one layer is supported"
    H, d = RET_HEADS, HEAD_DIM

    half = d // 2
    inv = ROPE_BASE ** (-jnp.arange(half, dtype=F32) / half)
    ang = jnp.arange(S, dtype=F32)[:, None] * inv[None, :]
    cos_t = jnp.concatenate([jnp.cos(ang), jnp.cos(ang)], axis=-1)
    sin_t = jnp.concatenate([-jnp.sin(ang), jnp.sin(ang)], axis=-1)
    log_gamma = jnp.log1p(-(2.0 ** (-5.0 - jnp.arange(H, dtype=F32))))

    x2d = x.reshape(T, D)
    for l in range(depth):
        proj = _inproj(x2d, norm_mix_g[l][None, :], w_in[l].astype(BF16), INPROJ_TM, INPROJ_TN)
        proj3 = proj.reshape(B, S, proj.shape[1])
        ret, wg_bf16 = _retention(proj3, log_gamma, cos_t, sin_t, ret_norm_g[l][None, :], w_gate[l])
        w_gates = (-LOG2_E * jnp.concatenate([w_rg[l], w_ig[l]], axis=-1)).astype(BF16)
        lru, wu_bf16 = _rglru(proj3, conv_w[l], conv_b[l][None, :], w_gates, b_rg[l][None, :],
                              b_ig[l][None, :], lru_lambda[l][None, :], lru_norm_g[l][None, :],
                              w_up[l])

        n_route = N_GROUPS + N_EXPERTS
        wr = jnp.concatenate([w_group[l], w_router[l], jnp.zeros((D, LANES - n_route), F32)], axis=-1)
        br = jnp.concatenate([b_group[l], b_router[l], jnp.zeros((LANES - n_route,), F32)])[None, :]
        wr_hi = wr.astype(BF16)
        wr_lo = (wr - wr_hi.astype(F32)).astype(BF16)
        x1, h2, logits_t = _outproj(ret.reshape(T, -1), lru.reshape(T, -1), x2d, w_out[l].astype(BF16),
                                    norm_ffn_g[l][None, :], jnp.concatenate([wr_hi, wr_lo], axis=1),
                                    br, OUTPROJ_TM)
        route, counts = _route(logits_t, ROUTE_TM)

        e_w, dest, tok, block_e, n_used, n_blocks, wd_bf16 = _layout(route, counts, MOE_ROWS,
                                                                     w_down[l])
        y = _moe(block_e, n_used, tok.reshape(n_blocks, 1, MOE_ROWS), h2, wg_bf16, wu_bf16, wd_bf16)
        dest3 = dest.reshape(T // COMBINE_TM, COMBINE_TM, TOP_K).transpose(0, 2, 1).reshape(
            T // COMBINE_TM, 1, TOP_K * COMBINE_TM)
        x2d = _combine(dest3, y, x1, e_w, norm_final_g[None, :], COMBINE_TM)
    return x2d.reshape(B, S, D)
```

```python
import math

import jax
import jax.numpy as jnp
from jax import lax
from jax.experimental import pallas as pl
from jax.experimental.pallas import tpu as pltpu

F32 = jnp.float32
BF16 = jnp.bfloat16
I32 = jnp.int32

EPS = 1e-6
LOG2_E = 1.4426950408889634
RET_HEADS = 8
HEAD_DIM = 128
RET_CHUNK = 128
ROPE_BASE = 10000.0
LRU_BLOCKS = 8
LRU_BLOCK_DIM = 128
CONV_WIDTH = 4
LRU_C = 8.0
N_GROUPS = 4
EXPERTS_PER_GROUP = 8
N_EXPERTS = N_GROUPS * EXPERTS_PER_GROUP
TOP_K = 2

LANES = 128
SUBLANES = 8
VMEM_LIMIT = 56 * 1024 * 1024

INPROJ_TM = 1024
INPROJ_TN = 2048
RET_HEADS_PER_STEP = 4
LRU_ROWS = 128
LRU_BLOCKS_PER_STEP = 2
OUTPROJ_TM = 512
ROUTE_TM = 2048
ROUTE_SUB = 256
ROUTE_ROWS = 40
INVERT_BLOCK = 2048
INVERT_UNROLL = 32
MOE_ROWS = 256
MOE_SLOTS = 4
MOE_UP_COLS = 256
MOE_DOWN_COLS = 512
COMBINE_TM = 256
COMBINE_SLOTS = 3
COMBINE_CHUNKS = 8


def _params(sem):
    return pltpu.CompilerParams(dimension_semantics=sem, vmem_limit_bytes=VMEM_LIMIT)


def _inproj_body(x_ref, g_ref, w_ref, o_ref, h_scr):
    @pl.when(pl.program_id(1) == 0)
    def _():
        x = x_ref[...]
        ms = jnp.mean(x * x, axis=-1, keepdims=True)
        h_scr[...] = (x * lax.rsqrt(ms + EPS) * g_ref[...]).astype(BF16)

    o_ref[...] = jnp.dot(h_scr[...], w_ref[...], preferred_element_type=F32).astype(o_ref.dtype)


def _inproj(x2d, g, w_bf16, tm, tn):
    T, D = x2d.shape
    N = w_bf16.shape[1]
    return pl.pallas_call(
        _inproj_body,
        grid=(T // tm, N // tn),
        in_specs=[
            pl.BlockSpec((tm, D), lambda i, j: (i, 0)),
            pl.BlockSpec((1, D), lambda i, j: (0, 0)),
            pl.BlockSpec((D, tn), lambda i, j: (0, j)),
        ],
        out_specs=pl.BlockSpec((tm, tn), lambda i, j: (i, j)),
        out_shape=jax.ShapeDtypeStruct((T, N), BF16),
        scratch_shapes=[pltpu.VMEM((tm, D), BF16)],
        compiler_params=_params(("arbitrary", "arbitrary")),
        name="inproj",
    )(x2d, g, w_bf16)


def _cast_plan(w, n_steps):
    E, R, C = w.shape
    if n_steps >= E:
        parts = n_steps // E
        assert n_steps == E * parts and R % parts == 0
        return (None, R // parts, C), (lambda s: (s // parts, s % parts, 0))
    per_step = E // n_steps
    assert E == per_step * n_steps
    return (per_step, R, C), (lambda s: (s, 0, 0))


def _retention_body(lg_ref, q_ref, k_ref, v_ref, g_ref, cos_ref, sin_ref, gn_ref, wsrc_ref, o_ref,
                    wdst_ref, mask_scr, qdec_scr, kdec_scr, qb_scr, kb_scr, qd_scr, kv_scr, sb_scr):
    wdst_ref[...] = wsrc_ref[...].astype(wdst_ref.dtype)
    C = RET_CHUNK
    d = HEAD_DIM
    S = q_ref.shape[0]
    row = lax.broadcasted_iota(I32, (C, d), 0).astype(F32)
    col = lax.broadcasted_iota(I32, (C, d), 1).astype(F32)
    rel = row - col
    scale = d ** -0.5
    n_chunks = S // C

    for hh in range(RET_HEADS_PER_STEP):
        hc = slice(hh * d, (hh + 1) * d)
        lg = lg_ref[pl.program_id(1) * RET_HEADS_PER_STEP + hh]
        mask_scr[...] = jnp.where(rel >= 0, jnp.exp(jnp.maximum(rel, 0.0) * lg), 0.0) * scale
        qdec_scr[...] = jnp.exp((row + 1.0) * lg) * scale
        kdec_scr[...] = jnp.exp((C - 1.0 - row) * lg)
        c_dec = jnp.exp(jnp.full((1, d), float(C), F32) * lg)
        gn = gn_ref[:, hc]

        for n in range(n_chunks):
            sl = pl.ds(n * C, C)
            cos = cos_ref[sl, :]
            sin = sin_ref[sl, :]
            q = q_ref[sl, hc].astype(F32)
            k = k_ref[sl, hc].astype(F32)
            q = q * cos + pltpu.roll(q, d // 2, 1) * sin
            k = k * cos + pltpu.roll(k, d // 2, 1) * sin
            qb_scr[sl, :] = q.astype(BF16)
            kb_scr[sl, :] = k.astype(BF16)
            qd_scr[sl, :] = (q * qdec_scr[...]).astype(BF16)
            kd_t = (k * kdec_scr[...]).T.astype(BF16)
            kv_scr[n] = jnp.dot(kd_t, v_ref[sl, hc], preferred_element_type=F32)
        state = jnp.zeros((d, d), F32)
        for n in range(n_chunks):
            sb_scr[n] = state.astype(BF16)
            state = c_dec * state + kv_scr[n]
        for n in range(n_chunks):
            sl = pl.ds(n * C, C)
            v = v_ref[sl, hc]
            scores = lax.dot_general(qb_scr[sl, :], kb_scr[sl, :], (((1,), (1,)), ((), ())),
                                     preferred_element_type=F32) * mask_scr[...]
            o = jnp.dot(scores.astype(BF16), v, preferred_element_type=F32)
            o = o + jnp.dot(qd_scr[sl, :], sb_scr[n], preferred_element_type=F32)
            mu = jnp.mean(o, axis=-1, keepdims=True)
            var = jnp.maximum(jnp.mean(o * o, axis=-1, keepdims=True) - mu * mu, 0.0)
            on = (o - mu) * lax.rsqrt(var + EPS) * gn
            g = g_ref[sl, hc].astype(F32)
            o_ref[sl, hc] = ((g / (1.0 + jnp.exp2(g * (-LOG2_E)))) * on).astype(o_ref.dtype)


def _retention(proj3, log_gamma, cos_t, sin_t, ret_norm_g, w_cast):
    B, S, _ = proj3.shape
    H = RET_HEADS
    d = HEAD_DIM
    hp = RET_HEADS_PER_STEP
    ng = H // hp
    blk = lambda off: pl.BlockSpec((None, S, hp * d), lambda b, h, off=off: (b, 0, off + h))
    wshape, wmap = _cast_plan(w_cast, B * ng)
    wspec = pl.BlockSpec(wshape, lambda b, h: wmap(b * ng + h))
    return pl.pallas_call(
        _retention_body,
        grid=(B, ng),
        in_specs=[
            pl.BlockSpec(memory_space=pltpu.SMEM),
            blk(0), blk(ng), blk(2 * ng), blk(3 * ng),
            pl.BlockSpec((S, d), lambda b, h: (0, 0)),
            pl.BlockSpec((S, d), lambda b, h: (0, 0)),
            pl.BlockSpec((1, hp * d), lambda b, h: (0, h)),
            wspec,
        ],
        out_specs=[pl.BlockSpec((None, S, hp * d), lambda b, h: (b, 0, h)), wspec],
        out_shape=[jax.ShapeDtypeStruct((B, S, H * d), BF16),
                   jax.ShapeDtypeStruct(w_cast.shape, BF16)],
        scratch_shapes=[pltpu.VMEM((RET_CHUNK, d), F32)] * 3 + [pltpu.VMEM((S, d), BF16)] * 3 + [
            pltpu.VMEM((S // RET_CHUNK, d, d), F32), pltpu.VMEM((S // RET_CHUNK, d, d), BF16)],
        compiler_params=_params(("arbitrary", "arbitrary")),
        name="retention",
    )(log_gamma, proj3, proj3, proj3, proj3, cos_t, sin_t, ret_norm_g, w_cast)


def _rglru_body(u_ref, z_ref, cw_ref, cb_ref, wg_ref, brg_ref, big_ref, lam_ref, gn_ref, wsrc_ref,
                o_ref, wdst_ref, uf_scr):
    wdst_ref[...] = wsrc_ref[...].astype(wdst_ref.dtype)
    W = LRU_BLOCK_DIM
    R = LRU_ROWS
    row_in_tile = lax.broadcasted_iota(I32, (R, W), 0) & (SUBLANES - 1)
    valids = [row_in_tile >= sh for sh in (1, 2, 4)]
    uf_scr[:SUBLANES, :] = jnp.zeros((SUBLANES, W), F32)
    for nb in range(LRU_BLOCKS_PER_STEP):
        _rglru_block(u_ref, z_ref, cw_ref, cb_ref, wg_ref.at[nb], brg_ref, big_ref, lam_ref, gn_ref,
                     o_ref, uf_scr, slice(nb * W, (nb + 1) * W), valids)


def _rglru_block(u_ref, z_ref, cw_ref, cb_ref, wg_ref, brg_ref, big_ref, lam_ref, gn_ref, o_ref,
                 uf_scr, lanes, valids):
    S = u_ref.shape[0]
    W = LRU_BLOCK_DIM
    R = LRU_ROWS
    K = CONV_WIDTH
    shifts = (1, 2, 4)
    nl = -lam_ref[:, lanes]
    softplus = jnp.maximum(nl, 0.0) + jnp.log1p(jnp.exp(-jnp.abs(nl)))
    coef = -LRU_C * LOG2_E * softplus
    cw = cw_ref[:, lanes]
    cb = cb_ref[:, lanes]
    brg = -LOG2_E * brg_ref[:, lanes]
    big = -LOG2_E * big_ref[:, lanes]
    gn = gn_ref[:, lanes]
    wg = wg_ref[...]
    carry = jnp.zeros((1, W), F32)

    for c in range(S // R):
        base = SUBLANES + c * R
        u = u_ref[c * R:(c + 1) * R, lanes].astype(F32)
        uf_scr[base:base + R, :] = u
        uc = cb + cw[K - 1:K, :] * u
        for j in range(1, K):
            uc = uc + cw[K - 1 - j:K - j, :] * uf_scr[base - j:base - j + R, :]
        gates = jnp.dot(uc.astype(BF16), wg, preferred_element_type=F32)
        r = 1.0 / (1.0 + jnp.exp2(gates[:, :W] + brg))
        i = 1.0 / (1.0 + jnp.exp2(gates[:, W:] + big))
        a = jnp.exp2(coef * r)
        t = 1.0 - a * a
        b = jnp.where(t > 0.0, t * lax.rsqrt(t), 0.0) * (i * uc)
        tiled = (R // SUBLANES, SUBLANES, W)
        for sh, valid in zip(shifts, valids):
            a_s = pltpu.roll(a.reshape(tiled), sh, 1).reshape(R, W)
            b_s = pltpu.roll(b.reshape(tiled), sh, 1).reshape(R, W)
            b = jnp.where(valid, a * b_s + b, b)
            a = jnp.where(valid, a * a_s, a)
        tiles = []
        for k in range(R // SUBLANES):
            rows = slice(k * SUBLANES, (k + 1) * SUBLANES)
            h_tile = a[rows, :] * carry + b[rows, :]
            carry = h_tile[SUBLANES - 1:SUBLANES, :]
            tiles.append(h_tile)
        h = jnp.concatenate(tiles, axis=0)
        ms = jnp.mean(h * h, axis=-1, keepdims=True)
        hl = h * lax.rsqrt(ms + EPS) * gn
        z = z_ref[c * R:(c + 1) * R, lanes].astype(F32)
        c0 = math.sqrt(2.0 / math.pi)
        gelu = 0.5 * z * (1.0 + jnp.tanh(z * (c0 + (c0 * 0.044715) * (z * z))))
        o_ref[c * R:(c + 1) * R, lanes] = (hl * gelu).astype(o_ref.dtype)


def _rglru(proj3, conv_w, conv_b, w_gates, b_rg, b_ig, lam, lru_norm_g, w_cast):
    B, S, _ = proj3.shape
    NB = LRU_BLOCKS
    W = LRU_BLOCK_DIM
    bp = LRU_BLOCKS_PER_STEP
    ng = NB // bp
    u_off = 4 * RET_HEADS * HEAD_DIM // (bp * W)
    z_off = u_off + ng
    vec = pl.BlockSpec((1, bp * W), lambda b, n: (0, n))
    wshape, wmap = _cast_plan(w_cast, B * ng)
    wspec = pl.BlockSpec(wshape, lambda b, n: wmap(b * ng + n))
    return pl.pallas_call(
        _rglru_body,
        grid=(B, ng),
        in_specs=[
            pl.BlockSpec((None, S, bp * W), lambda b, n: (b, 0, u_off + n)),
            pl.BlockSpec((None, S, bp * W), lambda b, n: (b, 0, z_off + n)),
            pl.BlockSpec((CONV_WIDTH, bp * W), lambda b, n: (0, n)),
            vec,
            pl.BlockSpec((bp, W, 2 * W), lambda b, n: (n, 0, 0)),
            vec, vec, vec, vec,
            wspec,
        ],
        out_specs=[pl.BlockSpec((None, S, bp * W), lambda b, n: (b, 0, n)), wspec],
        out_shape=[jax.ShapeDtypeStruct((B, S, NB * W), BF16),
                   jax.ShapeDtypeStruct(w_cast.shape, BF16)],
        scratch_shapes=[pltpu.VMEM((SUBLANES + S, W), F32)],
        compiler_params=_params(("arbitrary", "arbitrary")),
        name="rglru",
    )(proj3, proj3, conv_w, conv_b, w_gates, b_rg, b_ig, lam, lru_norm_g, w_cast)


def _rows_to_tokens(x2d):
    n = x2d.shape[1] // LANES
    parts = [x2d[:, s * LANES:(s + 1) * LANES] for s in range(n)]
    return jnp.swapaxes(jnp.stack(parts, axis=0), 0, 1)


def _tokens_to_rows(x3d):
    xt = jnp.swapaxes(x3d, 0, 1)
    return jnp.concatenate([xt[s] for s in range(x3d.shape[1])], axis=-1)


def _first_row_of_max(v, row):
    m = jnp.max(v, axis=0, keepdims=True)
    idx = jnp.min(jnp.where(v == m, row, v.shape[0]), axis=0, keepdims=True)
    return m, idx


def _outproj_body(ret_ref, lru_ref, x_ref, wo_ref, g_ref, wr_ref, br_ref, x1_ref, h2_ref, lg_ref):
    R = ret_ref.shape[1]
    acc = jnp.dot(ret_ref[...], wo_ref[:R, :], preferred_element_type=F32)
    acc = acc + jnp.dot(lru_ref[...], wo_ref[R:, :], preferred_element_type=F32)
    x1 = x_ref[...] + acc
    x1_ref[...] = x1
    ms = jnp.mean(x1 * x1, axis=-1, keepdims=True)
    h2 = x1 * lax.rsqrt(ms + EPS) * g_ref[...]
    h2_ref[...] = _rows_to_tokens(h2).astype(h2_ref.dtype)
    h_hi = h2.astype(BF16)
    h_lo = (h2 - h_hi.astype(F32)).astype(BF16)
    NR = lg_ref.shape[0]
    both = jnp.dot(h_hi, wr_ref[...], preferred_element_type=F32)
    lg = both[:, :NR] + jnp.dot(h_lo, wr_ref[:, :NR], preferred_element_type=F32)
    lg_ref[...] = (lg + both[:, NR:] + br_ref[...]).T


def _outproj(ret2d, lru2d, x2d, wo_bf16, g, wr_hi_lo, br, tm):
    T, D = x2d.shape
    R = ret2d.shape[1]
    L = lru2d.shape[1]
    NR = br.shape[1]
    const = lambda shape: pl.BlockSpec(shape, lambda i: (0, 0))
    return pl.pallas_call(
        _outproj_body,
        grid=(T // tm,),
        in_specs=[
            pl.BlockSpec((tm, R), lambda i: (i, 0)),
            pl.BlockSpec((tm, L), lambda i: (i, 0)),
            pl.BlockSpec((tm, D), lambda i: (i, 0)),
            const((R + L, D)), const((1, D)), const((D, 2 * NR)), const((1, NR)),
        ],
        out_specs=[
            pl.BlockSpec((tm, D), lambda i: (i, 0)),
            pl.BlockSpec((tm, D // LANES, LANES), lambda i: (i, 0, 0)),
            pl.BlockSpec((NR, tm), lambda i: (0, i)),
        ],
        out_shape=[
            jax.ShapeDtypeStruct((T, D), F32),
            jax.ShapeDtypeStruct((T, D // LANES, LANES), BF16),
            jax.ShapeDtypeStruct((NR, T), F32),
        ],
        compiler_params=_params(("arbitrary",)),
        name="outproj",
    )(ret2d, lru2d, x2d, wo_bf16, g, wr_hi_lo, br)


def _route_body(lg_ref, route_ref, counts_ref, run_scr):
    G, EG = N_GROUPS, EXPERTS_PER_GROUP
    R = ROUTE_ROWS
    sub = ROUTE_SUB
    neg = -jnp.inf

    @pl.when(pl.program_id(0) == 0)
    def _():
        run_scr[...] = jnp.zeros_like(run_scr)

    row = lax.broadcasted_iota(I32, (R, sub), 0)
    r_i = lax.broadcasted_iota(I32, (sub, sub), 0)
    c_i = lax.broadcasted_iota(I32, (sub, sub), 1)
    before = (r_i < c_i).astype(BF16)
    ones = jnp.ones((sub, LANES), BF16)
    run = run_scr[...]
    for s in range(lg_ref.shape[1] // sub):
        cols = slice(s * sub, (s + 1) * sub)
        lg = lg_ref[0:R, cols]
        gl = jnp.where(row < G, lg, neg)
        g_max, g_idx = _first_row_of_max(gl, row)
        g_p = 1.0 / jnp.sum(jnp.exp(gl - g_max), axis=0, keepdims=True)
        lo = G + EG * g_idx
        el = jnp.where((row >= lo) & (row < lo + EG), lg, neg)
        t1, i1 = _first_row_of_max(el, row)
        el2 = jnp.where(row == i1, neg, el)
        t2, i2 = _first_row_of_max(el2, row)
        p2 = jnp.exp(t2 - t1)
        w1 = g_p / (1.0 + p2)
        w2 = g_p * p2 / (1.0 + p2)
        oh1 = row == i1
        oh2 = row == i2
        oh = (oh1 | oh2).astype(BF16)
        prefix = jnp.dot(oh, before, preferred_element_type=F32) + jnp.tile(run, (1, sub // LANES))
        rank1 = jnp.sum(jnp.where(oh1, prefix, 0.0), axis=0, keepdims=True)
        rank2 = jnp.sum(jnp.where(oh2, prefix, 0.0), axis=0, keepdims=True)
        run = run + jnp.dot(oh, ones, preferred_element_type=F32)
        zero = jnp.zeros((1, sub), F32)
        route_ref[:, cols] = jnp.concatenate(
            [(i1 - G).astype(F32), (i2 - G).astype(F32), w1, w2, rank1, rank2, zero, zero], axis=0)
    run_scr[...] = run
    counts_ref[...] = run


def _route(logits_t, tm):
    NR, T = logits_t.shape
    R = ROUTE_ROWS
    return pl.pallas_call(
        _route_body,
        grid=(T // tm,),
        in_specs=[pl.BlockSpec((NR, tm), lambda i: (0, i))],
        out_specs=[pl.BlockSpec((SUBLANES, tm), lambda i: (0, i)),
                   pl.BlockSpec((R, LANES), lambda i: (0, 0))],
        out_shape=[jax.ShapeDtypeStruct((SUBLANES, T), F32), jax.ShapeDtypeStruct((R, LANES), F32)],
        scratch_shapes=[pltpu.VMEM((R, LANES), F32)],
        compiler_params=_params(("arbitrary",)),
        name="route",
    )(logits_t)


def _start_row_gather(src_hbm, idx_ref, dst, sem, lo, hi):
    for r in range(lo, hi):
        pltpu.make_async_copy(src_hbm.at[pl.ds(idx_ref[0, r], 1)], dst.at[pl.ds(r, 1)], sem).start(
            priority=r % 2)


def _zero_after(tile, n_rows, n_cols):
    bits = pltpu.bitcast(tile, jnp.uint32)
    zero = pltpu.bitcast((bits >> 16) >> 16, F32)
    return jnp.tile(zero, (n_rows // SUBLANES, n_cols // LANES))


def _pack_bf16_pair(lo, hi):
    lo_bits = pltpu.bitcast(lo.astype(BF16).astype(F32), jnp.uint32) >> 16
    hi_bits = pltpu.bitcast(hi.astype(BF16).astype(F32), jnp.uint32) & jnp.uint32(0xFFFF0000)
    return lo_bits | hi_bits


def _unpack_bf16_pair(words):
    lo = pltpu.bitcast(words << 16, F32)
    hi = pltpu.bitcast(words & jnp.uint32(0xFFFF0000), F32)
    return lo, hi


def _wait_rows(dst, sem):
    pltpu.make_async_copy(dst, dst, sem).wait()


def _moe_body(be_ref, nu_ref, tok0_ref, tok1_ref, tok2_ref, tokn_ref, h2_hbm, wg_ref, wu_ref, wd_ref,
              y_ref, rowbuf, xb, sem):
    del be_ref
    i = pl.program_id(0)
    last = nu_ref[0] - 1
    rows = rowbuf.shape[1]

    @pl.when(i > last)
    def _():
        y_ref[...] = jnp.zeros_like(y_ref)

    @pl.when(i <= last)
    def _():
        @pl.when(i == 0)
        def _():
            for b, tok_ref in enumerate((tok0_ref, tok1_ref, tok2_ref)):
                _start_row_gather(h2_hbm, tok_ref, rowbuf.at[b], sem.at[b], 0, rows)
            _wait_rows(rowbuf.at[0], sem.at[0])
            xb[0] = _tokens_to_rows(rowbuf[0].astype(F32)).astype(BF16)

        s1 = lax.rem(i + 1, MOE_SLOTS)
        s2 = lax.rem(i + 2, MOE_SLOTS)
        s3 = lax.rem(i + 3, MOE_SLOTS)
        _wait_rows(rowbuf.at[s1], sem.at[s1])
        xcur = xb.at[lax.rem(i, 2)]
        xnext = _tokens_to_rows(rowbuf[s1].astype(F32)).astype(BF16)

        DE = wg_ref.shape[1]
        D = wd_ref.shape[1]
        n_up, n_down = DE // MOE_UP_COLS, D // MOE_DOWN_COLS
        per = rows // (n_up + n_down)
        ready, nxt, nsem = rowbuf.at[s1], rowbuf.at[s3], sem.at[s3]
        hparts = []
        for c in range(n_up):
            cs = slice(c * MOE_UP_COLS, (c + 1) * MOE_UP_COLS)
            _start_row_gather(h2_hbm, tokn_ref, nxt, nsem, c * per, (c + 1) * per)
            zero = _zero_after(ready[0], rows, MOE_UP_COLS)
            gate = jnp.dot(xcur[...], wg_ref[:, cs], preferred_element_type=F32)
            up = jnp.dot(xcur[...], wu_ref[:, cs], preferred_element_type=F32) + zero
            hparts.append(((gate / (1.0 + jnp.exp(-gate))) * up).astype(BF16))
        hmid = jnp.concatenate(hparts, axis=-1)
        xb[lax.rem(i + 1, 2)] = xnext
        yparts = []
        for c in range(n_down):
            cs = slice(c * MOE_DOWN_COLS, (c + 1) * MOE_DOWN_COLS)
            g = n_up + c
            _start_row_gather(h2_hbm, tokn_ref, nxt, nsem, g * per,
                              rows if c == n_down - 1 else (g + 1) * per)
            zero = _zero_after(ready[0], rows, MOE_DOWN_COLS)
            yparts.append(jnp.dot(hmid, wd_ref[:, cs].astype(BF16), preferred_element_type=F32) + zero)
        for c in range(n_down // 2):
            cs = slice(c * MOE_DOWN_COLS, (c + 1) * MOE_DOWN_COLS)
            y_ref[:, cs] = _pack_bf16_pair(yparts[c], yparts[c + n_down // 2])

        @pl.when(i == last)
        def _():
            _wait_rows(rowbuf.at[s2], sem.at[s2])
            _wait_rows(nxt, nsem)


def _moe(block_e, n_used, tok3, h2, wg_bf16, wu_bf16, wd_bf16):
    NB, _, rows = tok3.shape
    T, n_tiles, _ = h2.shape
    E, D, DE = wg_bf16.shape
    tok_spec = lambda blk: pl.BlockSpec((None, 1, rows), lambda i, be, nu: (blk(i, nu[0] - 1), 0, 0),
                                        memory_space=pltpu.SMEM)
    w_spec = lambda shape: pl.BlockSpec(shape, lambda i, be, nu: (be[i], 0, 0))
    grid_spec = pltpu.PrefetchScalarGridSpec(
        num_scalar_prefetch=2,
        grid=(NB,),
        in_specs=[
            tok_spec(lambda i, last: 0),
            tok_spec(lambda i, last: jnp.minimum(1, last)),
            tok_spec(lambda i, last: jnp.minimum(2, last)),
            tok_spec(lambda i, last: jnp.minimum(i + 3, last)),
            pl.BlockSpec(memory_space=pl.ANY),
            w_spec((None, D, DE)), w_spec((None, D, DE)), w_spec((None, DE, D)),
        ],
        out_specs=pl.BlockSpec((rows, D // 2), lambda i, be, nu: (i, 0)),
        scratch_shapes=[pltpu.VMEM((MOE_SLOTS, rows, n_tiles, LANES), h2.dtype),
                        pltpu.VMEM((2, rows, D), BF16),
                        pltpu.SemaphoreType.DMA((MOE_SLOTS,))],
    )
    return pl.pallas_call(
        _moe_body,
        grid_spec=grid_spec,
        out_shape=jax.ShapeDtypeStruct((NB * rows, D // 2), jnp.uint32),
        compiler_params=_params(("arbitrary",)),
        name="moe",
    )(block_e, n_used, tok3, tok3, tok3, tok3, h2, wg_bf16, wu_bf16, wd_bf16)


def _combine_body(d0_ref, d1_ref, dn_ref, y_hbm, x1_ref, w_ref, g_ref, o_ref, ybuf, sem):
    i = pl.program_id(0)
    n = pl.num_programs(0)
    slot = lax.rem(i, COMBINE_SLOTS)
    slot1 = lax.rem(i + 1, COMBINE_SLOTS)
    slot2 = lax.rem(i + 2, COMBINE_SLOTS)
    tm = x1_ref.shape[0]

    @pl.when(i == 0)
    def _():
        _start_row_gather(y_hbm, d0_ref, ybuf.at[0], sem.at[0], 0, 2 * tm)
        _start_row_gather(y_hbm, d1_ref, ybuf.at[1], sem.at[1], 0, 2 * tm)

    _wait_rows(ybuf.at[slot], sem.at[slot])
    g = g_ref[...]
    yb, nxt, nsem = ybuf.at[slot], ybuf.at[slot2], sem.at[slot2]
    rc = tm // COMBINE_CHUNKS
    per = 2 * tm // COMBINE_CHUNKS
    for c in range(COMBINE_CHUNKS):
        rows = slice(c * rc, (c + 1) * rc)
        w = w_ref[rows, :]
        y0 = jnp.concatenate(_unpack_bf16_pair(yb[c * rc:(c + 1) * rc, :]), axis=-1)
        y1 = jnp.concatenate(_unpack_bf16_pair(yb[tm + c * rc:tm + (c + 1) * rc, :]), axis=-1)
        x = x1_ref[rows, :] + (w[:, 0:1] * y0 + w[:, 1:2] * y1)
        ms = jnp.mean(x * x, axis=-1, keepdims=True)
        o_ref[rows, :] = x * lax.rsqrt(ms + EPS) * g
        _start_row_gather(y_hbm, dn_ref, nxt, nsem, c * per, (c + 1) * per)

    @pl.when(i == n - 1)
    def _():
        _wait_rows(ybuf.at[slot1], sem.at[slot1])
        _wait_rows(nxt, nsem)


def _combine(dest3, y, x1, e_w, g, tm):
    T, D = x1.shape
    NT = T // tm
    return pl.pallas_call(
        _combine_body,
        grid=(NT,),
        in_specs=[
            pl.BlockSpec((None, 1, 2 * tm), lambda i: (0, 0, 0), memory_space=pltpu.SMEM),
            pl.BlockSpec((None, 1, 2 * tm), lambda i: (min(1, NT - 1), 0, 0), memory_space=pltpu.SMEM),
            pl.BlockSpec((None, 1, 2 * tm), lambda i: (jnp.minimum(i + 2, NT - 1), 0, 0),
                         memory_space=pltpu.SMEM),
            pl.BlockSpec(memory_space=pl.ANY),
            pl.BlockSpec((tm, D), lambda i: (i, 0)),
            pl.BlockSpec((tm, TOP_K), lambda i: (i, 0)),
            pl.BlockSpec((1, D), lambda i: (0, 0)),
        ],
        out_specs=pl.BlockSpec((tm, D), lambda i: (i, 0)),
        out_shape=jax.ShapeDtypeStruct((T, D), F32),
        scratch_shapes=[pltpu.VMEM((COMBINE_SLOTS, 2 * tm, y.shape[1]), y.dtype),
                        pltpu.SemaphoreType.DMA((COMBINE_SLOTS,))],
        compiler_params=_params(("arbitrary",)),
        name="combine",
    )(dest3, dest3, dest3, y, x1, e_w, g)


def _invert_body(lo_ref, hi_ref, dest_ref, tok_ref):
    i = pl.program_id(0)
    blk = dest_ref.shape[1]

    @pl.when(i == 0)
    def _():
        def clear(j, carry):
            tok_ref[j] = 0
            return carry

        for s in range(lo_ref.shape[0]):
            lax.fori_loop(lo_ref[s], hi_ref[s], clear, 0)

    shift = TOP_K.bit_length() - 1
    n = INVERT_UNROLL

    def place(c, carry):
        j0 = pl.multiple_of(c * n, n)
        tok0 = lax.shift_right_logical(i * blk + j0, shift)
        for u in range(n):
            tok_ref[dest_ref[0, j0 + u]] = tok0 + (u >> shift)
        return carry

    lax.fori_loop(0, blk // n, place, 0)


def _invert(hole_lo, hole_hi, dest_flat, n_rows):
    assert TOP_K & (TOP_K - 1) == 0
    A = dest_flat.shape[0]
    blk = min(INVERT_BLOCK, A)
    grid_spec = pltpu.PrefetchScalarGridSpec(
        num_scalar_prefetch=2,
        grid=(A // blk,),
        in_specs=[pl.BlockSpec((None, 1, blk), lambda i, lo, hi: (i, 0, 0), memory_space=pltpu.SMEM)],
        out_specs=pl.BlockSpec(memory_space=pltpu.SMEM),
    )
    return pl.pallas_call(
        _invert_body,
        grid_spec=grid_spec,
        out_shape=jax.ShapeDtypeStruct((n_rows,), I32),
        compiler_params=_params(("arbitrary",)),
        name="invert",
    )(hole_lo, hole_hi, dest_flat.reshape(A // blk, 1, blk))


def _layout(route, counts, rows):
    T = route.shape[1]
    A = T * TOP_K
    e_id = route[0:TOP_K].T.astype(I32)
    e_w = route[TOP_K:2 * TOP_K].T
    rank = route[2 * TOP_K:3 * TOP_K].T.astype(I32)
    counts = counts[N_GROUPS:N_GROUPS + N_EXPERTS, 0].astype(I32)
    padded = (counts + rows - 1) // rows * rows
    pad_end = jnp.cumsum(padded)
    pad_start = pad_end - padded
    experts = jnp.arange(N_EXPERTS, dtype=I32)
    start_of = jnp.sum(jnp.where(e_id[:, :, None] == experts, pad_start, 0), axis=-1)
    dest = start_of + rank
    n_blocks = (A + N_EXPERTS * (rows - 1)) // rows
    P = n_blocks * rows
    hole_lo = jnp.concatenate([pad_start + counts, pad_end[-1:]])
    hole_hi = jnp.concatenate([pad_end, jnp.full((1,), P, I32)])
    tok = _invert(hole_lo, hole_hi, dest.reshape(A), P)
    n_used = pad_end[-1:] // rows
    block_start = jnp.minimum(jnp.arange(n_blocks, dtype=I32), n_used - 1) * rows
    block_e = jnp.sum((pad_end[None, :] <= block_start[:, None]).astype(I32), axis=1)
    return e_w, dest, tok, block_e, n_used, n_blocks


def kernel(x, norm_mix_g, w_in, ret_norm_g, conv_w, conv_b, w_rg, b_rg, w_ig, b_ig, lru_lambda,
           lru_norm_g, w_out, norm_ffn_g, w_group, b_group, w_router, b_router, w_gate, w_up,
           w_down, norm_final_g):
    B, S, D = x.shape
    T = B * S
    depth = norm_mix_g.shape[0]
    assert depth == 1, "the combine kernel fuses the final norm, so only one layer is supported"
    H, d = RET_HEADS, HEAD_DIM

    half = d // 2
    inv = ROPE_BASE ** (-jnp.arange(half, dtype=F32) / half)
    ang = jnp.arange(S, dtype=F32)[:, None] * inv[None, :]
    cos_t = jnp.concatenate([jnp.cos(ang), jnp.cos(ang)], axis=-1)
    sin_t = jnp.concatenate([-jnp.sin(ang), jnp.sin(ang)], axis=-1)
    log_gamma = jnp.log1p(-(2.0 ** (-5.0 - jnp.arange(H, dtype=F32))))

    x2d = x.reshape(T, D)
    for l in range(depth):
        proj = _inproj(x2d, norm_mix_g[l][None, :], w_in[l].astype(BF16), INPROJ_TM, INPROJ_TN)
        proj3 = proj.reshape(B, S, proj.shape[1])
        ret, wg_bf16 = _retention(proj3, log_gamma, cos_t, sin_t, ret_norm_g[l][None, :], w_gate[l])
        w_gates = (-LOG2_E * jnp.concatenate([w_rg[l], w_ig[l]], axis=-1)).astype(BF16)
        lru, wu_bf16 = _rglru(proj3, conv_w[l], conv_b[l][None, :], w_gates, b_rg[l][None, :],
                              b_ig[l][None, :], lru_lambda[l][None, :], lru_norm_g[l][None, :],
                              w_up[l])

        n_route = N_GROUPS + N_EXPERTS
        wr = jnp.concatenate([w_group[l], w_router[l], jnp.zeros((D, LANES - n_route), F32)], axis=-1)
        br = jnp.concatenate([b_group[l], b_router[l], jnp.zeros((LANES - n_route,), F32)])[None, :]
        wr_hi = wr.astype(BF16)
        wr_lo = (wr - wr_hi.astype(F32)).astype(BF16)
        x1, h2, logits_t = _outproj(ret.reshape(T, -1), lru.reshape(T, -1), x2d, w_out[l].astype(BF16),
                                    norm_ffn_g[l][None, :], jnp.concatenate([wr_hi, wr_lo], axis=1),
                                    br, OUTPROJ_TM)
        route, counts = _route(logits_t, ROUTE_TM)

        e_w, dest, tok, block_e, n_used, n_blocks = _layout(route, counts, MOE_ROWS)
        y = _moe(block_e, n_used, tok.reshape(n_blocks, 1, MOE_ROWS), h2, wg_bf16, wu_bf16, w_down[l])
        dest3 = dest.reshape(T // COMBINE_TM, COMBINE_TM, TOP_K).transpose(0, 2, 1).reshape(
            T // COMBINE_TM, 1, TOP_K * COMBINE_TM)
        x2d = _combine(dest3, y, x1, e_w, norm_final_g[None, :], COMBINE_TM)
    return x2d.reshape(B, S, D)
```
